```python
import jax, jax.numpy as jnp
from jax import lax
import numpy as np

D_MODEL = 1024
BATCH = 16
SEQ = 256
DEPTH = 2
DEC_BATCH = 4
DEC_SEQ = 1024
PAST_LEN = 256

GRID_W = 64
CONV_W = 512
N_HEADS = 8
HEAD_DIM = 64
ATTN_W = N_HEADS * HEAD_DIM
POOL_W = 512
POOL_SIZES = (2, 4, 8, 16)
POOL_GROUP = POOL_W // len(POOL_SIZES)
N_BRANCH = 3
BRANCH_W = 512
WIN_H_MAX = 8
WIN_W = 16
Q_BLK_W = 16
K_BAND_W = Q_BLK_W + WIN_W
CTX_Q_BLOCK = 128
D_FF = 2816
CONV_K = 3
EPS = 1e-6
IN_W = 3 * CONV_W + 3 * ATTN_W + POOL_W + N_BRANCH * D_MODEL

kernel_name = "hybrid_diffusion_prefix_trunk_step"

F32 = jnp.float32


def rmsnorm(x, g):
    xf = x.astype(F32)
    y = xf * lax.rsqrt(jnp.mean(xf * xf, axis=-1, keepdims=True) + EPS)
    return (y * g.astype(F32)).astype(x.dtype)


def dwconv3(x, w):
    xp = jnp.pad(x, ((0, 0), (1, 1), (0, 0)))
    return xp[:, :-2] * w[0] + xp[:, 1:-1] * w[1] + xp[:, 2:] * w[2]


def adaln(cvec, w_mod, b_mod):
    m = jax.nn.silu(cvec) @ w_mod + b_mod
    return jnp.split(m[:, None, :], 6, axis=-1)


def multiscale_pool(p, pool_w, pool_scale):
    Bn, N, _ = p.shape
    pf = p.astype(F32)
    cs = jnp.concatenate([jnp.zeros((Bn, 1, POOL_W), F32), jnp.cumsum(pf, axis=1)], axis=1)
    t = np.arange(N)
    outs = []
    for gi, w in enumerate(POOL_SIZES):
        lo = np.clip(t - w // 2, 0, N)
        hi = np.clip(t - w // 2 + w, 0, N)
        cnt = (hi - lo).astype(np.float32)[None, :, None]
        sl = slice(gi * POOL_GROUP, (gi + 1) * POOL_GROUP)
        mean = (cs[:, hi, sl] - cs[:, lo, sl]) / cnt
        outs.append(mean - pf[:, :, sl])
    d = jnp.stack(outs, axis=2).astype(p.dtype)
    y = jnp.einsum('bngc,gce->bnge', d, pool_w).reshape(Bn, N, POOL_W)
    return y * pool_scale


def ctx_attention(q, k, v):
    Bn, L = q.shape[:2]
    nb = L // CTX_Q_BLOCK
    qb = q.reshape(Bn, nb, CTX_Q_BLOCK, N_HEADS, HEAD_DIM).transpose(1, 0, 2, 3, 4)
    scale = HEAD_DIM ** -0.5

    def one(qblk):
        s = jnp.einsum('bqhd,bkhd->bhqk', qblk, k).astype(F32) * scale
        pr = jax.nn.softmax(s, axis=-1).astype(v.dtype)
        return jnp.einsum('bhqk,bkhd->bqhd', pr, v)

    o = lax.map(one, qb)
    return o.transpose(1, 0, 2, 3, 4).reshape(Bn, L, ATTN_W)


def neighbourhood_attention(q, k, v, k_ctx, v_ctx, rpb):
    Bn, S = q.shape[:2]
    rows = S // GRID_W
    kh = min(WIN_H_MAX, rows)
    ncb = GRID_W // Q_BLK_W
    r = np.arange(rows)
    row_idx = np.clip(r - kh // 2, 0, rows - kh)[:, None] + np.arange(kh)
    j = np.arange(ncb)
    col_idx = np.clip(j * Q_BLK_W - WIN_W // 2, 0, GRID_W - K_BAND_W)[:, None] + np.arange(K_BAND_W)
    qc = j[:, None] * Q_BLK_W + np.arange(Q_BLK_W)
    col_start = np.clip(qc - WIN_W // 2, 0, GRID_W - WIN_W)[..., None]
    kc = col_idx[:, None, :]
    valid = (kc >= col_start) & (kc < col_start + WIN_W)
    dr = row_idx - r[:, None] + WIN_H_MAX - 1
    dc = np.clip(kc - qc[..., None] + WIN_W - 1, 0, 2 * WIN_W - 2)
    bias = rpb[:, dr[:, None, None, :, None], dc[None, :, :, None, :]].astype(F32)

    qg = q.reshape(Bn, rows, ncb, Q_BLK_W, N_HEADS, HEAD_DIM)
    kgrid = k.reshape(Bn, rows, GRID_W, N_HEADS, HEAD_DIM)
    vgrid = v.reshape(Bn, rows, GRID_W, N_HEADS, HEAD_DIM)
    kg = kgrid[:, row_idx][:, :, :, col_idx]
    vg = vgrid[:, row_idx][:, :, :, col_idx]
    scale = HEAD_DIM ** -0.5
    s_lat = jnp.einsum('brjqhd,brkjchd->bhrjqkc', qg, kg).astype(F32) * scale + bias[None]
    s_lat = jnp.where(valid[None, None, None, :, :, None, :], s_lat, -jnp.inf)
    s_ctx = jnp.einsum('brjqhd,bhld->bhrjql', qg, k_ctx).astype(F32) * scale
    nlat = kh * K_BAND_W
    s = jnp.concatenate([s_lat.reshape(s_lat.shape[:5] + (nlat,)), s_ctx], axis=-1)
    pr = jax.nn.softmax(s, axis=-1).astype(v.dtype)
    p_lat = pr[..., :nlat].reshape(s_lat.shape)
    p_ctx = pr[..., nlat:]
    o = (jnp.einsum('bhrjqkc,brkjchd->brjqhd', p_lat, vg)
         + jnp.einsum('bhrjql,bhld->brjqhd', p_ctx, v_ctx))
    return o.reshape(Bn, S, ATTN_W)


def token_mixer(h, w_in, conv_w, pool_w, pool_scale, w_branch, w_out, attend):
    Bn, N, _ = h.shape
    z = h @ w_in
    sizes = [CONV_W] * 3 + [ATTN_W] * 3 + [POOL_W]
    pts, acc = [], 0
    for sz in sizes:
        acc += sz
        pts.append(acc)
    b_g, c_g, hc, q, k, v, pz, gz = jnp.split(z, pts, axis=-1)
    y_conv = b_g * dwconv3(c_g * hc, conv_w)
    q = q.reshape(Bn, N, N_HEADS, HEAD_DIM)
    k = k.reshape(Bn, N, N_HEADS, HEAD_DIM)
    v = v.reshape(Bn, N, N_HEADS, HEAD_DIM)
    y_attn = attend(q, k, v)
    y_pool = multiscale_pool(pz, pool_w, pool_scale)
    ys = jnp.stack([y_conv, y_attn, y_pool], axis=2)
    proj = jnp.einsum('bnic,icd->bnid', ys, w_branch)
    gates = jax.nn.sigmoid(gz.astype(F32)).astype(h.dtype).reshape(Bn, N, N_BRANCH, D_MODEL)
    merged = jnp.sum(gates * proj, axis=2)
    return merged @ w_out, k.transpose(0, 2, 1, 3), v.transpose(0, 2, 1, 3)


def conv_glu(h, w_up, conv, w_down):
    u, val = jnp.split(h @ w_up, 2, axis=-1)
    u = dwconv3(u, conv)
    return (jax.nn.gelu(u, approximate=True) * val) @ w_down


def trunk_layer(x, mod, g1n, g2n, w_in, conv_w, pool_w, pool_scale, w_branch, w_out,
                w_up, f_conv, w_down, attend):
    sh1, sc1, gt1, sh2, sc2, gt2 = mod
    h = rmsnorm(x, g1n) * (1 + sc1) + sh1
    mix, k, v = token_mixer(h, w_in, conv_w, pool_w, pool_scale, w_branch, w_out, attend)
    x = x + gt1 * mix
    h = rmsnorm(x, g2n) * (1 + sc2) + sh2
    x = x + gt2 * conv_glu(h, w_up, f_conv, w_down)
    return x, k, v


def setup_inputs(seed: int = 0) -> dict:
    key = jax.random.key(seed)
    ks = jax.random.split(key, 24)
    nrm = jax.random.normal
    D = D_MODEL
    return {
        "x_prompt": nrm(ks[0], (BATCH, SEQ, D), F32),
        "x_sample": nrm(ks[1], (DEC_BATCH, DEC_SEQ, D), F32),
        "cache_kv": nrm(ks[2], (DEC_BATCH, DEPTH, 2, N_HEADS, PAST_LEN, HEAD_DIM), F32),
        "c": nrm(ks[3], (DEC_BATCH, D), F32),
        "c_ctx": nrm(ks[4], (D,), F32),
        "w_mod": nrm(ks[5], (DEPTH, D, 6 * D), F32) * (0.5 * D ** -0.5),
        "b_mod": nrm(ks[6], (DEPTH, 6 * D), F32) * 0.02,
        "g_norm1": 1.0 + 0.02 * nrm(ks[7], (DEPTH, D), F32),
        "g_norm2": 1.0 + 0.02 * nrm(ks[8], (DEPTH, D), F32),
        "w_in": nrm(ks[9], (DEPTH, D, IN_W), F32) * D ** -0.5,
        "conv_w": nrm(ks[10], (DEPTH, CONV_K, CONV_W), F32) * CONV_K ** -0.5,
        "rpb": nrm(ks[11], (DEPTH, N_HEADS, 2 * WIN_H_MAX - 1, 2 * WIN_W - 1), F32) * 0.1,
        "pool_w": nrm(ks[12], (DEPTH, len(POOL_SIZES), POOL_GROUP, POOL_GROUP), F32) * POOL_GROUP ** -0.5,
        "pool_scale": 1.0 + 0.02 * nrm(ks[13], (DEPTH, POOL_W), F32),
        "w_branch": nrm(ks[14], (DEPTH, N_BRANCH, BRANCH_W, D), F32) * BRANCH_W ** -0.5,
        "w_out": nrm(ks[15], (DEPTH, D, D), F32) * D ** -0.5,
        "ffn_w_up": nrm(ks[16], (DEPTH, D, 2 * D_FF), F32) * D ** -0.5,
        "ffn_conv": nrm(ks[17], (DEPTH, CONV_K, D_FF), F32) * CONV_K ** -0.5,
        "ffn_w_down": nrm(ks[18], (DEPTH, D_FF, D), F32) * D_FF ** -0.5,
        "g_final": 1.0 + 0.02 * nrm(ks[19], (D,), F32),
    }


def reference(x_prompt, x_sample, cache_kv, c, c_ctx, w_mod, b_mod, g_norm1, g_norm2, w_in,
              conv_w, rpb, pool_w, pool_scale, w_branch, w_out, ffn_w_up, ffn_conv, ffn_w_down,
              g_final):
    xp = x_prompt
    xs = x_sample
    kv_layers = []
    for l in range(DEPTH):
        shared = (g_norm1[l], g_norm2[l], w_in[l], conv_w[l], pool_w[l], pool_scale[l],
                  w_branch[l], w_out[l], ffn_w_up[l], ffn_conv[l], ffn_w_down[l])
        mod_ctx = adaln(c_ctx[None, :], w_mod[l], b_mod[l])
        xp, k_new, v_new = trunk_layer(xp, mod_ctx, *shared, attend=ctx_attention)
        kv_layers.append(jnp.stack([k_new, v_new], axis=1))
        mod_lat = adaln(c, w_mod[l], b_mod[l])
        k_ctx = cache_kv[:, l, 0]
        v_ctx = cache_kv[:, l, 1]
        rb = rpb[l]
        attend_lat = lambda q, k, v, kc=k_ctx, vc=v_ctx, rb=rb: neighbourhood_attention(q, k, v, kc, vc, rb)
        xs, _, _ = trunk_layer(xs, mod_lat, *shared, attend=attend_lat)
    y_prompt = rmsnorm(xp, g_final)
    y_sample = rmsnorm(xs, g_final)
    kv_state = jnp.stack(kv_layers, axis=1)
    return (y_prompt, y_sample, kv_state)
```

```python
import functools

import numpy as np
import jax
import jax.numpy as jnp
from jax import lax
from jax.experimental import pallas as pl
from jax.experimental.pallas import tpu as pltpu

F32 = jnp.float32
BF16 = jnp.bfloat16

D_MODEL = 1024
N_CTX_SEQ = 16
CTX_LEN = 256
DEPTH = 2
N_LAT_SEQ = 4
LAT_LEN = 1024
PAST_LEN = 256
GRID_W = 64
GRID_ROWS = LAT_LEN // GRID_W
CONV_W = 512
N_HEADS = 8
HEAD_DIM = 64
ATTN_W = N_HEADS * HEAD_DIM
POOL_W = 512
POOL_SIZES = (2, 4, 8, 16)
POOL_GROUP = POOL_W // len(POOL_SIZES)
N_BRANCH = 3
WIN_H = 8
WIN_W = 16
D_FF = 2816
EPS = 1e-6
IN_W = 3 * CONV_W + 3 * ATTN_W + POOL_W + N_BRANCH * D_MODEL

LANES = 128
TOK_TILE = 1024
IN_CHUNK = 512
N_IN_CHUNKS = IN_W // IN_CHUNK
FF_CHUNK = 256
N_FF_CHUNKS = D_FF // FF_CHUNK
MIX_TILE = 512
VMEM_LIMIT = 56 * 1024 * 1024


def _params(*sem):
    return pltpu.CompilerParams(dimension_semantics=sem, vmem_limit_bytes=VMEM_LIMIT)


def _norm_mod(x, g, shift, scale):
    ms = jnp.mean(x * x, axis=-1, keepdims=True)
    return (x * lax.rsqrt(ms + EPS) * g) * (1.0 + scale) + shift


def _row_in_seq(shape, seq_len):
    return lax.broadcasted_iota(jnp.int32, shape, 0) & (seq_len - 1)


def _shift_rows(x, s, row, seq_len):
    y = pltpu.roll(x, s % x.shape[0], 0)
    src = row - s
    return jnp.where((src >= 0) & (src < seq_len), y, 0.0)


def _dwconv3(x, w_ref, row, seq_len):
    return (_shift_rows(x, 1, row, seq_len) * w_ref[0:1, :] + x * w_ref[1:2, :]
            + _shift_rows(x, -1, row, seq_len) * w_ref[2:3, :])


def _pool_delta(p, seq_len):
    row = _row_in_seq((p.shape[0], POOL_GROUP), seq_len)
    outs = []
    for gi, w in enumerate(POOL_SIZES):
        pg = p[:, gi * POOL_GROUP:(gi + 1) * POOL_GROUP]
        acc = pg
        for off in range(-(w // 2), w - w // 2):
            if off != 0:
                acc = acc + _shift_rows(pg, -off, row, seq_len)
        lo = jnp.maximum(row - w // 2, 0)
        hi = jnp.minimum(row - w // 2 + w, seq_len)
        outs.append(acc / (hi - lo).astype(F32) - pg)
    return jnp.concatenate(outs, axis=-1)


def _adaln_kernel(c_ref, w_ref, b_ref, o_ref):
    c = c_ref[...]
    s = c * jax.nn.sigmoid(c)
    o_ref[0] = jnp.dot(s.astype(BF16), w_ref[0].astype(BF16), preferred_element_type=F32) + b_ref[0]


def _adaln(cvec, w_mod, b_mod):
    nb = cvec.shape[0]
    n_out = 6 * D_MODEL
    blk = 1024
    return pl.pallas_call(
        _adaln_kernel,
        grid=(DEPTH, n_out // blk),
        in_specs=[
            pl.BlockSpec((nb, D_MODEL), lambda l, n: (0, 0)),
            pl.BlockSpec((1, D_MODEL, blk), lambda l, n: (l, 0, n)),
            pl.BlockSpec((1, 1, blk), lambda l, n: (l, 0, n)),
        ],
        out_specs=pl.BlockSpec((1, nb, blk), lambda l, n: (l, 0, n)),
        out_shape=jax.ShapeDtypeStruct((DEPTH, nb, n_out), F32),
        compiler_params=_params("arbitrary", "arbitrary"),
        name="adaln",
    )(cvec, w_mod, b_mod.reshape(DEPTH, 1, n_out))


def _inproj_kernel(x_ref, mod_ref, g_ref, w_ref, cw_ref, yconv_ref, qkv_ref, d_ref, gates_ref,
                   *rest, seq_len, with_kv):
    if with_kv:
        kv_ref, h_scr, b_scr, c_scr = rest
    else:
        h_scr, b_scr, c_scr = rest
    j = pl.program_id(1)

    @pl.when(j == 0)
    def _():
        h = _norm_mod(x_ref[...], g_ref[...], mod_ref[0, 0:1, :], mod_ref[0, 1:2, :])
        h_scr[...] = h.astype(BF16)

    z = jnp.dot(h_scr[...], w_ref[...], preferred_element_type=F32)

    @pl.when(j == 0)
    def _():
        b_scr[...] = z

    @pl.when(j == 1)
    def _():
        c_scr[...] = z

    @pl.when(j == 2)
    def _():
        row = _row_in_seq(z.shape, seq_len)
        conv = _dwconv3(c_scr[...] * z, cw_ref, row, seq_len)
        yconv_ref[...] = (b_scr[...] * conv).astype(BF16)

    @pl.when(j == 3)
    def _():
        qkv_ref[...] = (z * (HEAD_DIM ** -0.5)).astype(BF16)

    @pl.when((j == 4) | (j == 5))
    def _():
        qkv_ref[...] = z.astype(BF16)
        if with_kv:
            for s in range(z.shape[0] // seq_len):
                for h in range(N_HEADS):
                    kv_ref[s, 0, h] = z[s * seq_len:(s + 1) * seq_len, h * HEAD_DIM:(h + 1) * HEAD_DIM]

    @pl.when(j == 6)
    def _():
        d_ref[...] = _pool_delta(z, seq_len).astype(BF16)

    @pl.when(j >= 7)
    def _():
        gates_ref[...] = jax.nn.sigmoid(z).astype(BF16)


def _inproj(x, mod, g1, w_in, conv_w, *, seq_len, with_kv):
    ntok = x.shape[0]
    nt = ntok // TOK_TILE
    per_tile_mod = mod.shape[0] > 1
    mod_map = (lambda i, j: (i, 0, 0)) if per_tile_mod else (lambda i, j: (0, 0, 0))
    out_shape = [
        jax.ShapeDtypeStruct((ntok, CONV_W), BF16),
        jax.ShapeDtypeStruct((ntok, 3 * ATTN_W), BF16),
        jax.ShapeDtypeStruct((ntok, POOL_W), BF16),
        jax.ShapeDtypeStruct((ntok, N_BRANCH * D_MODEL), BF16),
    ]
    out_specs = [
        pl.BlockSpec((TOK_TILE, IN_CHUNK), lambda i, j: (i, 0)),
        pl.BlockSpec((TOK_TILE, IN_CHUNK), lambda i, j: (i, jnp.clip(j - 3, 0, 2))),
        pl.BlockSpec((TOK_TILE, IN_CHUNK), lambda i, j: (i, 0)),
        pl.BlockSpec((TOK_TILE, IN_CHUNK), lambda i, j: (i, jnp.clip(j - 7, 0, 5))),
    ]
    if with_kv:
        spt = TOK_TILE // seq_len
        out_shape.append(jax.ShapeDtypeStruct((ntok // seq_len, 2, N_HEADS, seq_len, HEAD_DIM), F32))
        out_specs.append(pl.BlockSpec((spt, 1, N_HEADS, seq_len, HEAD_DIM),
                                      lambda i, j: (i, jnp.clip(j - 4, 0, 1), 0, 0, 0)))
    return pl.pallas_call(
        functools.partial(_inproj_kernel, seq_len=seq_len, with_kv=with_kv),
        grid=(nt, N_IN_CHUNKS),
        in_specs=[
            pl.BlockSpec((TOK_TILE, D_MODEL), lambda i, j: (i, 0)),
            pl.BlockSpec((1, 6, D_MODEL), mod_map),
            pl.BlockSpec((1, D_MODEL), lambda i, j: (0, 0)),
            pl.BlockSpec((D_MODEL, IN_CHUNK), lambda i, j: (0, j)),
            pl.BlockSpec((3, CONV_W), lambda i, j: (0, 0)),
        ],
        out_specs=out_specs,
        out_shape=out_shape,
        scratch_shapes=[
            pltpu.VMEM((TOK_TILE, D_MODEL), BF16),
            pltpu.VMEM((TOK_TILE, IN_CHUNK), F32),
            pltpu.VMEM((TOK_TILE, IN_CHUNK), F32),
        ],
        compiler_params=_params("arbitrary", "arbitrary"),
        name="inproj_ctx" if with_kv else "inproj_lat",
    )(x, mod, g1, w_in, conv_w)


def _attend_pair(q2, kvb):
    lane = lax.broadcasted_iota(jnp.int32, (1, LANES), 1)
    out = None
    for half in range(2):
        m = (lane >= half * HEAD_DIM) & (lane < (half + 1) * HEAD_DIM)
        qh = jnp.where(m, q2, jnp.zeros_like(q2))
        scores = []
        for k2, _, bias in kvb:
            s = lax.dot_general(qh, k2, (((1,), (1,)), ((), ())), preferred_element_type=F32)
            if bias is not None:
                s = s + bias[half]
            scores.append(s)
        mx = functools.reduce(jnp.maximum, [jnp.max(s, axis=-1, keepdims=True) for s in scores])
        es = [jnp.exp(s - mx) for s in scores]
        den = functools.reduce(lambda a, b: a + b, [jnp.sum(e, axis=-1, keepdims=True) for e in es])
        inv = 1.0 / den
        for e, (_, v2, _) in zip(es, kvb):
            vh = jnp.where(m, v2, jnp.zeros_like(v2))
            o = jnp.dot((e * inv).astype(BF16), vh, preferred_element_type=F32)
            out = o if out is None else out + o
    return out


def _ctx_attn_kernel(q_ref, k_ref, v_ref, o_ref):
    for hp in range(N_HEADS // 2):
        sl = slice(hp * LANES, (hp + 1) * LANES)
        o_ref[:, sl] = _attend_pair(q_ref[:, sl], [(k_ref[:, sl], v_ref[:, sl], None)]).astype(BF16)


def _ctx_attention(qkv):
    ntok = qkv.shape[0]
    return pl.pallas_call(
        _ctx_attn_kernel,
        grid=(ntok // CTX_LEN,),
        in_specs=[
            pl.BlockSpec((CTX_LEN, ATTN_W), lambda b: (b, 0)),
            pl.BlockSpec((CTX_LEN, ATTN_W), lambda b: (b, 1)),
            pl.BlockSpec((CTX_LEN, ATTN_W), lambda b: (b, 2)),
        ],
        out_specs=pl.BlockSpec((CTX_LEN, ATTN_W), lambda b: (b, 0)),
        out_shape=jax.ShapeDtypeStruct((ntok, ATTN_W), BF16),
        compiler_params=_params("arbitrary"),
        name="attn_ctx",
    )(qkv, qkv, qkv)


def _lat_attn_kernel(q_ref, k_ref, v_ref, kc_ref, vc_ref, bias_ref, o_ref):
    qr = pl.program_id(1)
    row0 = jnp.clip(qr - WIN_H // 2, 0, GRID_ROWS - WIN_H)
    start = pl.multiple_of(row0 * GRID_W, GRID_W)
    nkey = WIN_H * GRID_W
    for hp in range(N_HEADS // 2):
        sl = slice(hp * LANES, (hp + 1) * LANES)
        kvb = [
            (k_ref[pl.ds(start, nkey), sl], v_ref[pl.ds(start, nkey), sl],
             (bias_ref[0, 2 * hp], bias_ref[0, 2 * hp + 1])),
            (kc_ref[0, :, sl], vc_ref[0, :, sl], None),
        ]
        o_ref[:, sl] = _attend_pair(q_ref[:, sl], kvb).astype(BF16)


def _row_offset_index(qr):
    return qr - jnp.clip(qr - WIN_H // 2, 0, GRID_ROWS - WIN_H)


def _lat_attention(qkv, k_ctx, v_ctx, bias):
    ntok = qkv.shape[0]
    nseq = ntok // LAT_LEN
    return pl.pallas_call(
        _lat_attn_kernel,
        grid=(nseq, GRID_ROWS),
        in_specs=[
            pl.BlockSpec((GRID_W, ATTN_W), lambda b, r: (b * GRID_ROWS + r, 0)),
            pl.BlockSpec((LAT_LEN, ATTN_W), lambda b, r: (b, 1)),
            pl.BlockSpec((LAT_LEN, ATTN_W), lambda b, r: (b, 2)),
            pl.BlockSpec((1, PAST_LEN, ATTN_W), lambda b, r: (b, 0, 0)),
            pl.BlockSpec((1, PAST_LEN, ATTN_W), lambda b, r: (b, 0, 0)),
            pl.BlockSpec((1, N_HEADS, GRID_W, WIN_H * GRID_W),
                         lambda b, r: (_row_offset_index(r), 0, 0, 0)),
        ],
        out_specs=pl.BlockSpec((GRID_W, ATTN_W), lambda b, r: (b * GRID_ROWS + r, 0)),
        out_shape=jax.ShapeDtypeStruct((ntok, ATTN_W), BF16),
        compiler_params=_params("arbitrary", "arbitrary"),
        name="attn_lat",
    )(qkv, qkv, qkv, k_ctx, v_ctx, bias)


def _neighbourhood_bias(rpb_l):
    o = np.arange(WIN_H)[:, None]
    i = np.arange(WIN_H)[None, :]
    dr = i - o + WIN_H - 1
    qc = np.arange(GRID_W)[:, None]
    kc = np.arange(GRID_W)[None, :]
    col_start = np.clip(qc - WIN_W // 2, 0, GRID_W - WIN_W)
    valid = (kc >= col_start) & (kc < col_start + WIN_W)
    dc = np.clip(kc - qc + WIN_W - 1, 0, 2 * WIN_W - 2)
    g = rpb_l[:, dr[:, None, :, None], dc[None, :, None, :]]
    g = jnp.where(valid[None, None, :, None, :], g, -jnp.inf)
    return g.transpose(1, 0, 2, 3, 4).reshape(WIN_H, N_HEADS, GRID_W, WIN_H * GRID_W)


def _mix_kernel(x_ref, mod_ref, yc_ref, ya_ref, d_ref, gates_ref, pw_ref, ps_ref, wb_ref, wo_ref, o_ref):
    yp = [jnp.dot(d_ref[:, g * POOL_GROUP:(g + 1) * POOL_GROUP], pw_ref[g], preferred_element_type=F32)
          for g in range(len(POOL_SIZES))]
    y_pool = (jnp.concatenate(yp, axis=-1) * ps_ref[...]).astype(BF16)
    merged = None
    for i, y in enumerate((yc_ref[...], ya_ref[...], y_pool)):
        proj = jnp.dot(y, wb_ref[i], preferred_element_type=F32)
        term = gates_ref[:, i * D_MODEL:(i + 1) * D_MODEL].astype(F32) * proj
        merged = term if merged is None else merged + term
    mix = jnp.dot(merged.astype(BF16), wo_ref[...], preferred_element_type=F32)
    o_ref[...] = x_ref[...] + mod_ref[0, 2:3, :] * mix


def _mix(x, mod, yconv, yattn, d, gates, pool_w, pool_scale, w_branch, w_out):
    ntok = x.shape[0]
    seq_tiles = ntok // mod.shape[0] // MIX_TILE
    row = lambda i: (i, 0)
    const2 = lambda i: (0, 0)
    const3 = lambda i: (0, 0, 0)
    return pl.pallas_call(
        _mix_kernel,
        grid=(ntok // MIX_TILE,),
        in_specs=[
            pl.BlockSpec((MIX_TILE, D_MODEL), row),
            pl.BlockSpec((1, 6, D_MODEL), lambda i: (i // seq_tiles, 0, 0)),
            pl.BlockSpec((MIX_TILE, CONV_W), row),
            pl.BlockSpec((MIX_TILE, ATTN_W), row),
            pl.BlockSpec((MIX_TILE, POOL_W), row),
            pl.BlockSpec((MIX_TILE, N_BRANCH * D_MODEL), row),
            pl.BlockSpec((len(POOL_SIZES), POOL_GROUP, POOL_GROUP), const3),
            pl.BlockSpec((1, POOL_W), const2),
            pl.BlockSpec((N_BRANCH, CONV_W, D_MODEL), const3),
            pl.BlockSpec((D_MODEL, D_MODEL), const2),
        ],
        out_specs=pl.BlockSpec((MIX_TILE, D_MODEL), row),
        out_shape=jax.ShapeDtypeStruct((ntok, D_MODEL), F32),
        compiler_params=_params("arbitrary"),
        name="mix",
    )(x, mod, yconv, yattn, d, gates, pool_w, pool_scale, w_branch, w_out)


def _ffn_kernel(x_ref, mod_ref, g2_ref, wu_ref, wv_ref, fc_ref, wd_ref, gf_ref, o_ref, h_scr,
                *, seq_len, final):
    j = pl.program_id(1)

    @pl.when(j == 0)
    def _():
        h = _norm_mod(x_ref[...], g2_ref[...], mod_ref[0, 3:4, :], mod_ref[0, 4:5, :])
        h_scr[...] = h.astype(BF16)

    h = h_scr[...]
    u = jnp.dot(h, wu_ref[...], preferred_element_type=F32)
    val = jnp.dot(h, wv_ref[...], preferred_element_type=F32)
    u = _dwconv3(u, fc_ref, _row_in_seq(u.shape, seq_len), seq_len)
    act = (jax.nn.gelu(u, approximate=True) * val).astype(BF16)
    part = jnp.dot(act, wd_ref[...], preferred_element_type=F32)

    @pl.when(j == 0)
    def _():
        o_ref[...] = part

    @pl.when(j > 0)
    def _():
        o_ref[...] += part

    @pl.when(j == N_FF_CHUNKS - 1)
    def _():
        xn = x_ref[...] + mod_ref[0, 5:6, :] * o_ref[...]
        if final:
            ms = jnp.mean(xn * xn, axis=-1, keepdims=True)
            xn = xn * lax.rsqrt(ms + EPS) * gf_ref[...]
        o_ref[...] = xn


def _ffn(x, mod, g2, w_up, f_conv, w_down, g_final, *, seq_len, final):
    ntok = x.shape[0]
    per_tile_mod = mod.shape[0] > 1
    mod_map = (lambda i, j: (i, 0, 0)) if per_tile_mod else (lambda i, j: (0, 0, 0))
    return pl.pallas_call(
        functools.partial(_ffn_kernel, seq_len=seq_len, final=final),
        grid=(ntok // TOK_TILE, N_FF_CHUNKS),
        in_specs=[
            pl.BlockSpec((TOK_TILE, D_MODEL), lambda i, j: (i, 0)),
            pl.BlockSpec((1, 6, D_MODEL), mod_map),
            pl.BlockSpec((1, D_MODEL), lambda i, j: (0, 0)),
            pl.BlockSpec((D_MODEL, FF_CHUNK), lambda i, j: (0, j)),
            pl.BlockSpec((D_MODEL, FF_CHUNK), lambda i, j: (0, N_FF_CHUNKS + j)),
            pl.BlockSpec((3, FF_CHUNK), lambda i, j: (0, j)),
            pl.BlockSpec((FF_CHUNK, D_MODEL), lambda i, j: (j, 0)),
            pl.BlockSpec((1, D_MODEL), lambda i, j: (0, 0)),
        ],
        out_specs=pl.BlockSpec((TOK_TILE, D_MODEL), lambda i, j: (i, 0)),
        out_shape=jax.ShapeDtypeStruct((ntok, D_MODEL), F32),
        scratch_shapes=[pltpu.VMEM((TOK_TILE, D_MODEL), BF16)],
        compiler_params=_params("arbitrary", "arbitrary"),
        name="ffn",
    )(x, mod, g2, w_up, w_up, f_conv, w_down, g_final)


def kernel(x_prompt, x_sample, cache_kv, c, c_ctx, w_mod, b_mod, g_norm1, g_norm2, w_in, conv_w, rpb,
           pool_w, pool_scale, w_branch, w_out, ffn_w_up, ffn_conv, ffn_w_down, g_final):
    n_mod = 8
    cvec = jnp.concatenate([c_ctx[None, :], c, jnp.zeros((n_mod - 1 - N_LAT_SEQ, D_MODEL), F32)], axis=0)
    mod = _adaln(cvec, w_mod, b_mod).reshape(DEPTH, n_mod, 6, D_MODEL)

    xp = x_prompt.reshape(N_CTX_SEQ * CTX_LEN, D_MODEL)
    xs = x_sample.reshape(N_LAT_SEQ * LAT_LEN, D_MODEL)
    gf = g_final.reshape(1, D_MODEL)
    kv_layers = []
    for l in range(DEPTH):
        w_in_l = w_in[l].astype(BF16)
        pool_w_l = pool_w[l].astype(BF16)
        pool_s_l = pool_scale[l].reshape(1, POOL_W)
        w_br_l = w_branch[l].astype(BF16)
        w_out_l = w_out[l].astype(BF16)
        w_up_l = ffn_w_up[l].astype(BF16)
        w_dn_l = ffn_w_down[l].astype(BF16)
        g1 = g_norm1[l].reshape(1, D_MODEL)
        g2 = g_norm2[l].reshape(1, D_MODEL)
        mod_ctx = mod[l, 0:1]
        mod_lat = mod[l, 1:1 + N_LAT_SEQ]
        final = l == DEPTH - 1

        yc, qkv, d, gates, kv = _inproj(xp, mod_ctx, g1, w_in_l, conv_w[l], seq_len=CTX_LEN, with_kv=True)
        ya = _ctx_attention(qkv)
        xp = _mix(xp, mod_ctx, yc, ya, d, gates, pool_w_l, pool_s_l, w_br_l, w_out_l)
        xp = _ffn(xp, mod_ctx, g2, w_up_l, ffn_conv[l], w_dn_l, gf, seq_len=CTX_LEN, final=final)
        kv_layers.append(kv)

        k_ctx = cache_kv[:, l, 0].transpose(0, 2, 1, 3).reshape(N_LAT_SEQ, PAST_LEN, ATTN_W).astype(BF16)
        v_ctx = cache_kv[:, l, 1].transpose(0, 2, 1, 3).reshape(N_LAT_SEQ, PAST_LEN, ATTN_W).astype(BF16)
        bias = _neighbourhood_bias(rpb[l])
        yc, qkv, d, gates = _inproj(xs, mod_lat, g1, w_in_l, conv_w[l], seq_len=LAT_LEN, with_kv=False)
        ya = _lat_attention(qkv, k_ctx, v_ctx, bias)
        xs = _mix(xs, mod_lat, yc, ya, d, gates, pool_w_l, pool_s_l, w_br_l, w_out_l)
        xs = _ffn(xs, mod_lat, g2, w_up_l, ffn_conv[l], w_dn_l, gf, seq_len=LAT_LEN, final=final)

    y_prompt = xp.reshape(N_CTX_SEQ, CTX_LEN, D_MODEL)
    y_sample = xs.reshape(N_LAT_SEQ, LAT_LEN, D_MODEL)
    kv_state = jnp.stack(kv_layers, axis=1)
    return (y_prompt, y_sample, kv_state)
```

```python
import functools

import numpy as np
import jax
import jax.numpy as jnp
from jax import lax
from jax.experimental import pallas as pl
from jax.experimental.pallas import tpu as pltpu

F32 = jnp.float32
BF16 = jnp.bfloat16

D_MODEL = 1024
N_CTX_SEQ = 16
CTX_LEN = 256
DEPTH = 2
N_LAT_SEQ = 4
LAT_LEN = 1024
PAST_LEN = 256
GRID_W = 64
GRID_ROWS = LAT_LEN // GRID_W
CONV_W = 512
N_HEADS = 8
HEAD_DIM = 64
ATTN_W = N_HEADS * HEAD_DIM
POOL_W = 512
POOL_SIZES = (2, 4, 8, 16)
POOL_GROUP = POOL_W // len(POOL_SIZES)
N_BRANCH = 3
WIN_H = 8
WIN_W = 16
D_FF = 2816
EPS = 1e-6
IN_W = 3 * CONV_W + 3 * ATTN_W + POOL_W + N_BRANCH * D_MODEL

LANES = 128
TOK_TILE = 1024
IN_CHUNK = 512
N_IN_CHUNKS = IN_W // IN_CHUNK
FF_CHUNK = 256
N_FF_CHUNKS = D_FF // FF_CHUNK
MIX_TILE = 512
VMEM_LIMIT = 56 * 1024 * 1024


def _params(*sem):
    return pltpu.CompilerParams(dimension_semantics=sem, vmem_limit_bytes=VMEM_LIMIT)


def _norm_mod(x, g, shift, scale):
    ms = jnp.mean(x * x, axis=-1, keepdims=True)
    return (x * lax.rsqrt(ms + EPS) * g) * (1.0 + scale) + shift


def _row_in_seq(shape, seq_len):
    return lax.broadcasted_iota(jnp.int32, shape, 0) & (seq_len - 1)


def _shift_rows(x, s, row, seq_len):
    y = pltpu.roll(x, s % x.shape[0], 0)
    src = row - s
    return jnp.where((src >= 0) & (src < seq_len), y, 0.0)


def _dwconv3(x, w_ref, row, seq_len):
    return (_shift_rows(x, 1, row, seq_len) * w_ref[0:1, :] + x * w_ref[1:2, :]
            + _shift_rows(x, -1, row, seq_len) * w_ref[2:3, :])


def _pool_delta(p, seq_len):
    row = _row_in_seq((p.shape[0], POOL_GROUP), seq_len)
    outs = []
    for gi, w in enumerate(POOL_SIZES):
        pg = p[:, gi * POOL_GROUP:(gi + 1) * POOL_GROUP]
        acc = pg
        for off in range(-(w // 2), w - w // 2):
            if off != 0:
                acc = acc + _shift_rows(pg, -off, row, seq_len)
        lo = jnp.maximum(row - w // 2, 0)
        hi = jnp.minimum(row - w // 2 + w, seq_len)
        outs.append(acc / (hi - lo).astype(F32) - pg)
    return jnp.concatenate(outs, axis=-1)


def _adaln_kernel(c_ref, w_ref, b_ref, o_ref):
    c = c_ref[...]
    s = c * jax.nn.sigmoid(c)
    o_ref[0] = jnp.dot(s.astype(BF16), w_ref[0].astype(BF16), preferred_element_type=F32) + b_ref[0]


def _adaln(cvec, w_mod, b_mod):
    nb = cvec.shape[0]
    n_out = 6 * D_MODEL
    blk = 1024
    return pl.pallas_call(
        _adaln_kernel,
        grid=(DEPTH, n_out // blk),
        in_specs=[
            pl.BlockSpec((nb, D_MODEL), lambda l, n: (0, 0)),
            pl.BlockSpec((1, D_MODEL, blk), lambda l, n: (l, 0, n)),
            pl.BlockSpec((1, 1, blk), lambda l, n: (l, 0, n)),
        ],
        out_specs=pl.BlockSpec((1, nb, blk), lambda l, n: (l, 0, n)),
        out_shape=jax.ShapeDtypeStruct((DEPTH, nb, n_out), F32),
        compiler_params=_params("arbitrary", "arbitrary"),
        name="adaln",
    )(cvec, w_mod, b_mod.reshape(DEPTH, 1, n_out))


def _inproj_kernel(x_ref, mod_ref, g_ref, w_ref, cw_ref, yconv_ref, qkv_ref, d_ref, gates_ref,
                   *rest, seq_len, with_kv):
    if with_kv:
        kv_ref, h_scr, b_scr, c_scr = rest
    else:
        h_scr, b_scr, c_scr = rest
    j = pl.program_id(1)

    @pl.when(j == 0)
    def _():
        h = _norm_mod(x_ref[...], g_ref[...], mod_ref[0, 0:1, :], mod_ref[0, 1:2, :])
        h_scr[...] = h.astype(BF16)

    z = jnp.dot(h_scr[...], w_ref[...], preferred_element_type=F32)

    @pl.when(j == 0)
    def _():
        b_scr[...] = z

    @pl.when(j == 1)
    def _():
        c_scr[...] = z

    @pl.when(j == 2)
    def _():
        row = _row_in_seq(z.shape, seq_len)
        conv = _dwconv3(c_scr[...] * z, cw_ref, row, seq_len)
        yconv_ref[...] = (b_scr[...] * conv).astype(BF16)

    @pl.when(j == 3)
    def _():
        qkv_ref[...] = (z * (HEAD_DIM ** -0.5)).astype(BF16)

    @pl.when((j == 4) | (j == 5))
    def _():
        qkv_ref[...] = z.astype(BF16)
        if with_kv:
            for s in range(z.shape[0] // seq_len):
                for h in range(N_HEADS):
                    kv_ref[s, 0, h] = z[s * seq_len:(s + 1) * seq_len, h * HEAD_DIM:(h + 1) * HEAD_DIM]

    @pl.when(j == 6)
    def _():
        d_ref[...] = _pool_delta(z, seq_len).astype(BF16)

    @pl.when(j >= 7)
    def _():
        gates_ref[...] = jax.nn.sigmoid(z).astype(BF16)


def _inproj(x, mod, g1, w_in, conv_w, *, seq_len, with_kv):
    ntok = x.shape[0]
    nt = ntok // TOK_TILE
    per_tile_mod = mod.shape[0] > 1
    mod_map = (lambda i, j: (i, 0, 0)) if per_tile_mod else (lambda i, j: (0, 0, 0))
    out_shape = [
        jax.ShapeDtypeStruct((ntok, CONV_W), BF16),
        jax.ShapeDtypeStruct((ntok, 3 * ATTN_W), BF16),
        jax.ShapeDtypeStruct((ntok, POOL_W), BF16),
        jax.ShapeDtypeStruct((ntok, N_BRANCH * D_MODEL), BF16),
    ]
    out_specs = [
        pl.BlockSpec((TOK_TILE, IN_CHUNK), lambda i, j: (i, 0)),
        pl.BlockSpec((TOK_TILE, IN_CHUNK), lambda i, j: (i, jnp.clip(j - 3, 0, 2))),
        pl.BlockSpec((TOK_TILE, IN_CHUNK), lambda i, j: (i, 0)),
        pl.BlockSpec((TOK_TILE, IN_CHUNK), lambda i, j: (i, jnp.clip(j - 7, 0, 5))),
    ]
    if with_kv:
        spt = TOK_TILE // seq_len
        out_shape.append(jax.ShapeDtypeStruct((ntok // seq_len, 2, N_HEADS, seq_len, HEAD_DIM), F32))
        out_specs.append(pl.BlockSpec((spt, 1, N_HEADS, seq_len, HEAD_DIM),
                                      lambda i, j: (i, jnp.clip(j - 4, 0, 1), 0, 0, 0)))
    return pl.pallas_call(
        functools.partial(_inproj_kernel, seq_len=seq_len, with_kv=with_kv),
        grid=(nt, N_IN_CHUNKS),
        in_specs=[
            pl.BlockSpec((TOK_TILE, D_MODEL), lambda i, j: (i, 0)),
            pl.BlockSpec((1, 6, D_MODEL), mod_map),
            pl.BlockSpec((1, D_MODEL), lambda i, j: (0, 0)),
            pl.BlockSpec((D_MODEL, IN_CHUNK), lambda i, j: (0, j)),
            pl.BlockSpec((3, CONV_W), lambda i, j: (0, 0)),
        ],
        out_specs=out_specs,
        out_shape=out_shape,
        scratch_shapes=[
            pltpu.VMEM((TOK_TILE, D_MODEL), BF16),
            pltpu.VMEM((TOK_TILE, IN_CHUNK), F32),
            pltpu.VMEM((TOK_TILE, IN_CHUNK), F32),
        ],
        compiler_params=_params("arbitrary", "arbitrary"),
        name="inproj_ctx" if with_kv else "inproj_lat",
    )(x, mod, g1, w_in, conv_w)


def _attend_pair(q2, kvb):
    lane = lax.broadcasted_iota(jnp.int32, (1, LANES), 1)
    out = None
    for half in range(2):
        m = (lane >= half * HEAD_DIM) & (lane < (half + 1) * HEAD_DIM)
        qh = jnp.where(m, q2, jnp.zeros_like(q2))
        scores = []
        for k2, _, bias in kvb:
            s = lax.dot_general(qh, k2, (((1,), (1,)), ((), ())), preferred_element_type=F32)
            if bias is not None:
                s = s + bias[half]
            scores.append(s)
        mx = functools.reduce(jnp.maximum, [jnp.max(s, axis=-1, keepdims=True) for s in scores])
        es = [jnp.exp(s - mx) for s in scores]
        den = functools.reduce(lambda a, b: a + b, [jnp.sum(e, axis=-1, keepdims=True) for e in es])
        inv = 1.0 / den
        for e, (_, v2, _) in zip(es, kvb):
            vh = jnp.where(m, v2, jnp.zeros_like(v2))
            o = jnp.dot((e * inv).astype(BF16), vh, preferred_element_type=F32)
            out = o if out is None else out + o
    return out


def _ctx_attn_kernel(q_ref, k_ref, v_ref, o_ref):
    for hp in range(N_HEADS // 2):
        sl = slice(hp * LANES, (hp + 1) * LANES)
        o_ref[:, sl] = _attend_pair(q_ref[:, sl], [(k_ref[:, sl], v_ref[:, sl], None)]).astype(BF16)


def _ctx_attention(qkv):
    ntok = qkv.shape[0]
    return pl.pallas_call(
        _ctx_attn_kernel,
        grid=(ntok // CTX_LEN,),
        in_specs=[
            pl.BlockSpec((CTX_LEN, ATTN_W), lambda b: (b, 0)),
            pl.BlockSpec((CTX_LEN, ATTN_W), lambda b: (b, 1)),
            pl.BlockSpec((CTX_LEN, ATTN_W), lambda b: (b, 2)),
        ],
        out_specs=pl.BlockSpec((CTX_LEN, ATTN_W), lambda b: (b, 0)),
        out_shape=jax.ShapeDtypeStruct((ntok, ATTN_W), BF16),
        compiler_params=_params("arbitrary"),
        name="attn_ctx",
    )(qkv, qkv, qkv)


N_ROW_PAIRS = 2 * WIN_H - 2


def _fill_bias(rp_ref, bias_scr):
    shape = (GRID_W, LANES)
    qc = lax.broadcasted_iota(jnp.int32, shape, 0)
    kc = lax.broadcasted_iota(jnp.int32, shape, 1) & (GRID_W - 1)
    col_start = jnp.clip(qc - WIN_W // 2, 0, GRID_W - WIN_W)
    valid = (kc >= col_start) & (kc < col_start + WIN_W)
    for h in range(N_HEADS):
        pairs = []
        for dr in range(N_ROW_PAIRS):
            t = jnp.broadcast_to(rp_ref[h, dr:dr + 1, :], shape)
            t = pltpu.roll(t, LANES - (WIN_W - 1), 1)
            for bit in range(6):
                t = jnp.where(((qc >> bit) & 1) == 1, pltpu.roll(t, 1 << bit, 1), t)
            pairs.append(jnp.where(valid, t, -jnp.inf))
        for o in range(WIN_H):
            for u in range(WIN_H // 2):
                bias_scr[o, h, :, u * LANES:(u + 1) * LANES] = pairs[2 * u - o + WIN_H - 1]


def _lat_attn_kernel(q_ref, k_ref, v_ref, kc_ref, vc_ref, rp_ref, o_ref, bias_scr):
    qr = pl.program_id(1)

    @pl.when((pl.program_id(0) == 0) & (qr == 0))
    def _():
        _fill_bias(rp_ref, bias_scr)

    row0 = jnp.clip(qr - WIN_H // 2, 0, GRID_ROWS - WIN_H)
    start = pl.multiple_of(row0 * GRID_W, GRID_W)
    off = qr - row0
    nkey = WIN_H * GRID_W
    for hp in range(N_HEADS // 2):
        sl = slice(hp * LANES, (hp + 1) * LANES)
        kvb = [
            (k_ref[pl.ds(start, nkey), sl], v_ref[pl.ds(start, nkey), sl],
             (bias_scr[off, 2 * hp], bias_scr[off, 2 * hp + 1])),
            (kc_ref[0, :, sl], vc_ref[0, :, sl], None),
        ]
        o_ref[:, sl] = _attend_pair(q_ref[:, sl], kvb).astype(BF16)


def _lat_attention(qkv, k_ctx, v_ctx, rp):
    ntok = qkv.shape[0]
    nseq = ntok // LAT_LEN
    return pl.pallas_call(
        _lat_attn_kernel,
        grid=(nseq, GRID_ROWS),
        in_specs=[
            pl.BlockSpec((GRID_W, ATTN_W), lambda b, r: (b * GRID_ROWS + r, 0)),
            pl.BlockSpec((LAT_LEN, ATTN_W), lambda b, r: (b, 1)),
            pl.BlockSpec((LAT_LEN, ATTN_W), lambda b, r: (b, 2)),
            pl.BlockSpec((1, PAST_LEN, ATTN_W), lambda b, r: (b, 0, 0)),
            pl.BlockSpec((1, PAST_LEN, ATTN_W), lambda b, r: (b, 0, 0)),
            pl.BlockSpec(rp.shape, lambda b, r: (0, 0, 0)),
        ],
        out_specs=pl.BlockSpec((GRID_W, ATTN_W), lambda b, r: (b * GRID_ROWS + r, 0)),
        out_shape=jax.ShapeDtypeStruct((ntok, ATTN_W), BF16),
        scratch_shapes=[pltpu.VMEM((WIN_H, N_HEADS, GRID_W, WIN_H * GRID_W), F32)],
        compiler_params=_params("arbitrary", "arbitrary"),
        name="attn_lat",
    )(qkv, qkv, qkv, k_ctx, v_ctx, rp)


def _row_pair_table(rpb_l):
    pad = ((0, 0), (0, 16 - N_ROW_PAIRS), (0, GRID_W - rpb_l.shape[-1]))
    return jnp.concatenate([jnp.pad(rpb_l[:, :N_ROW_PAIRS], pad), jnp.pad(rpb_l[:, 1:], pad)], axis=-1)


def _mix_kernel(x_ref, mod_ref, yc_ref, ya_ref, d_ref, gates_ref, pw_ref, ps_ref, wb_ref, wo_ref, o_ref):
    yp = [jnp.dot(d_ref[:, g * POOL_GROUP:(g + 1) * POOL_GROUP], pw_ref[g], preferred_element_type=F32)
          for g in range(len(POOL_SIZES))]
    y_pool = (jnp.concatenate(yp, axis=-1) * ps_ref[...]).astype(BF16)
    merged = None
    for i, y in enumerate((yc_ref[...], ya_ref[...], y_pool)):
        proj = jnp.dot(y, wb_ref[i], preferred_element_type=F32)
        term = gates_ref[:, i * D_MODEL:(i + 1) * D_MODEL].astype(F32) * proj
        merged = term if merged is None else merged + term
    mix = jnp.dot(merged.astype(BF16), wo_ref[...], preferred_element_type=F32)
    o_ref[...] = x_ref[...] + mod_ref[0, 2:3, :] * mix


def _mix(x, mod, yconv, yattn, d, gates, pool_w, pool_scale, w_branch, w_out):
    ntok = x.shape[0]
    seq_tiles = ntok // mod.shape[0] // MIX_TILE
    row = lambda i: (i, 0)
    const2 = lambda i: (0, 0)
    const3 = lambda i: (0, 0, 0)
    return pl.pallas_call(
        _mix_kernel,
        grid=(ntok // MIX_TILE,),
        in_specs=[
            pl.BlockSpec((MIX_TILE, D_MODEL), row),
            pl.BlockSpec((1, 6, D_MODEL), lambda i: (i // seq_tiles, 0, 0)),
            pl.BlockSpec((MIX_TILE, CONV_W), row),
            pl.BlockSpec((MIX_TILE, ATTN_W), row),
            pl.BlockSpec((MIX_TILE, POOL_W), row),
            pl.BlockSpec((MIX_TILE, N_BRANCH * D_MODEL), row),
            pl.BlockSpec((len(POOL_SIZES), POOL_GROUP, POOL_GROUP), const3),
            pl.BlockSpec((1, POOL_W), const2),
            pl.BlockSpec((N_BRANCH, CONV_W, D_MODEL), const3),
            pl.BlockSpec((D_MODEL, D_MODEL), const2),
        ],
        out_specs=pl.BlockSpec((MIX_TILE, D_MODEL), row),
        out_shape=jax.ShapeDtypeStruct((ntok, D_MODEL), F32),
        compiler_params=_params("arbitrary"),
        name="mix",
    )(x, mod, yconv, yattn, d, gates, pool_w, pool_scale, w_branch, w_out)


def _ffn_kernel(x_ref, mod_ref, g2_ref, wu_ref, wv_ref, fc_ref, wd_ref, gf_ref, o_ref, h_scr,
                *, seq_len, final):
    j = pl.program_id(1)

    @pl.when(j == 0)
    def _():
        h = _norm_mod(x_ref[...], g2_ref[...], mod_ref[0, 3:4, :], mod_ref[0, 4:5, :])
        h_scr[...] = h.astype(BF16)

    h = h_scr[...]
    u = jnp.dot(h, wu_ref[...], preferred_element_type=F32)
    val = jnp.dot(h, wv_ref[...], preferred_element_type=F32)
    u = _dwconv3(u, fc_ref, _row_in_seq(u.shape, seq_len), seq_len)
    act = (jax.nn.gelu(u, approximate=True) * val).astype(BF16)
    part = jnp.dot(act, wd_ref[...], preferred_element_type=F32)

    @pl.when(j == 0)
    def _():
        o_ref[...] = part

    @pl.when(j > 0)
    def _():
        o_ref[...] += part

    @pl.when(j == N_FF_CHUNKS - 1)
    def _():
        xn = x_ref[...] + mod_ref[0, 5:6, :] * o_ref[...]
        if final:
            ms = jnp.mean(xn * xn, axis=-1, keepdims=True)
            xn = xn * lax.rsqrt(ms + EPS) * gf_ref[...]
        o_ref[...] = xn


def _ffn(x, mod, g2, w_up, f_conv, w_down, g_final, *, seq_len, final):
    ntok = x.shape[0]
    per_tile_mod = mod.shape[0] > 1
    mod_map = (lambda i, j: (i, 0, 0)) if per_tile_mod else (lambda i, j: (0, 0, 0))
    return pl.pallas_call(
        functools.partial(_ffn_kernel, seq_len=seq_len, final=final),
        grid=(ntok // TOK_TILE, N_FF_CHUNKS),
        in_specs=[
            pl.BlockSpec((TOK_TILE, D_MODEL), lambda i, j: (i, 0)),
            pl.BlockSpec((1, 6, D_MODEL), mod_map),
            pl.BlockSpec((1, D_MODEL), lambda i, j: (0, 0)),
            pl.BlockSpec((D_MODEL, FF_CHUNK), lambda i, j: (0, j)),
            pl.BlockSpec((D_MODEL, FF_CHUNK), lambda i, j: (0, N_FF_CHUNKS + j)),
            pl.BlockSpec((3, FF_CHUNK), lambda i, j: (0, j)),
            pl.BlockSpec((FF_CHUNK, D_MODEL), lambda i, j: (j, 0)),
            pl.BlockSpec((1, D_MODEL), lambda i, j: (0, 0)),
        ],
        out_specs=pl.BlockSpec((TOK_TILE, D_MODEL), lambda i, j: (i, 0)),
        out_shape=jax.ShapeDtypeStruct((ntok, D_MODEL), F32),
        scratch_shapes=[pltpu.VMEM((TOK_TILE, D_MODEL), BF16)],
        compiler_params=_params("arbitrary", "arbitrary"),
        name="ffn",
    )(x, mod, g2, w_up, w_up, f_conv, w_down, g_final)


def kernel(x_prompt, x_sample, cache_kv, c, c_ctx, w_mod, b_mod, g_norm1, g_norm2, w_in, conv_w, rpb,
           pool_w, pool_scale, w_branch, w_out, ffn_w_up, ffn_conv, ffn_w_down, g_final):
    n_mod = 8
    cvec = jnp.concatenate([c_ctx[None, :], c, jnp.zeros((n_mod - 1 - N_LAT_SEQ, D_MODEL), F32)], axis=0)
    mod = _adaln(cvec, w_mod, b_mod).reshape(DEPTH, n_mod, 6, D_MODEL)

    xp = x_prompt.reshape(N_CTX_SEQ * CTX_LEN, D_MODEL)
    xs = x_sample.reshape(N_LAT_SEQ * LAT_LEN, D_MODEL)
    gf = g_final.reshape(1, D_MODEL)
    kv_layers = []
    for l in range(DEPTH):
        w_in_l = w_in[l].astype(BF16)
        pool_w_l = pool_w[l].astype(BF16)
        pool_s_l = pool_scale[l].reshape(1, POOL_W)
        w_br_l = w_branch[l].astype(BF16)
        w_out_l = w_out[l].astype(BF16)
        w_up_l = ffn_w_up[l].astype(BF16)
        w_dn_l = ffn_w_down[l].astype(BF16)
        g1 = g_norm1[l].reshape(1, D_MODEL)
        g2 = g_norm2[l].reshape(1, D_MODEL)
        mod_ctx = mod[l, 0:1]
        mod_lat = mod[l, 1:1 + N_LAT_SEQ]
        final = l == DEPTH - 1

        yc, qkv, d, gates, kv = _inproj(xp, mod_ctx, g1, w_in_l, conv_w[l], seq_len=CTX_LEN, with_kv=True)
        ya = _ctx_attention(qkv)
        xp = _mix(xp, mod_ctx, yc, ya, d, gates, pool_w_l, pool_s_l, w_br_l, w_out_l)
        xp = _ffn(xp, mod_ctx, g2, w_up_l, ffn_conv[l], w_dn_l, gf, seq_len=CTX_LEN, final=final)
        kv_layers.append(kv)

        k_ctx = cache_kv[:, l, 0].transpose(0, 2, 1, 3).reshape(N_LAT_SEQ, PAST_LEN, ATTN_W).astype(BF16)
        v_ctx = cache_kv[:, l, 1].transpose(0, 2, 1, 3).reshape(N_LAT_SEQ, PAST_LEN, ATTN_W).astype(BF16)
        yc, qkv, d, gates = _inproj(xs, mod_lat, g1, w_in_l, conv_w[l], seq_len=LAT_LEN, with_kv=False)
        ya = _lat_attention(qkv, k_ctx, v_ctx, _row_pair_table(rpb[l]))
        xs = _mix(xs, mod_lat, yc, ya, d, gates, pool_w_l, pool_s_l, w_br_l, w_out_l)
        xs = _ffn(xs, mod_lat, g2, w_up_l, ffn_conv[l], w_dn_l, gf, seq_len=LAT_LEN, final=final)

    y_prompt = xp.reshape(N_CTX_SEQ, CTX_LEN, D_MODEL)
    y_sample = xs.reshape(N_LAT_SEQ, LAT_LEN, D_MODEL)
    kv_state = jnp.stack(kv_layers, axis=1)
    return (y_prompt, y_sample, kv_state)
```

```python
import functools

import numpy as np
import jax
import jax.numpy as jnp
from jax import lax
from jax.experimental import pallas as pl
from jax.experimental.pallas import tpu as pltpu

F32 = jnp.float32
BF16 = jnp.bfloat16

D_MODEL = 1024
N_CTX_SEQ = 16
CTX_LEN = 256
DEPTH = 2
N_LAT_SEQ = 4
LAT_LEN = 1024
PAST_LEN = 256
GRID_W = 64
GRID_ROWS = LAT_LEN // GRID_W
CONV_W = 512
N_HEADS = 8
HEAD_DIM = 64
ATTN_W = N_HEADS * HEAD_DIM
POOL_W = 512
POOL_SIZES = (2, 4, 8, 16)
POOL_GROUP = POOL_W // len(POOL_SIZES)
N_BRANCH = 3
WIN_H = 8
WIN_W = 16
D_FF = 2816
EPS = 1e-6
IN_W = 3 * CONV_W + 3 * ATTN_W + POOL_W + N_BRANCH * D_MODEL

LANES = 128
TOK_TILE = 1024
CTX_IN_TILE = 512
IN_CHUNK = 512
FF_CHUNK = 256
N_FF_CHUNKS = D_FF // FF_CHUNK
MIX_TILE = 512
VMEM_LIMIT = 56 * 1024 * 1024


def _params(*sem):
    return pltpu.CompilerParams(dimension_semantics=sem, vmem_limit_bytes=VMEM_LIMIT)


def _resident(shape):
    return pl.BlockSpec(shape, lambda *_: (0,) * len(shape), pipeline_mode=pl.Buffered(1))


def _mod_spec(mod, ntok, tile):
    tiles_per_mod = ntok // mod.shape[0] // tile
    return pl.BlockSpec((1, 6, D_MODEL), lambda i, *_: (i // tiles_per_mod, 0, 0))


def _norm_mod(x, g, shift, scale):
    ms = jnp.mean(x * x, axis=-1, keepdims=True)
    return (x * lax.rsqrt(ms + EPS) * g) * (1.0 + scale) + shift


def _row_in_seq(shape, seq_len):
    return lax.broadcasted_iota(jnp.int32, shape, 0) & (seq_len - 1)


def _shift_rows(x, s, row, seq_len):
    y = pltpu.roll(x, s % x.shape[0], 0)
    ok = (row >= s) if s > 0 else (row < seq_len + s)
    return jnp.where(ok, y, 0.0)


def _dwconv3(x, w_ref, row, seq_len):
    return (_shift_rows(x, 1, row, seq_len) * w_ref[0:1, :] + x * w_ref[1:2, :]
            + _shift_rows(x, -1, row, seq_len) * w_ref[2:3, :])


def _pool_delta(p, seq_len):
    row = _row_in_seq((p.shape[0], POOL_GROUP), seq_len)
    outs = []
    for gi, w in enumerate(POOL_SIZES):
        pg = p[:, gi * POOL_GROUP:(gi + 1) * POOL_GROUP]
        fwd = bwd = pg
        k = 1
        while k < w // 2:
            fwd = fwd + _shift_rows(fwd, -k, row, seq_len)
            bwd = bwd + _shift_rows(bwd, k, row, seq_len)
            k *= 2
        acc = fwd + _shift_rows(bwd, 1, row, seq_len)
        lo = jnp.maximum(row - w // 2, 0)
        hi = jnp.minimum(row - w // 2 + w, seq_len)
        outs.append(acc / (hi - lo).astype(F32) - pg)
    return jnp.concatenate(outs, axis=-1)


def _adaln_kernel(c_ref, w_ref, b_ref, o_ref):
    c = c_ref[...]
    s = c * jax.nn.sigmoid(c)
    o_ref[0] = jnp.dot(s.astype(BF16), w_ref[0].astype(BF16), preferred_element_type=F32) + b_ref[0]


def _adaln(cvec, w_mod, b_mod):
    nb = cvec.shape[0]
    n_out = 6 * D_MODEL
    blk = 1024
    return pl.pallas_call(
        _adaln_kernel,
        grid=(DEPTH, n_out // blk),
        in_specs=[
            pl.BlockSpec((nb, D_MODEL), lambda l, n: (0, 0)),
            pl.BlockSpec((1, D_MODEL, blk), lambda l, n: (l, 0, n)),
            pl.BlockSpec((1, 1, blk), lambda l, n: (l, 0, n)),
        ],
        out_specs=pl.BlockSpec((1, nb, blk), lambda l, n: (l, 0, n)),
        out_shape=jax.ShapeDtypeStruct((DEPTH, nb, n_out), F32),
        compiler_params=_params("arbitrary", "arbitrary"),
        name="adaln",
    )(cvec, w_mod, b_mod.reshape(DEPTH, 1, n_out))


COL_CONV = 0
COL_QKV = 3 * CONV_W
COL_POOL = COL_QKV + 3 * ATTN_W
COL_GATE = COL_POOL + POOL_W
GATE_PHASES = 2
GATE_BLOCK = N_BRANCH * D_MODEL // GATE_PHASES


def _inproj_kernel(x_ref, mod_ref, g_ref, w_ref, cw_ref, yconv_ref, qkv_ref, d_ref, gates_ref,
                   *rest, seq_len, with_kv):
    if with_kv:
        kv_ref, h_scr = rest
    else:
        (h_scr,) = rest
    j = pl.program_id(1)

    def proj(col):
        return jnp.dot(h_scr[...], w_ref[:, col:col + IN_CHUNK], preferred_element_type=F32)

    @pl.when(j == 0)
    def _():
        h = _norm_mod(x_ref[...], g_ref[...], mod_ref[0, 0:1, :], mod_ref[0, 1:2, :])
        h_scr[...] = h.astype(BF16)
        row = _row_in_seq((x_ref.shape[0], IN_CHUNK), seq_len)
        b_gate = proj(COL_CONV)
        conv = _dwconv3(proj(COL_CONV + CONV_W) * proj(COL_CONV + 2 * CONV_W), cw_ref, row, seq_len)
        yconv_ref[...] = (b_gate * conv).astype(BF16)
        qkv_ref[:, 0:ATTN_W] = (proj(COL_QKV) * (HEAD_DIM ** -0.5)).astype(BF16)
        for kv in range(2):
            z = proj(COL_QKV + (1 + kv) * ATTN_W)
            qkv_ref[:, (1 + kv) * ATTN_W:(2 + kv) * ATTN_W] = z.astype(BF16)
            if with_kv:
                for s in range(z.shape[0] // seq_len):
                    for h_i in range(N_HEADS):
                        kv_ref[s, kv, h_i] = z[s * seq_len:(s + 1) * seq_len,
                                               h_i * HEAD_DIM:(h_i + 1) * HEAD_DIM]
        d_ref[...] = _pool_delta(proj(COL_POOL), seq_len).astype(BF16)

    for phase in range(GATE_PHASES):
        @pl.when(j == 1 + phase)
        def _():
            for c in range(GATE_BLOCK // IN_CHUNK):
                z = proj(COL_GATE + phase * GATE_BLOCK + c * IN_CHUNK)
                gates_ref[:, c * IN_CHUNK:(c + 1) * IN_CHUNK] = jax.nn.sigmoid(z).astype(BF16)


def _inproj(x, mod, g1, w_in, conv_w, *, seq_len, with_kv, tile):
    ntok = x.shape[0]
    first = lambda i, j: (i, 0)
    out_shape = [
        jax.ShapeDtypeStruct((ntok, CONV_W), BF16),
        jax.ShapeDtypeStruct((ntok, 3 * ATTN_W), BF16),
        jax.ShapeDtypeStruct((ntok, POOL_W), BF16),
        jax.ShapeDtypeStruct((ntok, N_BRANCH * D_MODEL), BF16),
    ]
    out_specs = [
        pl.BlockSpec((tile, CONV_W), first),
        pl.BlockSpec((tile, 3 * ATTN_W), first),
        pl.BlockSpec((tile, POOL_W), first),
        pl.BlockSpec((tile, GATE_BLOCK), lambda i, j: (i, jnp.maximum(j - 1, 0))),
    ]
    if with_kv:
        spt = tile // seq_len
        out_shape.append(jax.ShapeDtypeStruct((ntok // seq_len, 2, N_HEADS, seq_len, HEAD_DIM), F32))
        out_specs.append(pl.BlockSpec((spt, 2, N_HEADS, seq_len, HEAD_DIM), lambda i, j: (i, 0, 0, 0, 0)))
    return pl.pallas_call(
        functools.partial(_inproj_kernel, seq_len=seq_len, with_kv=with_kv),
        grid=(ntok // tile, 1 + GATE_PHASES),
        in_specs=[
            pl.BlockSpec((tile, D_MODEL), first),
            _mod_spec(mod, ntok, tile),
            _resident((1, D_MODEL)),
            _resident((D_MODEL, IN_W)),
            _resident((3, CONV_W)),
        ],
        out_specs=out_specs,
        out_shape=out_shape,
        scratch_shapes=[pltpu.VMEM((tile, D_MODEL), BF16)],
        compiler_params=_params("arbitrary", "arbitrary"),
        name="inproj_ctx" if with_kv else "inproj_lat",
    )(x, mod, g1, w_in, conv_w)


def _attend_pair(q2, kvb):
    lane = lax.broadcasted_iota(jnp.int32, (1, LANES), 1)
    out = None
    for half in range(2):
        m = (lane >= half * HEAD_DIM) & (lane < (half + 1) * HEAD_DIM)
        qh = jnp.where(m, q2, jnp.zeros_like(q2))
        scores = []
        for k2, _, bias in kvb:
            s = lax.dot_general(qh, k2, (((1,), (1,)), ((), ())), preferred_element_type=F32)
            if bias is not None:
                s = s + bias[half]
            scores.append(s)
        mx = functools.reduce(jnp.maximum, [jnp.max(s, axis=-1, keepdims=True) for s in scores])
        es = [jnp.exp(s - mx) for s in scores]
        den = functools.reduce(lambda a, b: a + b, [jnp.sum(e, axis=-1, keepdims=True) for e in es])
        inv = 1.0 / den
        for e, (_, v2, _) in zip(es, kvb):
            vh = jnp.where(m, v2, jnp.zeros_like(v2))
            o = jnp.dot((e * inv).astype(BF16), vh, preferred_element_type=F32)
            out = o if out is None else out + o
    return out


def _ctx_attn_kernel(q_ref, k_ref, v_ref, o_ref):
    for hp in range(N_HEADS // 2):
        sl = slice(hp * LANES, (hp + 1) * LANES)
        o_ref[:, sl] = _attend_pair(q_ref[:, sl], [(k_ref[:, sl], v_ref[:, sl], None)]).astype(BF16)


def _ctx_attention(qkv):
    ntok = qkv.shape[0]
    return pl.pallas_call(
        _ctx_attn_kernel,
        grid=(ntok // CTX_LEN,),
        in_specs=[
            pl.BlockSpec((CTX_LEN, ATTN_W), lambda b: (b, 0)),
            pl.BlockSpec((CTX_LEN, ATTN_W), lambda b: (b, 1)),
            pl.BlockSpec((CTX_LEN, ATTN_W), lambda b: (b, 2)),
        ],
        out_specs=pl.BlockSpec((CTX_LEN, ATTN_W), lambda b: (b, 0)),
        out_shape=jax.ShapeDtypeStruct((ntok, ATTN_W), BF16),
        compiler_params=_params("arbitrary"),
        name="attn_ctx",
    )(qkv, qkv, qkv)


N_ROW_PAIRS = 2 * WIN_H - 2


def _fill_bias(rp_ref, bias_scr):
    shape = (GRID_W, LANES)
    qc = lax.broadcasted_iota(jnp.int32, shape, 0)
    kc = lax.broadcasted_iota(jnp.int32, shape, 1) & (GRID_W - 1)
    col_start = jnp.clip(qc - WIN_W // 2, 0, GRID_W - WIN_W)
    valid = (kc >= col_start) & (kc < col_start + WIN_W)
    for h in range(N_HEADS):
        pairs = []
        for dr in range(N_ROW_PAIRS):
            t = jnp.broadcast_to(rp_ref[h, dr:dr + 1, :], shape)
            t = pltpu.roll(t, LANES - (WIN_W - 1), 1)
            for bit in range(6):
                t = jnp.where(((qc >> bit) & 1) == 1, pltpu.roll(t, 1 << bit, 1), t)
            pairs.append(jnp.where(valid, t, -jnp.inf))
        for o in range(WIN_H):
            for u in range(WIN_H // 2):
                bias_scr[o, h, :, u * LANES:(u + 1) * LANES] = pairs[2 * u - o + WIN_H - 1]


def _lat_attn_kernel(q_ref, k_ref, v_ref, kc_ref, vc_ref, rp_ref, o_ref, bias_scr):
    qr = pl.program_id(1)

    @pl.when((pl.program_id(0) == 0) & (qr == 0))
    def _():
        _fill_bias(rp_ref, bias_scr)

    row0 = jnp.clip(qr - WIN_H // 2, 0, GRID_ROWS - WIN_H)
    start = pl.multiple_of(row0 * GRID_W, GRID_W)
    off = qr - row0
    nkey = WIN_H * GRID_W
    for hp in range(N_HEADS // 2):
        sl = slice(hp * LANES, (hp + 1) * LANES)
        kvb = [
            (k_ref[pl.ds(start, nkey), sl], v_ref[pl.ds(start, nkey), sl],
             (bias_scr[off, 2 * hp], bias_scr[off, 2 * hp + 1])),
            (kc_ref[0, :, sl], vc_ref[0, :, sl], None),
        ]
        o_ref[:, sl] = _attend_pair(q_ref[:, sl], kvb).astype(BF16)


def _lat_attention(qkv, k_ctx, v_ctx, rp):
    ntok = qkv.shape[0]
    nseq = ntok // LAT_LEN
    return pl.pallas_call(
        _lat_attn_kernel,
        grid=(nseq, GRID_ROWS),
        in_specs=[
            pl.BlockSpec((GRID_W, ATTN_W), lambda b, r: (b * GRID_ROWS + r, 0)),
            pl.BlockSpec((LAT_LEN, ATTN_W), lambda b, r: (b, 1)),
            pl.BlockSpec((LAT_LEN, ATTN_W), lambda b, r: (b, 2)),
            pl.BlockSpec((1, PAST_LEN, ATTN_W), lambda b, r: (b, 0, 0)),
            pl.BlockSpec((1, PAST_LEN, ATTN_W), lambda b, r: (b, 0, 0)),
            pl.BlockSpec(rp.shape, lambda b, r: (0, 0, 0)),
        ],
        out_specs=pl.BlockSpec((GRID_W, ATTN_W), lambda b, r: (b * GRID_ROWS + r, 0)),
        out_shape=jax.ShapeDtypeStruct((ntok, ATTN_W), BF16),
        scratch_shapes=[pltpu.VMEM((WIN_H, N_HEADS, GRID_W, WIN_H * GRID_W), F32)],
        compiler_params=_params("arbitrary", "arbitrary"),
        name="attn_lat",
    )(qkv, qkv, qkv, k_ctx, v_ctx, rp)


def _row_pair_table(rpb_l):
    pad = ((0, 0), (0, 16 - N_ROW_PAIRS), (0, GRID_W - rpb_l.shape[-1]))
    return jnp.concatenate([jnp.pad(rpb_l[:, :N_ROW_PAIRS], pad), jnp.pad(rpb_l[:, 1:], pad)], axis=-1)


def _mix_kernel(x_ref, mod_ref, yc_ref, ya_ref, d_ref, gates_ref, pw_ref, ps_ref, wb_ref, wo_ref, o_ref):
    yp = [jnp.dot(d_ref[:, g * POOL_GROUP:(g + 1) * POOL_GROUP], pw_ref[g], preferred_element_type=F32)
          for g in range(len(POOL_SIZES))]
    y_pool = (jnp.concatenate(yp, axis=-1) * ps_ref[...]).astype(BF16)
    merged = None
    for i, y in enumerate((yc_ref[...], ya_ref[...], y_pool)):
        proj = jnp.dot(y, wb_ref[i], preferred_element_type=F32)
        term = gates_ref[:, i * D_MODEL:(i + 1) * D_MODEL].astype(F32) * proj
        merged = term if merged is None else merged + term
    mix = jnp.dot(merged.astype(BF16), wo_ref[...], preferred_element_type=F32)
    o_ref[...] = x_ref[...] + mod_ref[0, 2:3, :] * mix


def _mix(x, mod, yconv, yattn, d, gates, pool_w, pool_scale, w_branch, w_out):
    ntok = x.shape[0]
    row = lambda i: (i, 0)
    const2 = lambda i: (0, 0)
    const3 = lambda i: (0, 0, 0)
    return pl.pallas_call(
        _mix_kernel,
        grid=(ntok // MIX_TILE,),
        in_specs=[
            pl.BlockSpec((MIX_TILE, D_MODEL), row),
            _mod_spec(mod, ntok, MIX_TILE),
            pl.BlockSpec((MIX_TILE, CONV_W), row),
            pl.BlockSpec((MIX_TILE, ATTN_W), row),
            pl.BlockSpec((MIX_TILE, POOL_W), row),
            pl.BlockSpec((MIX_TILE, N_BRANCH * D_MODEL), row),
            pl.BlockSpec((len(POOL_SIZES), POOL_GROUP, POOL_GROUP), const3),
            pl.BlockSpec((1, POOL_W), const2),
            pl.BlockSpec((N_BRANCH, CONV_W, D_MODEL), const3),
            pl.BlockSpec((D_MODEL, D_MODEL), const2),
        ],
        out_specs=pl.BlockSpec((MIX_TILE, D_MODEL), row),
        out_shape=jax.ShapeDtypeStruct((ntok, D_MODEL), F32),
        compiler_params=_params("arbitrary"),
        name="mix",
    )(x, mod, yconv, yattn, d, gates, pool_w, pool_scale, w_branch, w_out)


def _ffn_kernel(x_ref, mod_ref, g2_ref, wu_ref, fc_ref, wd_ref, gf_ref, o_ref, h_scr, act_scr,
                *, seq_len, final):
    h = _norm_mod(x_ref[...], g2_ref[...], mod_ref[0, 3:4, :], mod_ref[0, 4:5, :])
    h_scr[...] = h.astype(BF16)
    row = _row_in_seq((x_ref.shape[0], FF_CHUNK), seq_len)
    for c in range(N_FF_CHUNKS):
        sl = slice(c * FF_CHUNK, (c + 1) * FF_CHUNK)
        sl_val = slice(D_FF + c * FF_CHUNK, D_FF + (c + 1) * FF_CHUNK)
        u = jnp.dot(h_scr[...], wu_ref[:, sl], preferred_element_type=F32)
        val = jnp.dot(h_scr[...], wu_ref[:, sl_val], preferred_element_type=F32)
        u = _dwconv3(u, fc_ref.at[:, sl], row, seq_len)
        act_scr[:, sl] = (jax.nn.gelu(u, approximate=True) * val).astype(BF16)
    y = jnp.dot(act_scr[...], wd_ref[...], preferred_element_type=F32)
    xn = x_ref[...] + mod_ref[0, 5:6, :] * y
    if final:
        ms = jnp.mean(xn * xn, axis=-1, keepdims=True)
        xn = xn * lax.rsqrt(ms + EPS) * gf_ref[...]
    o_ref[...] = xn


def _ffn(x, mod, g2, w_up, f_conv, w_down, g_final, *, seq_len, final):
    ntok = x.shape[0]
    return pl.pallas_call(
        functools.partial(_ffn_kernel, seq_len=seq_len, final=final),
        grid=(ntok // TOK_TILE,),
        in_specs=[
            pl.BlockSpec((TOK_TILE, D_MODEL), lambda i: (i, 0)),
            _mod_spec(mod, ntok, TOK_TILE),
            _resident((1, D_MODEL)),
            _resident((D_MODEL, 2 * D_FF)),
            _resident((3, D_FF)),
            _resident((D_FF, D_MODEL)),
            _resident((1, D_MODEL)),
        ],
        out_specs=pl.BlockSpec((TOK_TILE, D_MODEL), lambda i: (i, 0)),
        out_shape=jax.ShapeDtypeStruct((ntok, D_MODEL), F32),
        scratch_shapes=[pltpu.VMEM((TOK_TILE, D_MODEL), BF16), pltpu.VMEM((TOK_TILE, D_FF), BF16)],
        compiler_params=_params("arbitrary"),
        name="ffn",
    )(x, mod, g2, w_up, f_conv, w_down, g_final)


def kernel(x_prompt, x_sample, cache_kv, c, c_ctx, w_mod, b_mod, g_norm1, g_norm2, w_in, conv_w, rpb,
           pool_w, pool_scale, w_branch, w_out, ffn_w_up, ffn_conv, ffn_w_down, g_final):
    n_mod = 8
    cvec = jnp.concatenate([c_ctx[None, :], c, jnp.zeros((n_mod - 1 - N_LAT_SEQ, D_MODEL), F32)], axis=0)
    mod = _adaln(cvec, w_mod, b_mod).reshape(DEPTH, n_mod, 6, D_MODEL)

    xp = x_prompt.reshape(N_CTX_SEQ * CTX_LEN, D_MODEL)
    xs = x_sample.reshape(N_LAT_SEQ * LAT_LEN, D_MODEL)
    gf = g_final.reshape(1, D_MODEL)
    kv_layers = []
    for l in range(DEPTH):
        w_in_l = w_in[l].astype(BF16)
        pool_w_l = pool_w[l].astype(BF16)
        pool_s_l = pool_scale[l].reshape(1, POOL_W)
        w_br_l = w_branch[l].astype(BF16)
        w_out_l = w_out[l].astype(BF16)
        w_up_l = ffn_w_up[l].astype(BF16)
        w_dn_l = ffn_w_down[l].astype(BF16)
        g1 = g_norm1[l].reshape(1, D_MODEL)
        g2 = g_norm2[l].reshape(1, D_MODEL)
        mod_ctx = mod[l, 0:1]
        mod_lat = mod[l, 1:1 + N_LAT_SEQ]
        final = l == DEPTH - 1

        yc, qkv, d, gates, kv = _inproj(xp, mod_ctx, g1, w_in_l, conv_w[l], seq_len=CTX_LEN, with_kv=True,
                                        tile=CTX_IN_TILE)
        ya = _ctx_attention(qkv)
        xp = _mix(xp, mod_ctx, yc, ya, d, gates, pool_w_l, pool_s_l, w_br_l, w_out_l)
        xp = _ffn(xp, mod_ctx, g2, w_up_l, ffn_conv[l], w_dn_l, gf, seq_len=CTX_LEN, final=final)
        kv_layers.append(kv)

        k_ctx = cache_kv[:, l, 0].transpose(0, 2, 1, 3).reshape(N_LAT_SEQ, PAST_LEN, ATTN_W).astype(BF16)
        v_ctx = cache_kv[:, l, 1].transpose(0, 2, 1, 3).reshape(N_LAT_SEQ, PAST_LEN, ATTN_W).astype(BF16)
        yc, qkv, d, gates = _inproj(xs, mod_lat, g1, w_in_l, conv_w[l], seq_len=LAT_LEN, with_kv=False,
                                    tile=TOK_TILE)
        ya = _lat_attention(qkv, k_ctx, v_ctx, _row_pair_table(rpb[l]))
        xs = _mix(xs, mod_lat, yc, ya, d, gates, pool_w_l, pool_s_l, w_br_l, w_out_l)
        xs = _ffn(xs, mod_lat, g2, w_up_l, ffn_conv[l], w_dn_l, gf, seq_len=LAT_LEN, final=final)

    y_prompt = xp.reshape(N_CTX_SEQ, CTX_LEN, D_MODEL)
    y_sample = xs.reshape(N_LAT_SEQ, LAT_LEN, D_MODEL)
    kv_state = jnp.stack(kv_layers, axis=1)
    return (y_prompt, y_sample, kv_state)
```

```python
import functools

import numpy as np
import jax
import jax.numpy as jnp
from jax import lax
from jax.experimental import pallas as pl
from jax.experimental.pallas import tpu as pltpu

F32 = jnp.float32
BF16 = jnp.bfloat16

D_MODEL = 1024
N_CTX_SEQ = 16
CTX_LEN = 256
DEPTH = 2
N_LAT_SEQ = 4
LAT_LEN = 1024
PAST_LEN = 256
GRID_W = 64
GRID_ROWS = LAT_LEN // GRID_W
CONV_W = 512
N_HEADS = 8
HEAD_DIM = 64
ATTN_W = N_HEADS * HEAD_DIM
POOL_W = 512
POOL_SIZES = (2, 4, 8, 16)
POOL_GROUP = POOL_W // len(POOL_SIZES)
N_BRANCH = 3
WIN_H = 8
WIN_W = 16
D_FF = 2816
EPS = 1e-6
IN_W = 3 * CONV_W + 3 * ATTN_W + POOL_W + N_BRANCH * D_MODEL

LANES = 128
TOK_TILE = 1024
CTX_IN_TILE = 512
IN_CHUNK = 512
FF_CHUNK = 256
N_FF_CHUNKS = D_FF // FF_CHUNK
MIX_TILE = 512
VMEM_LIMIT = 56 * 1024 * 1024


def _params(*sem):
    return pltpu.CompilerParams(dimension_semantics=sem, vmem_limit_bytes=VMEM_LIMIT)


def _resident(shape):
    return pl.BlockSpec(shape, lambda *_: (0,) * len(shape), pipeline_mode=pl.Buffered(1))


def _mod_spec(mod, ntok, tile):
    tiles_per_mod = ntok // mod.shape[0] // tile
    return pl.BlockSpec((1, 6, D_MODEL), lambda i, *_: (i // tiles_per_mod, 0, 0))


def _norm_mod(x, g, shift, scale):
    ms = jnp.mean(x * x, axis=-1, keepdims=True)
    return (x * lax.rsqrt(ms + EPS) * g) * (1.0 + scale) + shift


def _row_in_seq(shape, seq_len):
    return lax.broadcasted_iota(jnp.int32, shape, 0) & (seq_len - 1)


def _shift_rows(x, s, row, seq_len):
    y = pltpu.roll(x, s % x.shape[0], 0)
    ok = (row >= s) if s > 0 else (row < seq_len + s)
    return jnp.where(ok, y, 0.0)


def _dwconv3(x, w_ref, row, seq_len):
    return (_shift_rows(x, 1, row, seq_len) * w_ref[0:1, :] + x * w_ref[1:2, :]
            + _shift_rows(x, -1, row, seq_len) * w_ref[2:3, :])


def _pool_delta(p, seq_len):
    row = _row_in_seq((p.shape[0], POOL_GROUP), seq_len)
    outs = []
    for gi, w in enumerate(POOL_SIZES):
        pg = p[:, gi * POOL_GROUP:(gi + 1) * POOL_GROUP]
        fwd = bwd = pg
        k = 1
        while k < w // 2:
            fwd = fwd + _shift_rows(fwd, -k, row, seq_len)
            bwd = bwd + _shift_rows(bwd, k, row, seq_len)
            k *= 2
        acc = fwd + _shift_rows(bwd, 1, row, seq_len)
        lo = jnp.maximum(row - w // 2, 0)
        hi = jnp.minimum(row - w // 2 + w, seq_len)
        outs.append(acc / (hi - lo).astype(F32) - pg)
    return jnp.concatenate(outs, axis=-1)


def _adaln_kernel(c_ref, w_ref, b_ref, o_ref):
    c = c_ref[...]
    s = c * jax.nn.sigmoid(c)
    o_ref[0] = jnp.dot(s.astype(BF16), w_ref[0].astype(BF16), preferred_element_type=F32) + b_ref[0]


def _adaln(cvec, w_mod, b_mod):
    nb = cvec.shape[0]
    n_out = 6 * D_MODEL
    blk = 1024
    return pl.pallas_call(
        _adaln_kernel,
        grid=(DEPTH, n_out // blk),
        in_specs=[
            pl.BlockSpec((nb, D_MODEL), lambda l, n: (0, 0)),
            pl.BlockSpec((1, D_MODEL, blk), lambda l, n: (l, 0, n)),
            pl.BlockSpec((1, 1, blk), lambda l, n: (l, 0, n)),
        ],
        out_specs=pl.BlockSpec((1, nb, blk), lambda l, n: (l, 0, n)),
        out_shape=jax.ShapeDtypeStruct((DEPTH, nb, n_out), F32),
        compiler_params=_params("arbitrary", "arbitrary"),
        name="adaln",
    )(cvec, w_mod, b_mod.reshape(DEPTH, 1, n_out))


COL_CONV = 0
COL_QKV = 3 * CONV_W
COL_POOL = COL_QKV + 3 * ATTN_W
COL_GATE = COL_POOL + POOL_W
GATE_PHASES = 2
GATE_BLOCK = N_BRANCH * D_MODEL // GATE_PHASES


def _inproj_kernel(x_ref, mod_ref, g_ref, w_ref, cw_ref, yconv_ref, qkv_ref, d_ref, gates_ref,
                   *rest, seq_len, with_kv):
    if with_kv:
        kv_ref, h_scr = rest
    else:
        (h_scr,) = rest
    j = pl.program_id(1)

    def proj(col):
        return jnp.dot(h_scr[...], w_ref[:, col:col + IN_CHUNK], preferred_element_type=F32)

    @pl.when(j == 0)
    def _():
        h = _norm_mod(x_ref[...], g_ref[...], mod_ref[0, 0:1, :], mod_ref[0, 1:2, :])
        h_scr[...] = h.astype(BF16)
        row = _row_in_seq((x_ref.shape[0], IN_CHUNK), seq_len)
        b_gate = proj(COL_CONV)
        conv = _dwconv3(proj(COL_CONV + CONV_W) * proj(COL_CONV + 2 * CONV_W), cw_ref, row, seq_len)
        yconv_ref[...] = (b_gate * conv).astype(BF16)
        qkv_ref[:, 0:ATTN_W] = (proj(COL_QKV) * (HEAD_DIM ** -0.5)).astype(BF16)
        for kv in range(2):
            z = proj(COL_QKV + (1 + kv) * ATTN_W)
            qkv_ref[:, (1 + kv) * ATTN_W:(2 + kv) * ATTN_W] = z.astype(BF16)
            if with_kv:
                for s in range(z.shape[0] // seq_len):
                    for h_i in range(N_HEADS):
                        kv_ref[s, kv, h_i] = z[s * seq_len:(s + 1) * seq_len,
                                               h_i * HEAD_DIM:(h_i + 1) * HEAD_DIM]
        d_ref[...] = _pool_delta(proj(COL_POOL), seq_len).astype(BF16)

    for phase in range(GATE_PHASES):
        @pl.when(j == 1 + phase)
        def _():
            for c in range(GATE_BLOCK // IN_CHUNK):
                z = proj(COL_GATE + phase * GATE_BLOCK + c * IN_CHUNK)
                gates_ref[:, c * IN_CHUNK:(c + 1) * IN_CHUNK] = jax.nn.sigmoid(z).astype(BF16)


def _inproj(x, mod, g1, w_in, conv_w, *, seq_len, with_kv, tile):
    ntok = x.shape[0]
    first = lambda i, j: (i, 0)
    out_shape = [
        jax.ShapeDtypeStruct((ntok, CONV_W), BF16),
        jax.ShapeDtypeStruct((ntok, 3 * ATTN_W), BF16),
        jax.ShapeDtypeStruct((ntok, POOL_W), BF16),
        jax.ShapeDtypeStruct((ntok, N_BRANCH * D_MODEL), BF16),
    ]
    out_specs = [
        pl.BlockSpec((tile, CONV_W), first),
        pl.BlockSpec((tile, 3 * ATTN_W), first),
        pl.BlockSpec((tile, POOL_W), first),
        pl.BlockSpec((tile, GATE_BLOCK), lambda i, j: (i, jnp.maximum(j - 1, 0))),
    ]
    if with_kv:
        spt = tile // seq_len
        out_shape.append(jax.ShapeDtypeStruct((ntok // seq_len, 2, N_HEADS, seq_len, HEAD_DIM), F32))
        out_specs.append(pl.BlockSpec((spt, 2, N_HEADS, seq_len, HEAD_DIM), lambda i, j: (i, 0, 0, 0, 0)))
    return pl.pallas_call(
        functools.partial(_inproj_kernel, seq_len=seq_len, with_kv=with_kv),
        grid=(ntok // tile, 1 + GATE_PHASES),
        in_specs=[
            pl.BlockSpec((tile, D_MODEL), first),
            _mod_spec(mod, ntok, tile),
            _resident((1, D_MODEL)),
            _resident((D_MODEL, IN_W)),
            _resident((3, CONV_W)),
        ],
        out_specs=out_specs,
        out_shape=out_shape,
        scratch_shapes=[pltpu.VMEM((tile, D_MODEL), BF16)],
        compiler_params=_params("arbitrary", "arbitrary"),
        name="inproj_ctx" if with_kv else "inproj_lat",
    )(x, mod, g1, w_in, conv_w)


def _attend_pair(q2, kvb):
    lane = lax.broadcasted_iota(jnp.int32, (1, LANES), 1)
    out = None
    for half in range(2):
        m = (lane >= half * HEAD_DIM) & (lane < (half + 1) * HEAD_DIM)
        qh = jnp.where(m, q2, jnp.zeros_like(q2))
        scores = []
        for k2, _, bias in kvb:
            s = lax.dot_general(qh, k2, (((1,), (1,)), ((), ())), preferred_element_type=F32)
            if bias is not None:
                s = s + bias[half]
            scores.append(s)
        mx = functools.reduce(jnp.maximum, [jnp.max(s, axis=-1, keepdims=True) for s in scores])
        es = [jnp.exp(s - mx) for s in scores]
        den = functools.reduce(lambda a, b: a + b, [jnp.sum(e, axis=-1, keepdims=True) for e in es])
        acc = None
        for e, (_, v2, _) in zip(es, kvb):
            vh = jnp.where(m, v2, jnp.zeros_like(v2))
            o = jnp.dot(e.astype(BF16), vh, preferred_element_type=F32)
            acc = o if acc is None else acc + o
        acc = acc * (1.0 / den)
        out = acc if out is None else out + acc
    return out


def _ctx_attn_kernel(q_ref, k_ref, v_ref, o_ref):
    for hp in range(N_HEADS // 2):
        sl = slice(hp * LANES, (hp + 1) * LANES)
        o_ref[:, sl] = _attend_pair(q_ref[:, sl], [(k_ref[:, sl], v_ref[:, sl], None)]).astype(BF16)


def _ctx_attention(qkv):
    ntok = qkv.shape[0]
    return pl.pallas_call(
        _ctx_attn_kernel,
        grid=(ntok // CTX_LEN,),
        in_specs=[
            pl.BlockSpec((CTX_LEN, ATTN_W), lambda b: (b, 0)),
            pl.BlockSpec((CTX_LEN, ATTN_W), lambda b: (b, 1)),
            pl.BlockSpec((CTX_LEN, ATTN_W), lambda b: (b, 2)),
        ],
        out_specs=pl.BlockSpec((CTX_LEN, ATTN_W), lambda b: (b, 0)),
        out_shape=jax.ShapeDtypeStruct((ntok, ATTN_W), BF16),
        compiler_params=_params("arbitrary"),
        name="attn_ctx",
    )(qkv, qkv, qkv)


Q_ROWS = 4
Q_BLOCK = Q_ROWS * GRID_W
N_Q_BLOCKS = GRID_ROWS // Q_ROWS
KEY_ROWS = 12
N_KEYS = KEY_ROWS * GRID_W
N_REL_ROWS = 2 * WIN_H - 1
N_ROW_PAIRS = N_REL_ROWS + 1


def _first_key_row(m):
    return min(max(Q_ROWS * m - WIN_H // 2, 0), GRID_ROWS - KEY_ROWS)


def _fill_bias(rp_ref, bias_scr):
    shape = (N_ROW_PAIRS * GRID_W, LANES)
    qc = lax.broadcasted_iota(jnp.int32, shape, 0) & (GRID_W - 1)
    lane = lax.broadcasted_iota(jnp.int32, shape, 1)
    kc = lane & (GRID_W - 1)
    col_start = jnp.clip(qc - WIN_W // 2, 0, GRID_W - WIN_W)
    valid = (kc >= col_start) & (kc < col_start + WIN_W)
    first_half = lax.broadcasted_iota(jnp.int32, (GRID_W, LANES), 1) < GRID_W
    for h in range(N_HEADS):
        t = jnp.broadcast_to(rp_ref[h][:, None, :], (N_ROW_PAIRS, GRID_W, LANES)).reshape(shape)
        t = pltpu.roll(t, LANES - (WIN_W - 1), 1)
        for bit in range(6):
            t = jnp.where(((qc >> bit) & 1) == 1, pltpu.roll(t, 1 << bit, 1), t)
        t = jnp.where(valid, t, -jnp.inf)
        for m in range(N_Q_BLOCKS):
            for ql in range(Q_ROWS):
                qr = Q_ROWS * m + ql
                win0 = min(max(qr - WIN_H // 2, 0), GRID_ROWS - WIN_H)
                for u in range(KEY_ROWS // 2):
                    kr = _first_key_row(m) + 2 * u
                    in0 = win0 <= kr < win0 + WIN_H
                    in1 = win0 <= kr + 1 < win0 + WIN_H
                    if in0 or in1:
                        pair = kr - qr + WIN_H
                        tile = t[pair * GRID_W:(pair + 1) * GRID_W]
                        if not in0:
                            tile = jnp.where(first_half, -jnp.inf, tile)
                        if not in1:
                            tile = jnp.where(first_half, tile, -jnp.inf)
                    else:
                        tile = jnp.full((GRID_W, LANES), -jnp.inf, F32)
                    bias_scr[h, m, ql * GRID_W:(ql + 1) * GRID_W, u * LANES:(u + 1) * LANES] = tile


def _lat_attn_kernel(q_ref, k_ref, v_ref, kc_ref, vc_ref, rp_ref, o_ref, bias_scr):
    hp = pl.program_id(0)
    m = pl.program_id(1)

    @pl.when((hp == 0) & (m == 0))
    def _():
        _fill_bias(rp_ref, bias_scr)

    row0 = jnp.clip(Q_ROWS * m - WIN_H // 2, 0, GRID_ROWS - KEY_ROWS)
    start = pl.multiple_of(row0 * GRID_W, Q_BLOCK)
    for s in range(q_ref.shape[0]):
        kvb = [
            (k_ref[s, pl.ds(start, N_KEYS), :], v_ref[s, pl.ds(start, N_KEYS), :],
             (bias_scr[2 * hp, m], bias_scr[2 * hp + 1, m])),
            (kc_ref[s], vc_ref[s], None),
        ]
        o_ref[s] = _attend_pair(q_ref[s], kvb).astype(BF16)


def _lat_attention(qkv, k_ctx, v_ctx, rp):
    nseq = qkv.shape[0] // LAT_LEN
    qkv = qkv.reshape(nseq, LAT_LEN, 3 * ATTN_W)
    n_hp = ATTN_W // LANES
    out = pl.pallas_call(
        _lat_attn_kernel,
        grid=(n_hp, N_Q_BLOCKS),
        in_specs=[
            pl.BlockSpec((nseq, Q_BLOCK, LANES), lambda hp, m: (0, m, hp)),
            pl.BlockSpec((nseq, LAT_LEN, LANES), lambda hp, m: (0, 0, n_hp + hp)),
            pl.BlockSpec((nseq, LAT_LEN, LANES), lambda hp, m: (0, 0, 2 * n_hp + hp)),
            pl.BlockSpec((nseq, PAST_LEN, LANES), lambda hp, m: (0, 0, hp)),
            pl.BlockSpec((nseq, PAST_LEN, LANES), lambda hp, m: (0, 0, hp)),
            _resident(rp.shape),
        ],
        out_specs=pl.BlockSpec((nseq, Q_BLOCK, LANES), lambda hp, m: (0, m, hp)),
        out_shape=jax.ShapeDtypeStruct((nseq, LAT_LEN, ATTN_W), BF16),
        scratch_shapes=[pltpu.VMEM((N_HEADS, N_Q_BLOCKS, Q_BLOCK, N_KEYS), F32)],
        compiler_params=_params("arbitrary", "arbitrary"),
        name="attn_lat",
    )(qkv, qkv, qkv, k_ctx, v_ctx, rp)
    return out.reshape(nseq * LAT_LEN, ATTN_W)


def _row_pair_table(rpb_l):
    lane_pad = GRID_W - rpb_l.shape[-1]
    lo = jnp.pad(rpb_l, ((0, 0), (1, 0), (0, lane_pad)))
    hi = jnp.pad(rpb_l, ((0, 0), (0, 1), (0, lane_pad)))
    return jnp.concatenate([lo, hi], axis=-1)


def _mix_kernel(x_ref, mod_ref, yc_ref, ya_ref, d_ref, gates_ref, pw_ref, ps_ref, wb_ref, wo_ref, o_ref):
    yp = [jnp.dot(d_ref[:, g * POOL_GROUP:(g + 1) * POOL_GROUP], pw_ref[g], preferred_element_type=F32)
          for g in range(len(POOL_SIZES))]
    y_pool = (jnp.concatenate(yp, axis=-1) * ps_ref[...]).astype(BF16)
    merged = None
    for i, y in enumerate((yc_ref[...], ya_ref[...], y_pool)):
        proj = jnp.dot(y, wb_ref[i], preferred_element_type=F32)
        term = gates_ref[:, i * D_MODEL:(i + 1) * D_MODEL].astype(F32) * proj
        merged = term if merged is None else merged + term
    mix = jnp.dot(merged.astype(BF16), wo_ref[...], preferred_element_type=F32)
    o_ref[...] = x_ref[...] + mod_ref[0, 2:3, :] * mix


def _mix(x, mod, yconv, yattn, d, gates, pool_w, pool_scale, w_branch, w_out):
    ntok = x.shape[0]
    row = lambda i: (i, 0)
    const2 = lambda i: (0, 0)
    const3 = lambda i: (0, 0, 0)
    return pl.pallas_call(
        _mix_kernel,
        grid=(ntok // MIX_TILE,),
        in_specs=[
            pl.BlockSpec((MIX_TILE, D_MODEL), row),
            _mod_spec(mod, ntok, MIX_TILE),
            pl.BlockSpec((MIX_TILE, CONV_W), row),
            pl.BlockSpec((MIX_TILE, ATTN_W), row),
            pl.BlockSpec((MIX_TILE, POOL_W), row),
            pl.BlockSpec((MIX_TILE, N_BRANCH * D_MODEL), row),
            pl.BlockSpec((len(POOL_SIZES), POOL_GROUP, POOL_GROUP), const3),
            pl.BlockSpec((1, POOL_W), const2),
            pl.BlockSpec((N_BRANCH, CONV_W, D_MODEL), const3),
            pl.BlockSpec((D_MODEL, D_MODEL), const2),
        ],
        out_specs=pl.BlockSpec((MIX_TILE, D_MODEL), row),
        out_shape=jax.ShapeDtypeStruct((ntok, D_MODEL), F32),
        compiler_params=_params("arbitrary"),
        name="mix",
    )(x, mod, yconv, yattn, d, gates, pool_w, pool_scale, w_branch, w_out)


def _ffn_kernel(x_ref, mod_ref, g2_ref, wu_ref, fc_ref, wd_ref, gf_ref, o_ref, h_scr, act_scr,
                *, seq_len, final):
    h = _norm_mod(x_ref[...], g2_ref[...], mod_ref[0, 3:4, :], mod_ref[0, 4:5, :])
    h_scr[...] = h.astype(BF16)
    row = _row_in_seq((x_ref.shape[0], FF_CHUNK), seq_len)
    for c in range(N_FF_CHUNKS):
        sl = slice(c * FF_CHUNK, (c + 1) * FF_CHUNK)
        sl_val = slice(D_FF + c * FF_CHUNK, D_FF + (c + 1) * FF_CHUNK)
        u = jnp.dot(h_scr[...], wu_ref[:, sl], preferred_element_type=F32)
        val = jnp.dot(h_scr[...], wu_ref[:, sl_val], preferred_element_type=F32)
        u = _dwconv3(u, fc_ref.at[:, sl], row, seq_len)
        act_scr[:, sl] = (jax.nn.gelu(u, approximate=True) * val).astype(BF16)
    y = jnp.dot(act_scr[...], wd_ref[...], preferred_element_type=F32)
    xn = x_ref[...] + mod_ref[0, 5:6, :] * y
    if final:
        ms = jnp.mean(xn * xn, axis=-1, keepdims=True)
        xn = xn * lax.rsqrt(ms + EPS) * gf_ref[...]
    o_ref[...] = xn


def _ffn(x, mod, g2, w_up, f_conv, w_down, g_final, *, seq_len, final):
    ntok = x.shape[0]
    return pl.pallas_call(
        functools.partial(_ffn_kernel, seq_len=seq_len, final=final),
        grid=(ntok // TOK_TILE,),
        in_specs=[
            pl.BlockSpec((TOK_TILE, D_MODEL), lambda i: (i, 0)),
            _mod_spec(mod, ntok, TOK_TILE),
            _resident((1, D_MODEL)),
            _resident((D_MODEL, 2 * D_FF)),
            _resident((3, D_FF)),
            _resident((D_FF, D_MODEL)),
            _resident((1, D_MODEL)),
        ],
        out_specs=pl.BlockSpec((TOK_TILE, D_MODEL), lambda i: (i, 0)),
        out_shape=jax.ShapeDtypeStruct((ntok, D_MODEL), F32),
        scratch_shapes=[pltpu.VMEM((TOK_TILE, D_MODEL), BF16), pltpu.VMEM((TOK_TILE, D_FF), BF16)],
        compiler_params=_params("arbitrary"),
        name="ffn",
    )(x, mod, g2, w_up, f_conv, w_down, g_final)


def kernel(x_prompt, x_sample, cache_kv, c, c_ctx, w_mod, b_mod, g_norm1, g_norm2, w_in, conv_w, rpb,
           pool_w, pool_scale, w_branch, w_out, ffn_w_up, ffn_conv, ffn_w_down, g_final):
    n_mod = 8
    cvec = jnp.concatenate([c_ctx[None, :], c, jnp.zeros((n_mod - 1 - N_LAT_SEQ, D_MODEL), F32)], axis=0)
    mod = _adaln(cvec, w_mod, b_mod).reshape(DEPTH, n_mod, 6, D_MODEL)

    xp = x_prompt.reshape(N_CTX_SEQ * CTX_LEN, D_MODEL)
    xs = x_sample.reshape(N_LAT_SEQ * LAT_LEN, D_MODEL)
    gf = g_final.reshape(1, D_MODEL)
    kv_layers = []
    for l in range(DEPTH):
        w_in_l = w_in[l].astype(BF16)
        pool_w_l = pool_w[l].astype(BF16)
        pool_s_l = pool_scale[l].reshape(1, POOL_W)
        w_br_l = w_branch[l].astype(BF16)
        w_out_l = w_out[l].astype(BF16)
        w_up_l = ffn_w_up[l].astype(BF16)
        w_dn_l = ffn_w_down[l].astype(BF16)
        g1 = g_norm1[l].reshape(1, D_MODEL)
        g2 = g_norm2[l].reshape(1, D_MODEL)
        mod_ctx = mod[l, 0:1]
        mod_lat = mod[l, 1:1 + N_LAT_SEQ]
        final = l == DEPTH - 1

        yc, qkv, d, gates, kv = _inproj(xp, mod_ctx, g1, w_in_l, conv_w[l], seq_len=CTX_LEN, with_kv=True,
                                        tile=CTX_IN_TILE)
        ya = _ctx_attention(qkv)
        xp = _mix(xp, mod_ctx, yc, ya, d, gates, pool_w_l, pool_s_l, w_br_l, w_out_l)
        xp = _ffn(xp, mod_ctx, g2, w_up_l, ffn_conv[l], w_dn_l, gf, seq_len=CTX_LEN, final=final)
        kv_layers.append(kv)

        k_ctx = cache_kv[:, l, 0].transpose(0, 2, 1, 3).reshape(N_LAT_SEQ, PAST_LEN, ATTN_W).astype(BF16)
        v_ctx = cache_kv[:, l, 1].transpose(0, 2, 1, 3).reshape(N_LAT_SEQ, PAST_LEN, ATTN_W).astype(BF16)
        yc, qkv, d, gates = _inproj(xs, mod_lat, g1, w_in_l, conv_w[l], seq_len=LAT_LEN, with_kv=False,
                                    tile=TOK_TILE)
        ya = _lat_attention(qkv, k_ctx, v_ctx, _row_pair_table(rpb[l]))
        xs = _mix(xs, mod_lat, yc, ya, d, gates, pool_w_l, pool_s_l, w_br_l, w_out_l)
        xs = _ffn(xs, mod_lat, g2, w_up_l, ffn_conv[l], w_dn_l, gf, seq_len=LAT_LEN, final=final)

    y_prompt = xp.reshape(N_CTX_SEQ, CTX_LEN, D_MODEL)
    y_sample = xs.reshape(N_LAT_SEQ, LAT_LEN, D_MODEL)
    kv_state = jnp.stack(kv_layers, axis=1)
    return (y_prompt, y_sample, kv_state)
```

```python
import functools

import numpy as np
import jax
import jax.numpy as jnp
from jax import lax
from jax.experimental import pallas as pl
from jax.experimental.pallas import tpu as pltpu

F32 = jnp.float32
BF16 = jnp.bfloat16

D_MODEL = 1024
N_CTX_SEQ = 16
CTX_LEN = 256
DEPTH = 2
N_LAT_SEQ = 4
LAT_LEN = 1024
PAST_LEN = 256
GRID_W = 64
GRID_ROWS = LAT_LEN // GRID_W
CONV_W = 512
N_HEADS = 8
HEAD_DIM = 64
ATTN_W = N_HEADS * HEAD_DIM
POOL_W = 512
POOL_SIZES = (2, 4, 8, 16)
POOL_GROUP = POOL_W // len(POOL_SIZES)
N_BRANCH = 3
WIN_H = 8
WIN_W = 16
D_FF = 2816
EPS = 1e-6
IN_W = 3 * CONV_W + 3 * ATTN_W + POOL_W + N_BRANCH * D_MODEL

LANES = 128
TOK_TILE = 1024
CTX_IN_TILE = 512
IN_CHUNK = 512
FF_CHUNK = 256
N_FF_CHUNKS = D_FF // FF_CHUNK
MIX_TILE = 512
VMEM_LIMIT = 56 * 1024 * 1024


def _params(*sem):
    return pltpu.CompilerParams(dimension_semantics=sem, vmem_limit_bytes=VMEM_LIMIT)


def _resident(shape):
    return pl.BlockSpec(shape, lambda *_: (0,) * len(shape), pipeline_mode=pl.Buffered(1))


def _mod_spec(mod, ntok, tile):
    tiles_per_mod = ntok // mod.shape[0] // tile
    return pl.BlockSpec((1, 6, D_MODEL), lambda i, *_: (i // tiles_per_mod, 0, 0))


def _norm_mod(x, g, shift, scale):
    ms = jnp.mean(x * x, axis=-1, keepdims=True)
    return (x * lax.rsqrt(ms + EPS) * g) * (1.0 + scale) + shift


def _row_in_seq(shape, seq_len):
    return lax.broadcasted_iota(jnp.int32, shape, 0) & (seq_len - 1)


def _shift_rows(x, s, row, seq_len):
    y = pltpu.roll(x, s % x.shape[0], 0)
    ok = (row >= s) if s > 0 else (row < seq_len + s)
    return jnp.where(ok, y, 0.0)


def _dwconv3(x, w_ref, row, seq_len):
    return (_shift_rows(x, 1, row, seq_len) * w_ref[0:1, :] + x * w_ref[1:2, :]
            + _shift_rows(x, -1, row, seq_len) * w_ref[2:3, :])


def _pool_delta(p, seq_len):
    row = _row_in_seq((p.shape[0], POOL_GROUP), seq_len)
    outs = []
    for gi, w in enumerate(POOL_SIZES):
        pg = p[:, gi * POOL_GROUP:(gi + 1) * POOL_GROUP]
        fwd = bwd = pg
        k = 1
        while k < w // 2:
            fwd = fwd + _shift_rows(fwd, -k, row, seq_len)
            bwd = bwd + _shift_rows(bwd, k, row, seq_len)
            k *= 2
        acc = fwd + _shift_rows(bwd, 1, row, seq_len)
        lo = jnp.maximum(row - w // 2, 0)
        hi = jnp.minimum(row - w // 2 + w, seq_len)
        outs.append(acc / (hi - lo).astype(F32) - pg)
    return jnp.concatenate(outs, axis=-1)


def _adaln_kernel(c_ref, w_ref, b_ref, o_ref):
    c = c_ref[...]
    s = c * jax.nn.sigmoid(c)
    o_ref[0] = jnp.dot(s.astype(BF16), w_ref[0].astype(BF16), preferred_element_type=F32) + b_ref[0]


def _adaln(cvec, w_mod, b_mod):
    nb = cvec.shape[0]
    n_out = 6 * D_MODEL
    blk = 1024
    return pl.pallas_call(
        _adaln_kernel,
        grid=(DEPTH, n_out // blk),
        in_specs=[
            pl.BlockSpec((nb, D_MODEL), lambda l, n: (0, 0)),
            pl.BlockSpec((1, D_MODEL, blk), lambda l, n: (l, 0, n)),
            pl.BlockSpec((1, 1, blk), lambda l, n: (l, 0, n)),
        ],
        out_specs=pl.BlockSpec((1, nb, blk), lambda l, n: (l, 0, n)),
        out_shape=jax.ShapeDtypeStruct((DEPTH, nb, n_out), F32),
        compiler_params=_params("arbitrary", "arbitrary"),
        name="adaln",
    )(cvec, w_mod, b_mod.reshape(DEPTH, 1, n_out))


COL_CONV = 0
COL_QKV = 3 * CONV_W
COL_POOL = COL_QKV + 3 * ATTN_W
COL_GATE = COL_POOL + POOL_W
GATE_PHASES = 2
GATE_BLOCK = N_BRANCH * D_MODEL // GATE_PHASES


def _store_heads_lane_dense(z_scr, dst, seq_len):
    half = seq_len // 2
    first_half = lax.broadcasted_iota(jnp.int32, (half, LANES), 1) < HEAD_DIM
    for s in range(z_scr.shape[1] // seq_len):
        for hp in range(N_HEADS // 2):
            e = z_scr[hp, pl.ds(s * seq_len, half, stride=2), :]
            o = z_scr[hp, pl.ds(s * seq_len + 1, half, stride=2), :]
            dst(s, 2 * hp)[...] = jnp.where(first_half, e, pltpu.roll(o, HEAD_DIM, 1))
            dst(s, 2 * hp + 1)[...] = jnp.where(first_half, pltpu.roll(e, HEAD_DIM, 1), o)


def _inproj_kernel(*refs, seq_len, kv_mode):
    x_ref, mod_ref, g_ref, w_ref, cw_ref = refs[:5]
    refs = refs[5:]
    if kv_mode == "stack":
        kv_prev_ref, refs = refs[0], refs[1:]
    yconv_ref, qkv_ref, d_ref, gates_ref = refs[:4]
    if kv_mode is None:
        (h_scr,) = refs[4:]
    else:
        kv_ref, h_scr, z_scr = refs[4:]
    j = pl.program_id(1)

    def proj(col):
        return jnp.dot(h_scr[...], w_ref[:, col:col + IN_CHUNK], preferred_element_type=F32)

    @pl.when(j == 0)
    def _():
        h = _norm_mod(x_ref[...], g_ref[...], mod_ref[0, 0:1, :], mod_ref[0, 1:2, :])
        h_scr[...] = h.astype(BF16)
        row = _row_in_seq((x_ref.shape[0], IN_CHUNK), seq_len)
        b_gate = proj(COL_CONV)
        conv = _dwconv3(proj(COL_CONV + CONV_W) * proj(COL_CONV + 2 * CONV_W), cw_ref, row, seq_len)
        yconv_ref[...] = (b_gate * conv).astype(BF16)
        qkv_ref[:, 0:ATTN_W] = (proj(COL_QKV) * (HEAD_DIM ** -0.5)).astype(BF16)
        for kv in range(2):
            z = proj(COL_QKV + (1 + kv) * ATTN_W)
            qkv_ref[:, (1 + kv) * ATTN_W:(2 + kv) * ATTN_W] = z.astype(BF16)
            if kv_mode is not None:
                for hp in range(N_HEADS // 2):
                    z_scr[hp] = z[:, hp * LANES:(hp + 1) * LANES]
                if kv_mode == "stack":
                    dst = lambda s, h_i, kv=kv: kv_ref.at[s, 1, kv, h_i]
                else:
                    dst = lambda s, h_i, kv=kv: kv_ref.at[s, kv, h_i]
                _store_heads_lane_dense(z_scr, dst, seq_len)
        if kv_mode == "stack":
            kv_ref[:, 0] = kv_prev_ref[...]
        d_ref[...] = _pool_delta(proj(COL_POOL), seq_len).astype(BF16)

    for phase in range(GATE_PHASES):
        @pl.when(j == 1 + phase)
        def _():
            for c in range(GATE_BLOCK // IN_CHUNK):
                z = proj(COL_GATE + phase * GATE_BLOCK + c * IN_CHUNK)
                gates_ref[:, c * IN_CHUNK:(c + 1) * IN_CHUNK] = jax.nn.sigmoid(z).astype(BF16)


def _inproj(x, mod, g1, w_in, conv_w, kv_prev=None, *, seq_len, kv_mode, tile):
    ntok = x.shape[0]
    first = lambda i, j: (i, 0)
    assert (kv_prev is not None) == (kv_mode == "stack")
    out_shape = [
        jax.ShapeDtypeStruct((ntok, CONV_W), BF16),
        jax.ShapeDtypeStruct((ntok, 3 * ATTN_W), BF16),
        jax.ShapeDtypeStruct((ntok, POOL_W), BF16),
        jax.ShapeDtypeStruct((ntok, N_BRANCH * D_MODEL), BF16),
    ]
    out_specs = [
        pl.BlockSpec((tile, CONV_W), first),
        pl.BlockSpec((tile, 3 * ATTN_W), first),
        pl.BlockSpec((tile, POOL_W), first),
        pl.BlockSpec((tile, GATE_BLOCK), lambda i, j: (i, jnp.maximum(j - 1, 0))),
    ]
    in_specs = [
        pl.BlockSpec((tile, D_MODEL), first),
        _mod_spec(mod, ntok, tile),
        _resident((1, D_MODEL)),
        _resident((D_MODEL, IN_W)),
        _resident((3, CONV_W)),
    ]
    operands = [x, mod, g1, w_in, conv_w]
    scratch = [pltpu.VMEM((tile, D_MODEL), BF16)]
    if kv_mode is not None:
        spt = tile // seq_len
        slab = (N_HEADS, seq_len * HEAD_DIM // LANES, LANES)
        one_layer = (spt, 2) + slab
        scratch.append(pltpu.VMEM((N_HEADS // 2, tile, LANES), F32))
        if kv_mode == "stack":
            in_specs.append(pl.BlockSpec(one_layer, lambda i, j: (i, 0, 0, 0, 0)))
            operands.append(kv_prev)
            out_shape.append(jax.ShapeDtypeStruct((ntok // seq_len, 2, 2) + slab, F32))
            out_specs.append(pl.BlockSpec((spt, 2, 2) + slab, lambda i, j: (i, 0, 0, 0, 0, 0)))
        else:
            out_shape.append(jax.ShapeDtypeStruct((ntok // seq_len, 2) + slab, F32))
            out_specs.append(pl.BlockSpec(one_layer, lambda i, j: (i, 0, 0, 0, 0)))
    return pl.pallas_call(
        functools.partial(_inproj_kernel, seq_len=seq_len, kv_mode=kv_mode),
        grid=(ntok // tile, 1 + GATE_PHASES),
        in_specs=in_specs,
        out_specs=out_specs,
        out_shape=out_shape,
        scratch_shapes=scratch,
        compiler_params=_params("arbitrary", "arbitrary"),
        name="inproj_lat" if kv_mode is None else "inproj_ctx",
    )(*operands)


def _attend_pair(q2, kvb):
    lane = lax.broadcasted_iota(jnp.int32, (1, LANES), 1)
    out = None
    for half in range(2):
        m = (lane >= half * HEAD_DIM) & (lane < (half + 1) * HEAD_DIM)
        qh = jnp.where(m, q2, jnp.zeros_like(q2))
        scores = []
        for k2, _, bias in kvb:
            s = lax.dot_general(qh, k2, (((1,), (1,)), ((), ())), preferred_element_type=F32)
            if bias is not None:
                s = s + bias[half]
            scores.append(s)
        mx = functools.reduce(jnp.maximum, [jnp.max(s, axis=-1, keepdims=True) for s in scores])
        es = [jnp.exp(s - mx) for s in scores]
        den = functools.reduce(lambda a, b: a + b, [jnp.sum(e, axis=-1, keepdims=True) for e in es])
        acc = None
        for e, (_, v2, _) in zip(es, kvb):
            vh = jnp.where(m, v2, jnp.zeros_like(v2))
            o = jnp.dot(e.astype(BF16), vh, preferred_element_type=F32)
            acc = o if acc is None else acc + o
        acc = acc * (1.0 / den)
        out = acc if out is None else out + acc
    return out


def _ctx_attn_kernel(q_ref, k_ref, v_ref, o_ref):
    for hp in range(N_HEADS // 2):
        sl = slice(hp * LANES, (hp + 1) * LANES)
        o_ref[:, sl] = _attend_pair(q_ref[:, sl], [(k_ref[:, sl], v_ref[:, sl], None)]).astype(BF16)


def _ctx_attention(qkv):
    ntok = qkv.shape[0]
    return pl.pallas_call(
        _ctx_attn_kernel,
        grid=(ntok // CTX_LEN,),
        in_specs=[
            pl.BlockSpec((CTX_LEN, ATTN_W), lambda b: (b, 0)),
            pl.BlockSpec((CTX_LEN, ATTN_W), lambda b: (b, 1)),
            pl.BlockSpec((CTX_LEN, ATTN_W), lambda b: (b, 2)),
        ],
        out_specs=pl.BlockSpec((CTX_LEN, ATTN_W), lambda b: (b, 0)),
        out_shape=jax.ShapeDtypeStruct((ntok, ATTN_W), BF16),
        compiler_params=_params("arbitrary"),
        name="attn_ctx",
    )(qkv, qkv, qkv)


Q_ROWS = 4
Q_BLOCK = Q_ROWS * GRID_W
N_Q_BLOCKS = GRID_ROWS // Q_ROWS
KEY_ROWS = 12
N_KEYS = KEY_ROWS * GRID_W
N_REL_ROWS = 2 * WIN_H - 1
N_ROW_PAIRS = N_REL_ROWS + 1


def _first_key_row(m):
    return min(max(Q_ROWS * m - WIN_H // 2, 0), GRID_ROWS - KEY_ROWS)


def _fill_bias(rp_ref, bias_scr):
    shape = (N_ROW_PAIRS * GRID_W, LANES)
    qc = lax.broadcasted_iota(jnp.int32, shape, 0) & (GRID_W - 1)
    lane = lax.broadcasted_iota(jnp.int32, shape, 1)
    kc = lane & (GRID_W - 1)
    col_start = jnp.clip(qc - WIN_W // 2, 0, GRID_W - WIN_W)
    valid = (kc >= col_start) & (kc < col_start + WIN_W)
    first_half = lax.broadcasted_iota(jnp.int32, (GRID_W, LANES), 1) < GRID_W
    for h in range(N_HEADS):
        t = jnp.broadcast_to(rp_ref[h][:, None, :], (N_ROW_PAIRS, GRID_W, LANES)).reshape(shape)
        t = pltpu.roll(t, LANES - (WIN_W - 1), 1)
        for bit in range(6):
            t = jnp.where(((qc >> bit) & 1) == 1, pltpu.roll(t, 1 << bit, 1), t)
        t = jnp.where(valid, t, -jnp.inf)
        for m in range(N_Q_BLOCKS):
            for ql in range(Q_ROWS):
                qr = Q_ROWS * m + ql
                win0 = min(max(qr - WIN_H // 2, 0), GRID_ROWS - WIN_H)
                for u in range(KEY_ROWS // 2):
                    kr = _first_key_row(m) + 2 * u
                    in0 = win0 <= kr < win0 + WIN_H
                    in1 = win0 <= kr + 1 < win0 + WIN_H
                    if in0 or in1:
                        pair = kr - qr + WIN_H
                        tile = t[pair * GRID_W:(pair + 1) * GRID_W]
                        if not in0:
                            tile = jnp.where(first_half, -jnp.inf, tile)
                        if not in1:
                            tile = jnp.where(first_half, tile, -jnp.inf)
                    else:
                        tile = jnp.full((GRID_W, LANES), -jnp.inf, F32)
                    bias_scr[h, m, ql * GRID_W:(ql + 1) * GRID_W, u * LANES:(u + 1) * LANES] = tile


def _lat_attn_kernel(q_ref, k_ref, v_ref, cache_ref, rp_ref, o_ref, bias_scr):
    hp = pl.program_id(0)
    m = pl.program_id(1)

    @pl.when((hp == 0) & (m == 0))
    def _():
        _fill_bias(rp_ref, bias_scr)

    row0 = jnp.clip(Q_ROWS * m - WIN_H // 2, 0, GRID_ROWS - KEY_ROWS)
    start = pl.multiple_of(row0 * GRID_W, Q_BLOCK)
    for s in range(q_ref.shape[0]):
        kc, vc = (jnp.concatenate([cache_ref[s, i, 0], cache_ref[s, i, 1]], axis=-1).astype(BF16)
                  for i in range(2))
        kvb = [
            (k_ref[s, pl.ds(start, N_KEYS), :], v_ref[s, pl.ds(start, N_KEYS), :],
             (bias_scr[2 * hp, m], bias_scr[2 * hp + 1, m])),
            (kc, vc, None),
        ]
        o_ref[s] = _attend_pair(q_ref[s], kvb).astype(BF16)


def _lat_attention(qkv, cache_kv, layer, rp):
    nseq = qkv.shape[0] // LAT_LEN
    qkv = qkv.reshape(nseq, LAT_LEN, 3 * ATTN_W)
    n_hp = ATTN_W // LANES
    out = pl.pallas_call(
        _lat_attn_kernel,
        grid=(n_hp, N_Q_BLOCKS),
        in_specs=[
            pl.BlockSpec((nseq, Q_BLOCK, LANES), lambda hp, m: (0, m, hp)),
            pl.BlockSpec((nseq, LAT_LEN, LANES), lambda hp, m: (0, 0, n_hp + hp)),
            pl.BlockSpec((nseq, LAT_LEN, LANES), lambda hp, m: (0, 0, 2 * n_hp + hp)),
            pl.BlockSpec((nseq, None, 2, 2, PAST_LEN, HEAD_DIM), lambda hp, m: (0, layer, 0, hp, 0, 0)),
            _resident(rp.shape),
        ],
        out_specs=pl.BlockSpec((nseq, Q_BLOCK, LANES), lambda hp, m: (0, m, hp)),
        out_shape=jax.ShapeDtypeStruct((nseq, LAT_LEN, ATTN_W), BF16),
        scratch_shapes=[pltpu.VMEM((N_HEADS, N_Q_BLOCKS, Q_BLOCK, N_KEYS), F32)],
        compiler_params=_params("arbitrary", "arbitrary"),
        name="attn_lat",
    )(qkv, qkv, qkv, cache_kv, rp)
    return out.reshape(nseq * LAT_LEN, ATTN_W)


def _row_pair_table(rpb_l):
    lane_pad = GRID_W - rpb_l.shape[-1]
    lo = jnp.pad(rpb_l, ((0, 0), (1, 0), (0, lane_pad)))
    hi = jnp.pad(rpb_l, ((0, 0), (0, 1), (0, lane_pad)))
    return jnp.concatenate([lo, hi], axis=-1)


def _mix_kernel(x_ref, mod_ref, yc_ref, ya_ref, d_ref, gates_ref, pw_ref, ps_ref, wb_ref, wo_ref, o_ref):
    yp = [jnp.dot(d_ref[:, g * POOL_GROUP:(g + 1) * POOL_GROUP], pw_ref[g], preferred_element_type=F32)
          for g in range(len(POOL_SIZES))]
    y_pool = (jnp.concatenate(yp, axis=-1) * ps_ref[...]).astype(BF16)
    merged = None
    for i, y in enumerate((yc_ref[...], ya_ref[...], y_pool)):
        proj = jnp.dot(y, wb_ref[i], preferred_element_type=F32)
        term = gates_ref[:, i * D_MODEL:(i + 1) * D_MODEL].astype(F32) * proj
        merged = term if merged is None else merged + term
    mix = jnp.dot(merged.astype(BF16), wo_ref[...], preferred_element_type=F32)
    o_ref[...] = x_ref[...] + mod_ref[0, 2:3, :] * mix


def _mix(x, mod, yconv, yattn, d, gates, pool_w, pool_scale, w_branch, w_out):
    ntok = x.shape[0]
    row = lambda i: (i, 0)
    const2 = lambda i: (0, 0)
    const3 = lambda i: (0, 0, 0)
    return pl.pallas_call(
        _mix_kernel,
        grid=(ntok // MIX_TILE,),
        in_specs=[
            pl.BlockSpec((MIX_TILE, D_MODEL), row),
            _mod_spec(mod, ntok, MIX_TILE),
            pl.BlockSpec((MIX_TILE, CONV_W), row),
            pl.BlockSpec((MIX_TILE, ATTN_W), row),
            pl.BlockSpec((MIX_TILE, POOL_W), row),
            pl.BlockSpec((MIX_TILE, N_BRANCH * D_MODEL), row),
            pl.BlockSpec((len(POOL_SIZES), POOL_GROUP, POOL_GROUP), const3),
            pl.BlockSpec((1, POOL_W), const2),
            pl.BlockSpec((N_BRANCH, CONV_W, D_MODEL), const3),
            pl.BlockSpec((D_MODEL, D_MODEL), const2),
        ],
        out_specs=pl.BlockSpec((MIX_TILE, D_MODEL), row),
        out_shape=jax.ShapeDtypeStruct((ntok, D_MODEL), F32),
        compiler_params=_params("arbitrary"),
        name="mix",
    )(x, mod, yconv, yattn, d, gates, pool_w, pool_scale, w_branch, w_out)


def _ffn_kernel(x_ref, mod_ref, g2_ref, wu_ref, fc_ref, wd_ref, gf_ref, o_ref, h_scr, act_scr,
                *, seq_len, final):
    h = _norm_mod(x_ref[...], g2_ref[...], mod_ref[0, 3:4, :], mod_ref[0, 4:5, :])
    h_scr[...] = h.astype(BF16)
    row = _row_in_seq((x_ref.shape[0], FF_CHUNK), seq_len)
    for c in range(N_FF_CHUNKS):
        sl = slice(c * FF_CHUNK, (c + 1) * FF_CHUNK)
        sl_val = slice(D_FF + c * FF_CHUNK, D_FF + (c + 1) * FF_CHUNK)
        u = jnp.dot(h_scr[...], wu_ref[:, sl], preferred_element_type=F32)
        val = jnp.dot(h_scr[...], wu_ref[:, sl_val], preferred_element_type=F32)
        u = _dwconv3(u, fc_ref.at[:, sl], row, seq_len)
        act_scr[:, sl] = (jax.nn.gelu(u, approximate=True) * val).astype(BF16)
    y = jnp.dot(act_scr[...], wd_ref[...], preferred_element_type=F32)
    xn = x_ref[...] + mod_ref[0, 5:6, :] * y
    if final:
        ms = jnp.mean(xn * xn, axis=-1, keepdims=True)
        xn = xn * lax.rsqrt(ms + EPS) * gf_ref[...]
    o_ref[...] = xn


def _ffn(x, mod, g2, w_up, f_conv, w_down, g_final, *, seq_len, final):
    ntok = x.shape[0]
    return pl.pallas_call(
        functools.partial(_ffn_kernel, seq_len=seq_len, final=final),
        grid=(ntok // TOK_TILE,),
        in_specs=[
            pl.BlockSpec((TOK_TILE, D_MODEL), lambda i: (i, 0)),
            _mod_spec(mod, ntok, TOK_TILE),
            _resident((1, D_MODEL)),
            _resident((D_MODEL, 2 * D_FF)),
            _resident((3, D_FF)),
            _resident((D_FF, D_MODEL)),
            _resident((1, D_MODEL)),
        ],
        out_specs=pl.BlockSpec((TOK_TILE, D_MODEL), lambda i: (i, 0)),
        out_shape=jax.ShapeDtypeStruct((ntok, D_MODEL), F32),
        scratch_shapes=[pltpu.VMEM((TOK_TILE, D_MODEL), BF16), pltpu.VMEM((TOK_TILE, D_FF), BF16)],
        compiler_params=_params("arbitrary"),
        name="ffn",
    )(x, mod, g2, w_up, f_conv, w_down, g_final)


def kernel(x_prompt, x_sample, cache_kv, c, c_ctx, w_mod, b_mod, g_norm1, g_norm2, w_in, conv_w, rpb,
           pool_w, pool_scale, w_branch, w_out, ffn_w_up, ffn_conv, ffn_w_down, g_final):
    n_mod = 8
    cvec = jnp.concatenate([c_ctx[None, :], c, jnp.zeros((n_mod - 1 - N_LAT_SEQ, D_MODEL), F32)], axis=0)
    mod = _adaln(cvec, w_mod, b_mod).reshape(DEPTH, n_mod, 6, D_MODEL)

    xp = x_prompt.reshape(N_CTX_SEQ * CTX_LEN, D_MODEL)
    xs = x_sample.reshape(N_LAT_SEQ * LAT_LEN, D_MODEL)
    gf = g_final.reshape(1, D_MODEL)
    assert DEPTH == 2
    kv = None
    for l in range(DEPTH):
        w_in_l = w_in[l].astype(BF16)
        pool_w_l = pool_w[l].astype(BF16)
        pool_s_l = pool_scale[l].reshape(1, POOL_W)
        w_br_l = w_branch[l].astype(BF16)
        w_out_l = w_out[l].astype(BF16)
        w_up_l = ffn_w_up[l].astype(BF16)
        w_dn_l = ffn_w_down[l].astype(BF16)
        g1 = g_norm1[l].reshape(1, D_MODEL)
        g2 = g_norm2[l].reshape(1, D_MODEL)
        mod_ctx = mod[l, 0:1]
        mod_lat = mod[l, 1:1 + N_LAT_SEQ]
        final = l == DEPTH - 1

        yc, qkv, d, gates, kv = _inproj(xp, mod_ctx, g1, w_in_l, conv_w[l], kv, seq_len=CTX_LEN,
                                        kv_mode="new" if l == 0 else "stack", tile=CTX_IN_TILE)
        ya = _ctx_attention(qkv)
        xp = _mix(xp, mod_ctx, yc, ya, d, gates, pool_w_l, pool_s_l, w_br_l, w_out_l)
        xp = _ffn(xp, mod_ctx, g2, w_up_l, ffn_conv[l], w_dn_l, gf, seq_len=CTX_LEN, final=final)

        yc, qkv, d, gates = _inproj(xs, mod_lat, g1, w_in_l, conv_w[l], seq_len=LAT_LEN, kv_mode=None,
                                    tile=TOK_TILE)
        ya = _lat_attention(qkv, cache_kv, l, _row_pair_table(rpb[l]))
        xs = _mix(xs, mod_lat, yc, ya, d, gates, pool_w_l, pool_s_l, w_br_l, w_out_l)
        xs = _ffn(xs, mod_lat, g2, w_up_l, ffn_conv[l], w_dn_l, gf, seq_len=LAT_LEN, final=final)

    y_prompt = xp.reshape(N_CTX_SEQ, CTX_LEN, D_MODEL)
    y_sample = xs.reshape(N_LAT_SEQ, LAT_LEN, D_MODEL)
    kv_state = kv.reshape(N_CTX_SEQ, DEPTH, 2, N_HEADS, CTX_LEN, HEAD_DIM)
    return (y_prompt, y_sample, kv_state)
```

```python
import functools

import jax
import jax.numpy as jnp
from jax import lax
from jax.experimental import pallas as pl
from jax.experimental.pallas import tpu as pltpu

F32 = jnp.float32
BF16 = jnp.bfloat16

D_MODEL = 1024
N_CTX_SEQ = 16
CTX_LEN = 256
DEPTH = 2
N_LAT_SEQ = 4
LAT_LEN = 1024
PAST_LEN = 256
GRID_W = 64
GRID_ROWS = LAT_LEN // GRID_W
CONV_W = 512
N_HEADS = 8
HEAD_DIM = 64
ATTN_W = N_HEADS * HEAD_DIM
POOL_W = 512
POOL_SIZES = (2, 4, 8, 16)
POOL_GROUP = POOL_W // len(POOL_SIZES)
N_BRANCH = 3
WIN_H = 8
WIN_W = 16
D_FF = 2816
EPS = 1e-6
IN_W = 3 * CONV_W + 3 * ATTN_W + POOL_W + N_BRANCH * D_MODEL

LANES = 128
TOK_TILE = 1024
CTX_IN_TILE = 512
IN_CHUNK = 512
FF_CHUNK = 256
N_FF_CHUNKS = D_FF // FF_CHUNK
MIX_TILE = 512
VMEM_LIMIT = 56 * 1024 * 1024


def _params(*sem):
    return pltpu.CompilerParams(dimension_semantics=sem, vmem_limit_bytes=VMEM_LIMIT)


def _resident(shape):
    return pl.BlockSpec(shape, lambda *_: (0,) * len(shape), pipeline_mode=pl.Buffered(1))


HBM_SPEC = pl.BlockSpec(memory_space=pltpu.HBM)


def _mod_spec(mod, ntok, tile):
    tiles_per_mod = ntok // mod.shape[0] // tile
    return pl.BlockSpec((1, 6, D_MODEL), lambda i, *_: (i // tiles_per_mod, 0, 0))


class _CastStream:
    def __init__(self, chunks, stage, sem):
        self.chunks, self.stage, self.sem = chunks, stage, sem
        self.next = 0

    def _copy(self, k):
        return pltpu.make_async_copy(self.chunks[k][0], self.stage.at[k % 2], self.sem.at[k % 2])

    def prime(self):
        for k in range(min(2, len(self.chunks))):
            self._copy(k).start()

    def ready(self, k):
        assert k == self.next, (k, self.next)
        self.next += 1
        self._copy(k).wait()
        self.chunks[k][1][...] = self.stage[k % 2].astype(BF16)
        if k + 2 < len(self.chunks):
            self._copy(k + 2).start()


def _first_tile_variants(i, body):
    pl.when(i == 0)(functools.partial(body, True))
    pl.when(i > 0)(functools.partial(body, False))


def _norm_mod(x, g, shift, scale):
    ms = jnp.mean(x * x, axis=-1, keepdims=True)
    return (x * lax.rsqrt(ms + EPS) * g) * (1.0 + scale) + shift


def _row_in_seq(shape, seq_len):
    return lax.broadcasted_iota(jnp.int32, shape, 0) & (seq_len - 1)


def _shift_rows(x, s, row, seq_len):
    y = pltpu.roll(x, s % x.shape[0], 0)
    ok = (row >= s) if s > 0 else (row < seq_len + s)
    return jnp.where(ok, y, 0.0)


def _dwconv3(x, w_ref, row, seq_len):
    return (_shift_rows(x, 1, row, seq_len) * w_ref[0:1, :] + x * w_ref[1:2, :]
            + _shift_rows(x, -1, row, seq_len) * w_ref[2:3, :])


def _pool_delta(p, seq_len):
    row = _row_in_seq((p.shape[0], POOL_GROUP), seq_len)
    outs = []
    for gi, w in enumerate(POOL_SIZES):
        pg = p[:, gi * POOL_GROUP:(gi + 1) * POOL_GROUP]
        fwd = bwd = pg
        k = 1
        while k < w // 2:
            fwd = fwd + _shift_rows(fwd, -k, row, seq_len)
            bwd = bwd + _shift_rows(bwd, k, row, seq_len)
            k *= 2
        acc = fwd + _shift_rows(bwd, 1, row, seq_len)
        lo = jnp.maximum(row - w // 2, 0)
        hi = jnp.minimum(row - w // 2 + w, seq_len)
        outs.append(acc / (hi - lo).astype(F32) - pg)
    return jnp.concatenate(outs, axis=-1)


def _adaln_kernel(c_ref, w_ref, b_ref, o_ref):
    c = c_ref[...]
    s = c * jax.nn.sigmoid(c)
    o_ref[0] = jnp.dot(s.astype(BF16), w_ref[0].astype(BF16), preferred_element_type=F32) + b_ref[0]


def _adaln(cvec, w_mod, b_mod):
    nb = cvec.shape[0]
    n_out = 6 * D_MODEL
    blk = 1024
    return pl.pallas_call(
        _adaln_kernel,
        grid=(DEPTH, n_out // blk),
        in_specs=[
            pl.BlockSpec((nb, D_MODEL), lambda l, n: (0, 0)),
            pl.BlockSpec((1, D_MODEL, blk), lambda l, n: (l, 0, n)),
            pl.BlockSpec((1, 1, blk), lambda l, n: (l, 0, n)),
        ],
        out_specs=pl.BlockSpec((1, nb, blk), lambda l, n: (l, 0, n)),
        out_shape=jax.ShapeDtypeStruct((DEPTH, nb, n_out), F32),
        compiler_params=_params("arbitrary", "arbitrary"),
        name="adaln",
    )(cvec, w_mod, b_mod.reshape(DEPTH, 1, n_out))


COL_CONV = 0
COL_QKV = 3 * CONV_W
COL_POOL = COL_QKV + 3 * ATTN_W
COL_GATE = COL_POOL + POOL_W
GATE_PHASES = 2
GATE_BLOCK = N_BRANCH * D_MODEL // GATE_PHASES


def _inproj_kernel(*refs, seq_len, kv_mode, layer):
    x_ref, mod_ref, g_ref, w_hbm, cw_ref = refs[:5]
    refs = refs[5:]
    if kv_mode == "stack":
        kv_prev_hbm, refs = refs[0], refs[1:]
    yconv_ref, qkv_ref, d_ref, gates_ref = refs[:4]
    refs = refs[4:]
    if kv_mode is not None:
        kv_out, refs = refs[0], refs[1:]
    h_scr, w_bf, stage, sem = refs[:4]
    i = pl.program_id(0)
    j = pl.program_id(1)
    tile = x_ref.shape[0]
    spt = tile // seq_len

    stream = _CastStream(
        [(w_hbm.at[layer, :, pl.ds(c * IN_CHUNK, IN_CHUNK)], w_bf.at[:, pl.ds(c * IN_CHUNK, IN_CHUNK)])
         for c in range(IN_W // IN_CHUNK)], stage, sem)

    if kv_mode == "stack":
        kv_scr, kv_sem = refs[4:]
        seqs = pl.ds(i * spt, spt)
        prev_copy = pltpu.make_async_copy(kv_prev_hbm.at[seqs], kv_out.at[seqs, 0], kv_sem.at[0])
        new_copy = pltpu.make_async_copy(kv_scr, kv_out.at[seqs, 1], kv_sem.at[1])
        kv_dst = kv_scr
    else:
        kv_dst = kv_out if kv_mode == "new" else None

    def proj(col, first):
        if first:
            stream.ready(col // IN_CHUNK)
        return jnp.dot(h_scr[...], w_bf[:, col:col + IN_CHUNK], preferred_element_type=F32)

    def mixer_phase(first):
        if first:
            stream.prime()
        if kv_mode == "stack":
            prev_copy.start()
        h = _norm_mod(x_ref[...], g_ref[...], mod_ref[0, 0:1, :], mod_ref[0, 1:2, :])
        h_scr[...] = h.astype(BF16)
        row = _row_in_seq((tile, IN_CHUNK), seq_len)
        b_gate = proj(COL_CONV, first)
        c_gate = proj(COL_CONV + CONV_W, first)
        conv = _dwconv3(c_gate * proj(COL_CONV + 2 * CONV_W, first), cw_ref, row, seq_len)
        yconv_ref[...] = (b_gate * conv).astype(BF16)
        qkv_ref[:, 0:ATTN_W] = (proj(COL_QKV, first) * (HEAD_DIM ** -0.5)).astype(BF16)
        for kv in range(2):
            z = proj(COL_QKV + (1 + kv) * ATTN_W, first)
            qkv_ref[:, (1 + kv) * ATTN_W:(2 + kv) * ATTN_W] = z.astype(BF16)
            if kv_dst is not None:
                for s in range(spt):
                    for h_i in range(N_HEADS):
                        kv_dst[s, kv, h_i] = z[s * seq_len:(s + 1) * seq_len,
                                               h_i * HEAD_DIM:(h_i + 1) * HEAD_DIM]
        if kv_mode == "stack":
            new_copy.start()
        d_ref[...] = _pool_delta(proj(COL_POOL, first), seq_len).astype(BF16)

    def gate_phase(phase, first):
        for c in range(GATE_BLOCK // IN_CHUNK):
            z = proj(COL_GATE + phase * GATE_BLOCK + c * IN_CHUNK, first)
            gates_ref[:, c * IN_CHUNK:(c + 1) * IN_CHUNK] = jax.nn.sigmoid(z).astype(BF16)
        if kv_mode == "stack" and phase == GATE_PHASES - 1:
            prev_copy.wait()
            new_copy.wait()

    def body(first):
        pl.when(j == 0)(functools.partial(mixer_phase, first))
        for phase in range(GATE_PHASES):
            pl.when(j == 1 + phase)(functools.partial(gate_phase, phase, first))

    _first_tile_variants(i, body)


def _inproj(x, mod, g1, w_in, conv_w, kv_prev=None, *, seq_len, kv_mode, tile, layer):
    ntok = x.shape[0]
    first = lambda i, j: (i, 0)
    assert (kv_prev is not None) == (kv_mode == "stack")
    out_shape = [
        jax.ShapeDtypeStruct((ntok, CONV_W), BF16),
        jax.ShapeDtypeStruct((ntok, 3 * ATTN_W), BF16),
        jax.ShapeDtypeStruct((ntok, POOL_W), BF16),
        jax.ShapeDtypeStruct((ntok, N_BRANCH * D_MODEL), BF16),
    ]
    out_specs = [
        pl.BlockSpec((tile, CONV_W), first),
        pl.BlockSpec((tile, 3 * ATTN_W), first),
        pl.BlockSpec((tile, POOL_W), first),
        pl.BlockSpec((tile, GATE_BLOCK), lambda i, j: (i, jnp.maximum(j - 1, 0))),
    ]
    in_specs = [
        pl.BlockSpec((tile, D_MODEL), first),
        _mod_spec(mod, ntok, tile),
        _resident((1, D_MODEL)),
        HBM_SPEC,
        _resident((3, CONV_W)),
    ]
    operands = [x, mod, g1, w_in, conv_w]
    scratch = [
        pltpu.VMEM((tile, D_MODEL), BF16),
        pltpu.VMEM((D_MODEL, IN_W), BF16),
        pltpu.VMEM((2, D_MODEL, IN_CHUNK), F32),
        pltpu.SemaphoreType.DMA((2,)),
    ]
    nseq = ntok // seq_len
    spt = tile // seq_len
    one_layer = (2, N_HEADS, seq_len, HEAD_DIM)
    if kv_mode == "new":
        out_shape.append(jax.ShapeDtypeStruct((nseq,) + one_layer, F32))
        out_specs.append(pl.BlockSpec((spt,) + one_layer, lambda i, j: (i, 0, 0, 0, 0)))
    elif kv_mode == "stack":
        in_specs.append(HBM_SPEC)
        operands.append(kv_prev)
        out_shape.append(jax.ShapeDtypeStruct((nseq, 2) + one_layer, F32))
        out_specs.append(HBM_SPEC)
        scratch += [pltpu.VMEM((spt,) + one_layer, F32), pltpu.SemaphoreType.DMA((2,))]
    return pl.pallas_call(
        functools.partial(_inproj_kernel, seq_len=seq_len, kv_mode=kv_mode, layer=layer),
        grid=(ntok // tile, 1 + GATE_PHASES),
        in_specs=in_specs,
        out_specs=out_specs,
        out_shape=out_shape,
        scratch_shapes=scratch,
        compiler_params=_params("arbitrary", "arbitrary"),
        name="inproj_lat" if kv_mode is None else "inproj_ctx",
    )(*operands)


def _attend_pair(q2, kvb):
    lane = lax.broadcasted_iota(jnp.int32, (1, LANES), 1)
    out = None
    for half in range(2):
        m = (lane >= half * HEAD_DIM) & (lane < (half + 1) * HEAD_DIM)
        qh = jnp.where(m, q2, jnp.zeros_like(q2))
        scores = []
        for k2, _, bias in kvb:
            s = lax.dot_general(qh, k2, (((1,), (1,)), ((), ())), preferred_element_type=F32)
            if bias is not None:
                s = s + bias[half]
            scores.append(s)
        mx = functools.reduce(jnp.maximum, [jnp.max(s, axis=-1, keepdims=True) for s in scores])
        es = [jnp.exp(s - mx) for s in scores]
        den = functools.reduce(lambda a, b: a + b, [jnp.sum(e, axis=-1, keepdims=True) for e in es])
        acc = None
        for e, (_, v2, _) in zip(es, kvb):
            vh = jnp.where(m, v2, jnp.zeros_like(v2))
            o = jnp.dot(e.astype(BF16), vh, preferred_element_type=F32)
            acc = o if acc is None else acc + o
        acc = acc * (1.0 / den)
        out = acc if out is None else out + acc
    return out


def _ctx_attn_kernel(q_ref, k_ref, v_ref, o_ref):
    for hp in range(N_HEADS // 2):
        sl = slice(hp * LANES, (hp + 1) * LANES)
        o_ref[:, sl] = _attend_pair(q_ref[:, sl], [(k_ref[:, sl], v_ref[:, sl], None)]).astype(BF16)


def _ctx_attention(qkv):
    ntok = qkv.shape[0]
    return pl.pallas_call(
        _ctx_attn_kernel,
        grid=(ntok // CTX_LEN,),
        in_specs=[
            pl.BlockSpec((CTX_LEN, ATTN_W), lambda b: (b, 0)),
            pl.BlockSpec((CTX_LEN, ATTN_W), lambda b: (b, 1)),
            pl.BlockSpec((CTX_LEN, ATTN_W), lambda b: (b, 2)),
        ],
        out_specs=pl.BlockSpec((CTX_LEN, ATTN_W), lambda b: (b, 0)),
        out_shape=jax.ShapeDtypeStruct((ntok, ATTN_W), BF16),
        compiler_params=_params("arbitrary"),
        name="attn_ctx",
    )(qkv, qkv, qkv)


Q_ROWS = 4
Q_BLOCK = Q_ROWS * GRID_W
N_Q_BLOCKS = GRID_ROWS // Q_ROWS
KEY_ROWS = 12
N_KEYS = KEY_ROWS * GRID_W
N_REL_ROWS = 2 * WIN_H - 1
N_ROW_PAIRS = N_REL_ROWS + 1


def _first_key_row(m):
    return min(max(Q_ROWS * m - WIN_H // 2, 0), GRID_ROWS - KEY_ROWS)


def _fill_bias(rp_ref, bias_scr):
    shape = (N_ROW_PAIRS * GRID_W, LANES)
    qc = lax.broadcasted_iota(jnp.int32, shape, 0) & (GRID_W - 1)
    lane = lax.broadcasted_iota(jnp.int32, shape, 1)
    kc = lane & (GRID_W - 1)
    col_start = jnp.clip(qc - WIN_W // 2, 0, GRID_W - WIN_W)
    valid = (kc >= col_start) & (kc < col_start + WIN_W)
    first_half = lax.broadcasted_iota(jnp.int32, (GRID_W, LANES), 1) < GRID_W
    for h in range(N_HEADS):
        t = jnp.broadcast_to(rp_ref[h][:, None, :], (N_ROW_PAIRS, GRID_W, LANES)).reshape(shape)
        t = pltpu.roll(t, LANES - (WIN_W - 1), 1)
        for bit in range(6):
            t = jnp.where(((qc >> bit) & 1) == 1, pltpu.roll(t, 1 << bit, 1), t)
        t = jnp.where(valid, t, -jnp.inf)
        for m in range(N_Q_BLOCKS):
            for ql in range(Q_ROWS):
                qr = Q_ROWS * m + ql
                win0 = min(max(qr - WIN_H // 2, 0), GRID_ROWS - WIN_H)
                for u in range(KEY_ROWS // 2):
                    kr = _first_key_row(m) + 2 * u
                    in0 = win0 <= kr < win0 + WIN_H
                    in1 = win0 <= kr + 1 < win0 + WIN_H
                    if in0 or in1:
                        pair = kr - qr + WIN_H
                        tile = t[pair * GRID_W:(pair + 1) * GRID_W]
                        if not in0:
                            tile = jnp.where(first_half, -jnp.inf, tile)
                        if not in1:
                            tile = jnp.where(first_half, tile, -jnp.inf)
                    else:
                        tile = jnp.full((GRID_W, LANES), -jnp.inf, F32)
                    bias_scr[h, m, ql * GRID_W:(ql + 1) * GRID_W, u * LANES:(u + 1) * LANES] = tile


def _lat_attn_kernel(q_ref, k_ref, v_ref, cache_ref, rp_ref, o_ref, bias_scr):
    hp = pl.program_id(0)
    m = pl.program_id(1)

    @pl.when((hp == 0) & (m == 0))
    def _():
        _fill_bias(rp_ref, bias_scr)

    row0 = jnp.clip(Q_ROWS * m - WIN_H // 2, 0, GRID_ROWS - KEY_ROWS)
    start = pl.multiple_of(row0 * GRID_W, Q_BLOCK)
    for s in range(q_ref.shape[0]):
        kc, vc = (jnp.concatenate([cache_ref[s, i, 0], cache_ref[s, i, 1]], axis=-1).astype(BF16)
                  for i in range(2))
        kvb = [
            (k_ref[s, pl.ds(start, N_KEYS), :], v_ref[s, pl.ds(start, N_KEYS), :],
             (bias_scr[2 * hp, m], bias_scr[2 * hp + 1, m])),
            (kc, vc, None),
        ]
        o_ref[s] = _attend_pair(q_ref[s], kvb).astype(BF16)


def _lat_attention(qkv, cache_kv, layer, rp):
    nseq = qkv.shape[0] // LAT_LEN
    qkv = qkv.reshape(nseq, LAT_LEN, 3 * ATTN_W)
    n_hp = ATTN_W // LANES
    out = pl.pallas_call(
        _lat_attn_kernel,
        grid=(n_hp, N_Q_BLOCKS),
        in_specs=[
            pl.BlockSpec((nseq, Q_BLOCK, LANES), lambda hp, m: (0, m, hp)),
            pl.BlockSpec((nseq, LAT_LEN, LANES), lambda hp, m: (0, 0, n_hp + hp)),
            pl.BlockSpec((nseq, LAT_LEN, LANES), lambda hp, m: (0, 0, 2 * n_hp + hp)),
            pl.BlockSpec((nseq, None, 2, 2, PAST_LEN, HEAD_DIM), lambda hp, m: (0, layer, 0, hp, 0, 0)),
            _resident(rp.shape),
        ],
        out_specs=pl.BlockSpec((nseq, Q_BLOCK, LANES), lambda hp, m: (0, m, hp)),
        out_shape=jax.ShapeDtypeStruct((nseq, LAT_LEN, ATTN_W), BF16),
        scratch_shapes=[pltpu.VMEM((N_HEADS, N_Q_BLOCKS, Q_BLOCK, N_KEYS), F32)],
        compiler_params=_params("arbitrary", "arbitrary"),
        name="attn_lat",
    )(qkv, qkv, qkv, cache_kv, rp)
    return out.reshape(nseq * LAT_LEN, ATTN_W)


def _row_pair_table(rpb_l):
    lane_pad = GRID_W - rpb_l.shape[-1]
    lo = jnp.pad(rpb_l, ((0, 0), (1, 0), (0, lane_pad)))
    hi = jnp.pad(rpb_l, ((0, 0), (0, 1), (0, lane_pad)))
    return jnp.concatenate([lo, hi], axis=-1)


OUT_ROW_CHUNK = D_MODEL // 2


def _mix_kernel(x_ref, mod_ref, yc_ref, ya_ref, d_ref, gates_ref, pw_hbm, ps_ref, wb_hbm, wo_hbm, o_ref,
                pw_bf, wb_bf, wo_bf, stage, sem, pw_stage, pw_sem, *, layer):
    chunks = [(wb_hbm.at[layer, b], wb_bf.at[b]) for b in range(N_BRANCH)]
    chunks += [(wo_hbm.at[layer, pl.ds(r * OUT_ROW_CHUNK, OUT_ROW_CHUNK)],
                wo_bf.at[pl.ds(r * OUT_ROW_CHUNK, OUT_ROW_CHUNK)]) for r in range(D_MODEL // OUT_ROW_CHUNK)]
    stream = _CastStream(chunks, stage, sem)
    pool_copy = pltpu.make_async_copy(pw_hbm.at[layer], pw_stage, pw_sem)

    def body(first):
        if first:
            pool_copy.start()
            stream.prime()
            pool_copy.wait()
            pw_bf[...] = pw_stage[...].astype(BF16)
        yp = [jnp.dot(d_ref[:, g * POOL_GROUP:(g + 1) * POOL_GROUP], pw_bf[g], preferred_element_type=F32)
              for g in range(len(POOL_SIZES))]
        y_pool = (jnp.concatenate(yp, axis=-1) * ps_ref[...]).astype(BF16)
        merged = None
        for b, y in enumerate((yc_ref[...], ya_ref[...], y_pool)):
            if first:
                stream.ready(b)
            proj = jnp.dot(y, wb_bf[b], preferred_element_type=F32)
            term = gates_ref[:, b * D_MODEL:(b + 1) * D_MODEL].astype(F32) * proj
            merged = term if merged is None else merged + term
        if first:
            for r in range(D_MODEL // OUT_ROW_CHUNK):
                stream.ready(N_BRANCH + r)
        mix = jnp.dot(merged.astype(BF16), wo_bf[...], preferred_element_type=F32)
        o_ref[...] = x_ref[...] + mod_ref[0, 2:3, :] * mix

    _first_tile_variants(pl.program_id(0), body)


def _mix(x, mod, yconv, yattn, d, gates, pool_w, pool_scale, w_branch, w_out, *, layer):
    ntok = x.shape[0]
    row = lambda i: (i, 0)
    pool_shape = (len(POOL_SIZES), POOL_GROUP, POOL_GROUP)
    return pl.pallas_call(
        functools.partial(_mix_kernel, layer=layer),
        grid=(ntok // MIX_TILE,),
        in_specs=[
            pl.BlockSpec((MIX_TILE, D_MODEL), row),
            _mod_spec(mod, ntok, MIX_TILE),
            pl.BlockSpec((MIX_TILE, CONV_W), row),
            pl.BlockSpec((MIX_TILE, ATTN_W), row),
            pl.BlockSpec((MIX_TILE, POOL_W), row),
            pl.BlockSpec((MIX_TILE, N_BRANCH * D_MODEL), row),
            HBM_SPEC,
            _resident((1, POOL_W)),
            HBM_SPEC,
            HBM_SPEC,
        ],
        out_specs=pl.BlockSpec((MIX_TILE, D_MODEL), row),
        out_shape=jax.ShapeDtypeStruct((ntok, D_MODEL), F32),
        scratch_shapes=[
            pltpu.VMEM(pool_shape, BF16),
            pltpu.VMEM((N_BRANCH, CONV_W, D_MODEL), BF16),
            pltpu.VMEM((D_MODEL, D_MODEL), BF16),
            pltpu.VMEM((2, CONV_W, D_MODEL), F32),
            pltpu.SemaphoreType.DMA((2,)),
            pltpu.VMEM(pool_shape, F32),
            pltpu.SemaphoreType.DMA,
        ],
        compiler_params=_params("arbitrary"),
        name="mix",
    )(x, mod, yconv, yattn, d, gates, pool_w, pool_scale, w_branch, w_out)


def _ffn_kernel(x_ref, mod_ref, g2_ref, wu_hbm, fc_ref, wd_hbm, gf_ref, o_ref,
                h_scr, act_scr, wu_bf, wd_bf, up_stage, up_sem, down_stage, down_sem,
                *, seq_len, final, layer):
    def cols(c, half):
        return pl.ds(half * D_FF + c * FF_CHUNK, FF_CHUNK)

    up_stream = _CastStream(
        [(wu_hbm.at[layer, :, cols(c, half)], wu_bf.at[:, cols(c, half)])
         for c in range(N_FF_CHUNKS) for half in range(2)], up_stage, up_sem)
    down_stream = _CastStream(
        [(wd_hbm.at[layer, pl.ds(c * FF_CHUNK, FF_CHUNK)], wd_bf.at[pl.ds(c * FF_CHUNK, FF_CHUNK)])
         for c in range(N_FF_CHUNKS)], down_stage, down_sem)

    def body(first):
        if first:
            up_stream.prime()
            down_stream.prime()
        h = _norm_mod(x_ref[...], g2_ref[...], mod_ref[0, 3:4, :], mod_ref[0, 4:5, :])
        h_scr[...] = h.astype(BF16)
        row = _row_in_seq((x_ref.shape[0], FF_CHUNK), seq_len)
        for c in range(N_FF_CHUNKS):
            sl = slice(c * FF_CHUNK, (c + 1) * FF_CHUNK)
            if first:
                up_stream.ready(2 * c)
                up_stream.ready(2 * c + 1)
            u = jnp.dot(h_scr[...], wu_bf[:, cols(c, 0)], preferred_element_type=F32)
            val = jnp.dot(h_scr[...], wu_bf[:, cols(c, 1)], preferred_element_type=F32)
            u = _dwconv3(u, fc_ref.at[:, sl], row, seq_len)
            act_scr[:, sl] = (jax.nn.gelu(u, approximate=True) * val).astype(BF16)
            if first:
                down_stream.ready(c)
        y = jnp.dot(act_scr[...], wd_bf[...], preferred_element_type=F32)
        xn = x_ref[...] + mod_ref[0, 5:6, :] * y
        if final:
            ms = jnp.mean(xn * xn, axis=-1, keepdims=True)
            xn = xn * lax.rsqrt(ms + EPS) * gf_ref[...]
        o_ref[...] = xn

    _first_tile_variants(pl.program_id(0), body)


def _ffn(x, mod, g2, w_up, f_conv, w_down, g_final, *, seq_len, final, layer):
    ntok = x.shape[0]
    return pl.pallas_call(
        functools.partial(_ffn_kernel, seq_len=seq_len, final=final, layer=layer),
        grid=(ntok // TOK_TILE,),
        in_specs=[
            pl.BlockSpec((TOK_TILE, D_MODEL), lambda i: (i, 0)),
            _mod_spec(mod, ntok, TOK_TILE),
            _resident((1, D_MODEL)),
            HBM_SPEC,
            _resident((3, D_FF)),
            HBM_SPEC,
            _resident((1, D_MODEL)),
        ],
        out_specs=pl.BlockSpec((TOK_TILE, D_MODEL), lambda i: (i, 0)),
        out_shape=jax.ShapeDtypeStruct((ntok, D_MODEL), F32),
        scratch_shapes=[
            pltpu.VMEM((TOK_TILE, D_MODEL), BF16),
            pltpu.VMEM((TOK_TILE, D_FF), BF16),
            pltpu.VMEM((D_MODEL, 2 * D_FF), BF16),
            pltpu.VMEM((D_FF, D_MODEL), BF16),
            pltpu.VMEM((2, D_MODEL, FF_CHUNK), F32),
            pltpu.SemaphoreType.DMA((2,)),
            pltpu.VMEM((2, FF_CHUNK, D_MODEL), F32),
            pltpu.SemaphoreType.DMA((2,)),
        ],
        compiler_params=_params("arbitrary"),
        name="ffn",
    )(x, mod, g2, w_up, f_conv, w_down, g_final)


def kernel(x_prompt, x_sample, cache_kv, c, c_ctx, w_mod, b_mod, g_norm1, g_norm2, w_in, conv_w, rpb,
           pool_w, pool_scale, w_branch, w_out, ffn_w_up, ffn_conv, ffn_w_down, g_final):
    n_mod = 8
    cvec = jnp.concatenate([c_ctx[None, :], c, jnp.zeros((n_mod - 1 - N_LAT_SEQ, D_MODEL), F32)], axis=0)
    mod = _adaln(cvec, w_mod, b_mod).reshape(DEPTH, n_mod, 6, D_MODEL)

    xp = x_prompt.reshape(N_CTX_SEQ * CTX_LEN, D_MODEL)
    xs = x_sample.reshape(N_LAT_SEQ * LAT_LEN, D_MODEL)
    gf = g_final.reshape(1, D_MODEL)
    assert DEPTH == 2
    kv = None
    for l in range(DEPTH):
        pool_s_l = pool_scale[l].reshape(1, POOL_W)
        g1 = g_norm1[l].reshape(1, D_MODEL)
        g2 = g_norm2[l].reshape(1, D_MODEL)
        mod_ctx = mod[l, 0:1]
        mod_lat = mod[l, 1:1 + N_LAT_SEQ]
        final = l == DEPTH - 1

        yc, qkv, d, gates, kv = _inproj(xp, mod_ctx, g1, w_in, conv_w[l], kv, seq_len=CTX_LEN,
                                        kv_mode="new" if l == 0 else "stack", tile=CTX_IN_TILE, layer=l)
        ya = _ctx_attention(qkv)
        xp = _mix(xp, mod_ctx, yc, ya, d, gates, pool_w, pool_s_l, w_branch, w_out, layer=l)
        xp = _ffn(xp, mod_ctx, g2, ffn_w_up, ffn_conv[l], ffn_w_down, gf, seq_len=CTX_LEN, final=final, layer=l)

        yc, qkv, d, gates = _inproj(xs, mod_lat, g1, w_in, conv_w[l], seq_len=LAT_LEN, kv_mode=None,
                                    tile=TOK_TILE, layer=l)
        ya = _lat_attention(qkv, cache_kv, l, _row_pair_table(rpb[l]))
        xs = _mix(xs, mod_lat, yc, ya, d, gates, pool_w, pool_s_l, w_branch, w_out, layer=l)
        xs = _ffn(xs, mod_lat, g2, ffn_w_up, ffn_conv[l], ffn_w_down, gf, seq_len=LAT_LEN, final=final, layer=l)

    y_prompt = xp.reshape(N_CTX_SEQ, CTX_LEN, D_MODEL)
    y_sample = xs.reshape(N_LAT_SEQ, LAT_LEN, D_MODEL)
    return (y_prompt, y_sample, kv)
```

```python
import functools

import jax
import jax.numpy as jnp
from jax import lax
from jax.experimental import pallas as pl
from jax.experimental.pallas import tpu as pltpu

F32 = jnp.float32
BF16 = jnp.bfloat16

D_MODEL = 1024
N_CTX_SEQ = 16
CTX_LEN = 256
DEPTH = 2
N_LAT_SEQ = 4
LAT_LEN = 1024
PAST_LEN = 256
GRID_W = 64
GRID_ROWS = LAT_LEN // GRID_W
CONV_W = 512
N_HEADS = 8
HEAD_DIM = 64
ATTN_W = N_HEADS * HEAD_DIM
POOL_W = 512
POOL_SIZES = (2, 4, 8, 16)
POOL_GROUP = POOL_W // len(POOL_SIZES)
N_BRANCH = 3
WIN_H = 8
WIN_W = 16
D_FF = 2816
EPS = 1e-6
IN_W = 3 * CONV_W + 3 * ATTN_W + POOL_W + N_BRANCH * D_MODEL

LANES = 128
TOK_TILE = 1024
CTX_IN_TILE = 512
IN_CHUNK = 512
FF_CHUNK = 256
N_FF_CHUNKS = D_FF // FF_CHUNK
MIX_TILE = 512
VMEM_LIMIT = 56 * 1024 * 1024


def _params(*sem):
    return pltpu.CompilerParams(dimension_semantics=sem, vmem_limit_bytes=VMEM_LIMIT)


def _resident(shape):
    return pl.BlockSpec(shape, lambda *_: (0,) * len(shape), pipeline_mode=pl.Buffered(1))


def _mod_spec(mod, ntok, tile):
    tiles_per_mod = ntok // mod.shape[0] // tile
    return pl.BlockSpec((1, 6, D_MODEL), lambda i, *_: (i // tiles_per_mod, 0, 0))


def _norm_mod(x, g, shift, scale):
    ms = jnp.mean(x * x, axis=-1, keepdims=True)
    return (x * lax.rsqrt(ms + EPS) * g) * (1.0 + scale) + shift


def _row_in_seq(shape, seq_len):
    return lax.broadcasted_iota(jnp.int32, shape, 0) & (seq_len - 1)


def _shift_rows(x, s, row, seq_len):
    y = pltpu.roll(x, s % x.shape[0], 0)
    ok = (row >= s) if s > 0 else (row < seq_len + s)
    return jnp.where(ok, y, 0.0)


def _dwconv3(x, w_ref, row, seq_len):
    return (_shift_rows(x, 1, row, seq_len) * w_ref[0:1, :] + x * w_ref[1:2, :]
            + _shift_rows(x, -1, row, seq_len) * w_ref[2:3, :])


def _pool_delta(p, seq_len):
    row = _row_in_seq((p.shape[0], POOL_GROUP), seq_len)
    outs = []
    for gi, w in enumerate(POOL_SIZES):
        pg = p[:, gi * POOL_GROUP:(gi + 1) * POOL_GROUP]
        fwd = bwd = pg
        k = 1
        while k < w // 2:
            fwd = fwd + _shift_rows(fwd, -k, row, seq_len)
            bwd = bwd + _shift_rows(bwd, k, row, seq_len)
            k *= 2
        acc = fwd + _shift_rows(bwd, 1, row, seq_len)
        lo = jnp.maximum(row - w // 2, 0)
        hi = jnp.minimum(row - w // 2 + w, seq_len)
        outs.append(acc / (hi - lo).astype(F32) - pg)
    return jnp.concatenate(outs, axis=-1)


def _adaln_kernel(c_ref, w_ref, b_ref, o_ref):
    c = c_ref[...]
    s = c * jax.nn.sigmoid(c)
    o_ref[0] = jnp.dot(s.astype(BF16), w_ref[0].astype(BF16), preferred_element_type=F32) + b_ref[0]


def _adaln(cvec, w_mod, b_mod):
    nb = cvec.shape[0]
    n_out = 6 * D_MODEL
    blk = 1024
    return pl.pallas_call(
        _adaln_kernel,
        grid=(DEPTH, n_out // blk),
        in_specs=[
            pl.BlockSpec((nb, D_MODEL), lambda l, n: (0, 0)),
            pl.BlockSpec((1, D_MODEL, blk), lambda l, n: (l, 0, n)),
            pl.BlockSpec((1, 1, blk), lambda l, n: (l, 0, n)),
        ],
        out_specs=pl.BlockSpec((1, nb, blk), lambda l, n: (l, 0, n)),
        out_shape=jax.ShapeDtypeStruct((DEPTH, nb, n_out), F32),
        compiler_params=_params("arbitrary", "arbitrary"),
        name="adaln",
    )(cvec, w_mod, b_mod.reshape(DEPTH, 1, n_out))


COL_CONV = 0
COL_QKV = 3 * CONV_W
COL_POOL = COL_QKV + 3 * ATTN_W
COL_GATE = COL_POOL + POOL_W
N_GATE_COLS = N_BRANCH * D_MODEL


def _inproj_kernel(*refs, seq_len, kv_mode, gate_phases):
    x_ref, mod_ref, g_ref, w_ref, cw_ref = refs[:5]
    refs = refs[5:]
    if kv_mode == "stack":
        kv_prev_ref, refs = refs[0], refs[1:]
    yconv_ref, qkv_ref, d_ref, gates_ref = refs[:4]
    refs = refs[4:]
    if kv_mode is not None:
        kv_ref, refs = refs[0], refs[1:]
    h_scr = refs[0]
    kv_dst = None
    if kv_mode == "new":
        kv_dst = kv_ref
    elif kv_mode == "stack":
        kv_dst = refs[1]
    j = pl.program_id(1)
    gate_block = N_GATE_COLS // gate_phases

    def proj(col):
        return jnp.dot(h_scr[...], w_ref[:, col:col + IN_CHUNK], preferred_element_type=F32)

    def kv_step(step):
        if kv_mode == "stack":
            if step % DEPTH == 0:
                kv_ref[...] = kv_prev_ref[...]
            else:
                kv_ref[0] = kv_dst[step // DEPTH]

    @pl.when(j == 0)
    def _():
        h = _norm_mod(x_ref[...], g_ref[...], mod_ref[0, 0:1, :], mod_ref[0, 1:2, :])
        h_scr[...] = h.astype(BF16)
        row = _row_in_seq((x_ref.shape[0], IN_CHUNK), seq_len)
        b_gate = proj(COL_CONV)
        conv = _dwconv3(proj(COL_CONV + CONV_W) * proj(COL_CONV + 2 * CONV_W), cw_ref, row, seq_len)
        yconv_ref[...] = (b_gate * conv).astype(BF16)
        qkv_ref[:, 0:ATTN_W] = (proj(COL_QKV) * (HEAD_DIM ** -0.5)).astype(BF16)
        for kv in range(2):
            z = proj(COL_QKV + (1 + kv) * ATTN_W)
            qkv_ref[:, (1 + kv) * ATTN_W:(2 + kv) * ATTN_W] = z.astype(BF16)
            if kv_dst is not None:
                for s in range(z.shape[0] // seq_len):
                    for h_i in range(N_HEADS):
                        kv_dst[s, kv, h_i] = z[s * seq_len:(s + 1) * seq_len,
                                               h_i * HEAD_DIM:(h_i + 1) * HEAD_DIM]
        d_ref[...] = _pool_delta(proj(COL_POOL), seq_len).astype(BF16)
        kv_step(0)

    for phase in range(gate_phases):
        @pl.when(j == 1 + phase)
        def _():
            for c in range(gate_block // IN_CHUNK):
                z = proj(COL_GATE + phase * gate_block + c * IN_CHUNK)
                gates_ref[:, c * IN_CHUNK:(c + 1) * IN_CHUNK] = jax.nn.sigmoid(z).astype(BF16)
            kv_step(1 + phase)


def _inproj(x, mod, g1, w_in, conv_w, kv_prev=None, *, seq_len, tile, kv_mode):
    ntok = x.shape[0]
    first = lambda i, j: (i, 0)
    assert (kv_prev is not None) == (kv_mode == "stack")
    spt = tile // seq_len
    gate_phases = spt * DEPTH - 1 if kv_mode == "stack" else 2
    gate_block = N_GATE_COLS // gate_phases
    out_shape = [
        jax.ShapeDtypeStruct((ntok, CONV_W), BF16),
        jax.ShapeDtypeStruct((ntok, 3 * ATTN_W), BF16),
        jax.ShapeDtypeStruct((ntok, POOL_W), BF16),
        jax.ShapeDtypeStruct((ntok, N_GATE_COLS), BF16),
    ]
    out_specs = [
        pl.BlockSpec((tile, CONV_W), first),
        pl.BlockSpec((tile, 3 * ATTN_W), first),
        pl.BlockSpec((tile, POOL_W), first),
        pl.BlockSpec((tile, gate_block), lambda i, j: (i, jnp.maximum(j - 1, 0))),
    ]
    in_specs = [
        pl.BlockSpec((tile, D_MODEL), first),
        _mod_spec(mod, ntok, tile),
        _resident((1, D_MODEL)),
        _resident((D_MODEL, IN_W)),
        _resident((3, CONV_W)),
    ]
    operands = [x, mod, g1, w_in, conv_w]
    scratch = [pltpu.VMEM((tile, D_MODEL), BF16)]
    nseq = ntok // seq_len
    one_layer = (2, N_HEADS, seq_len, HEAD_DIM)
    if kv_mode == "new":
        out_shape.append(jax.ShapeDtypeStruct((nseq,) + one_layer, F32))
        out_specs.append(pl.BlockSpec((spt,) + one_layer, lambda i, j: (i, 0, 0, 0, 0)))
    elif kv_mode == "stack":
        assert DEPTH == 2
        in_specs.append(pl.BlockSpec((1,) + one_layer, lambda i, j: (i * spt + j // DEPTH, 0, 0, 0, 0)))
        operands.append(kv_prev)
        out_shape.append(jax.ShapeDtypeStruct((nseq, DEPTH) + one_layer, F32))
        out_specs.append(pl.BlockSpec((1, None) + one_layer,
                                      lambda i, j: (i * spt + j // DEPTH, j % DEPTH, 0, 0, 0, 0)))
        scratch.append(pltpu.VMEM((spt,) + one_layer, F32))
    return pl.pallas_call(
        functools.partial(_inproj_kernel, seq_len=seq_len, kv_mode=kv_mode, gate_phases=gate_phases),
        grid=(ntok // tile, 1 + gate_phases),
        in_specs=in_specs,
        out_specs=out_specs,
        out_shape=out_shape,
        scratch_shapes=scratch,
        compiler_params=_params("arbitrary", "arbitrary"),
        name="inproj_lat" if kv_mode is None else "inproj_ctx",
    )(*operands)


def _attend_pair(q2, kvb):
    lane = lax.broadcasted_iota(jnp.int32, (1, LANES), 1)
    out = None
    for half in range(2):
        m = (lane >= half * HEAD_DIM) & (lane < (half + 1) * HEAD_DIM)
        qh = jnp.where(m, q2, jnp.zeros_like(q2))
        scores = []
        for k2, _, bias in kvb:
            s = lax.dot_general(qh, k2, (((1,), (1,)), ((), ())), preferred_element_type=F32)
            if bias is not None:
                s = s + bias[half]
            scores.append(s)
        mx = functools.reduce(jnp.maximum, [jnp.max(s, axis=-1, keepdims=True) for s in scores])
        es = [jnp.exp(s - mx) for s in scores]
        den = functools.reduce(lambda a, b: a + b, [jnp.sum(e, axis=-1, keepdims=True) for e in es])
        acc = None
        for e, (_, v2, _) in zip(es, kvb):
            vh = jnp.where(m, v2, jnp.zeros_like(v2))
            o = jnp.dot(e.astype(BF16), vh, preferred_element_type=F32)
            acc = o if acc is None else acc + o
        acc = acc * (1.0 / den)
        out = acc if out is None else out + acc
    return out


def _ctx_attn_kernel(q_ref, k_ref, v_ref, o_ref):
    for hp in range(N_HEADS // 2):
        sl = slice(hp * LANES, (hp + 1) * LANES)
        o_ref[:, sl] = _attend_pair(q_ref[:, sl], [(k_ref[:, sl], v_ref[:, sl], None)]).astype(BF16)


def _ctx_attention(qkv):
    ntok = qkv.shape[0]
    return pl.pallas_call(
        _ctx_attn_kernel,
        grid=(ntok // CTX_LEN,),
        in_specs=[
            pl.BlockSpec((CTX_LEN, ATTN_W), lambda b: (b, 0)),
            pl.BlockSpec((CTX_LEN, ATTN_W), lambda b: (b, 1)),
            pl.BlockSpec((CTX_LEN, ATTN_W), lambda b: (b, 2)),
        ],
        out_specs=pl.BlockSpec((CTX_LEN, ATTN_W), lambda b: (b, 0)),
        out_shape=jax.ShapeDtypeStruct((ntok, ATTN_W), BF16),
        compiler_params=_params("arbitrary"),
        name="attn_ctx",
    )(qkv, qkv, qkv)


Q_ROWS = 4
Q_BLOCK = Q_ROWS * GRID_W
N_Q_BLOCKS = GRID_ROWS // Q_ROWS
KEY_ROWS = 12
N_KEYS = KEY_ROWS * GRID_W
N_REL_ROWS = 2 * WIN_H - 1
N_ROW_PAIRS = N_REL_ROWS + 1


def _first_key_row(m):
    return min(max(Q_ROWS * m - WIN_H // 2, 0), GRID_ROWS - KEY_ROWS)


def _fill_bias(rp_ref, bias_scr):
    shape = (N_ROW_PAIRS * GRID_W, LANES)
    qc = lax.broadcasted_iota(jnp.int32, shape, 0) & (GRID_W - 1)
    lane = lax.broadcasted_iota(jnp.int32, shape, 1)
    kc = lane & (GRID_W - 1)
    col_start = jnp.clip(qc - WIN_W // 2, 0, GRID_W - WIN_W)
    valid = (kc >= col_start) & (kc < col_start + WIN_W)
    first_half = lax.broadcasted_iota(jnp.int32, (GRID_W, LANES), 1) < GRID_W
    for h in range(N_HEADS):
        t = jnp.broadcast_to(rp_ref[h][:, None, :], (N_ROW_PAIRS, GRID_W, LANES)).reshape(shape)
        t = pltpu.roll(t, LANES - (WIN_W - 1), 1)
        for bit in range(6):
            t = jnp.where(((qc >> bit) & 1) == 1, pltpu.roll(t, 1 << bit, 1), t)
        t = jnp.where(valid, t, -jnp.inf)
        for m in range(N_Q_BLOCKS):
            for ql in range(Q_ROWS):
                qr = Q_ROWS * m + ql
                win0 = min(max(qr - WIN_H // 2, 0), GRID_ROWS - WIN_H)
                for u in range(KEY_ROWS // 2):
                    kr = _first_key_row(m) + 2 * u
                    in0 = win0 <= kr < win0 + WIN_H
                    in1 = win0 <= kr + 1 < win0 + WIN_H
                    if in0 or in1:
                        pair = kr - qr + WIN_H
                        tile = t[pair * GRID_W:(pair + 1) * GRID_W]
                        if not in0:
                            tile = jnp.where(first_half, -jnp.inf, tile)
                        if not in1:
                            tile = jnp.where(first_half, tile, -jnp.inf)
                    else:
                        tile = jnp.full((GRID_W, LANES), -jnp.inf, F32)
                    bias_scr[h, m, ql * GRID_W:(ql + 1) * GRID_W, u * LANES:(u + 1) * LANES] = tile


def _lat_attn_kernel(q_ref, k_ref, v_ref, cache_ref, rp_ref, o_ref, bias_scr):
    hp = pl.program_id(0)
    m = pl.program_id(1)

    @pl.when((hp == 0) & (m == 0))
    def _():
        _fill_bias(rp_ref, bias_scr)

    row0 = jnp.clip(Q_ROWS * m - WIN_H // 2, 0, GRID_ROWS - KEY_ROWS)
    start = pl.multiple_of(row0 * GRID_W, Q_BLOCK)
    for s in range(q_ref.shape[0]):
        kc, vc = (jnp.concatenate([cache_ref[s, i, 0], cache_ref[s, i, 1]], axis=-1).astype(BF16)
                  for i in range(2))
        kvb = [
            (k_ref[s, pl.ds(start, N_KEYS), :], v_ref[s, pl.ds(start, N_KEYS), :],
             (bias_scr[2 * hp, m], bias_scr[2 * hp + 1, m])),
            (kc, vc, None),
        ]
        o_ref[s] = _attend_pair(q_ref[s], kvb).astype(BF16)


def _lat_attention(qkv, cache_kv, layer, rp):
    nseq = qkv.shape[0] // LAT_LEN
    qkv = qkv.reshape(nseq, LAT_LEN, 3 * ATTN_W)
    n_hp = ATTN_W // LANES
    out = pl.pallas_call(
        _lat_attn_kernel,
        grid=(n_hp, N_Q_BLOCKS),
        in_specs=[
            pl.BlockSpec((nseq, Q_BLOCK, LANES), lambda hp, m: (0, m, hp)),
            pl.BlockSpec((nseq, LAT_LEN, LANES), lambda hp, m: (0, 0, n_hp + hp)),
            pl.BlockSpec((nseq, LAT_LEN, LANES), lambda hp, m: (0, 0, 2 * n_hp + hp)),
            pl.BlockSpec((nseq, None, 2, 2, PAST_LEN, HEAD_DIM), lambda hp, m: (0, layer, 0, hp, 0, 0)),
            _resident(rp.shape),
        ],
        out_specs=pl.BlockSpec((nseq, Q_BLOCK, LANES), lambda hp, m: (0, m, hp)),
        out_shape=jax.ShapeDtypeStruct((nseq, LAT_LEN, ATTN_W), BF16),
        scratch_shapes=[pltpu.VMEM((N_HEADS, N_Q_BLOCKS, Q_BLOCK, N_KEYS), F32)],
        compiler_params=_params("arbitrary", "arbitrary"),
        name="attn_lat",
    )(qkv, qkv, qkv, cache_kv, rp)
    return out.reshape(nseq * LAT_LEN, ATTN_W)


def _row_pair_table(rpb_l):
    lane_pad = GRID_W - rpb_l.shape[-1]
    lo = jnp.pad(rpb_l, ((0, 0), (1, 0), (0, lane_pad)))
    hi = jnp.pad(rpb_l, ((0, 0), (0, 1), (0, lane_pad)))
    return jnp.concatenate([lo, hi], axis=-1)


def _mix_kernel(x_ref, mod_ref, yc_ref, ya_ref, d_ref, gates_ref, pw_ref, ps_ref, wb_ref, wo_ref, o_ref):
    yp = [jnp.dot(d_ref[:, g * POOL_GROUP:(g + 1) * POOL_GROUP], pw_ref[g], preferred_element_type=F32)
          for g in range(len(POOL_SIZES))]
    y_pool = (jnp.concatenate(yp, axis=-1) * ps_ref[...]).astype(BF16)
    merged = None
    for i, y in enumerate((yc_ref[...], ya_ref[...], y_pool)):
        proj = jnp.dot(y, wb_ref[i], preferred_element_type=F32)
        term = gates_ref[:, i * D_MODEL:(i + 1) * D_MODEL].astype(F32) * proj
        merged = term if merged is None else merged + term
    mix = jnp.dot(merged.astype(BF16), wo_ref[...], preferred_element_type=F32)
    o_ref[...] = x_ref[...] + mod_ref[0, 2:3, :] * mix


def _mix(x, mod, yconv, yattn, d, gates, pool_w, pool_scale, w_branch, w_out):
    ntok = x.shape[0]
    row = lambda i: (i, 0)
    return pl.pallas_call(
        _mix_kernel,
        grid=(ntok // MIX_TILE,),
        in_specs=[
            pl.BlockSpec((MIX_TILE, D_MODEL), row),
            _mod_spec(mod, ntok, MIX_TILE),
            pl.BlockSpec((MIX_TILE, CONV_W), row),
            pl.BlockSpec((MIX_TILE, ATTN_W), row),
            pl.BlockSpec((MIX_TILE, POOL_W), row),
            pl.BlockSpec((MIX_TILE, N_BRANCH * D_MODEL), row),
            _resident((len(POOL_SIZES), POOL_GROUP, POOL_GROUP)),
            _resident((1, POOL_W)),
            _resident((N_BRANCH, CONV_W, D_MODEL)),
            _resident((D_MODEL, D_MODEL)),
        ],
        out_specs=pl.BlockSpec((MIX_TILE, D_MODEL), row),
        out_shape=jax.ShapeDtypeStruct((ntok, D_MODEL), F32),
        compiler_params=_params("arbitrary"),
        name="mix",
    )(x, mod, yconv, yattn, d, gates, pool_w, pool_scale, w_branch, w_out)


def _ffn_kernel(x_ref, mod_ref, g2_ref, wu_ref, fc_ref, wd_ref, gf_ref, o_ref, h_scr, act_scr,
                *, seq_len, final):
    h = _norm_mod(x_ref[...], g2_ref[...], mod_ref[0, 3:4, :], mod_ref[0, 4:5, :])
    h_scr[...] = h.astype(BF16)
    row = _row_in_seq((x_ref.shape[0], FF_CHUNK), seq_len)
    for c in range(N_FF_CHUNKS):
        sl = slice(c * FF_CHUNK, (c + 1) * FF_CHUNK)
        sl_val = slice(D_FF + c * FF_CHUNK, D_FF + (c + 1) * FF_CHUNK)
        u = jnp.dot(h_scr[...], wu_ref[:, sl], preferred_element_type=F32)
        val = jnp.dot(h_scr[...], wu_ref[:, sl_val], preferred_element_type=F32)
        u = _dwconv3(u, fc_ref.at[:, sl], row, seq_len)
        act_scr[:, sl] = (jax.nn.gelu(u, approximate=True) * val).astype(BF16)
    y = jnp.dot(act_scr[...], wd_ref[...], preferred_element_type=F32)
    xn = x_ref[...] + mod_ref[0, 5:6, :] * y
    if final:
        ms = jnp.mean(xn * xn, axis=-1, keepdims=True)
        xn = xn * lax.rsqrt(ms + EPS) * gf_ref[...]
    o_ref[...] = xn


def _ffn(x, mod, g2, w_up, f_conv, w_down, g_final, *, seq_len, final):
    ntok = x.shape[0]
    return pl.pallas_call(
        functools.partial(_ffn_kernel, seq_len=seq_len, final=final),
        grid=(ntok // TOK_TILE,),
        in_specs=[
            pl.BlockSpec((TOK_TILE, D_MODEL), lambda i: (i, 0)),
            _mod_spec(mod, ntok, TOK_TILE),
            _resident((1, D_MODEL)),
            _resident((D_MODEL, 2 * D_FF)),
            _resident((3, D_FF)),
            _resident((D_FF, D_MODEL)),
            _resident((1, D_MODEL)),
        ],
        out_specs=pl.BlockSpec((TOK_TILE, D_MODEL), lambda i: (i, 0)),
        out_shape=jax.ShapeDtypeStruct((ntok, D_MODEL), F32),
        scratch_shapes=[pltpu.VMEM((TOK_TILE, D_MODEL), BF16), pltpu.VMEM((TOK_TILE, D_FF), BF16)],
        compiler_params=_params("arbitrary"),
        name="ffn",
    )(x, mod, g2, w_up, f_conv, w_down, g_final)


def kernel(x_prompt, x_sample, cache_kv, c, c_ctx, w_mod, b_mod, g_norm1, g_norm2, w_in, conv_w, rpb,
           pool_w, pool_scale, w_branch, w_out, ffn_w_up, ffn_conv, ffn_w_down, g_final):
    n_mod = 8
    cvec = jnp.concatenate([c_ctx[None, :], c, jnp.zeros((n_mod - 1 - N_LAT_SEQ, D_MODEL), F32)], axis=0)
    mod = _adaln(cvec, w_mod, b_mod).reshape(DEPTH, n_mod, 6, D_MODEL)

    xp = x_prompt.reshape(N_CTX_SEQ * CTX_LEN, D_MODEL)
    xs = x_sample.reshape(N_LAT_SEQ * LAT_LEN, D_MODEL)
    gf = g_final.reshape(1, D_MODEL)
    kv_state = None
    for l in range(DEPTH):
        w_in_l = w_in[l].astype(BF16)
        pool_w_l = pool_w[l].astype(BF16)
        pool_s_l = pool_scale[l].reshape(1, POOL_W)
        w_br_l = w_branch[l].astype(BF16)
        w_out_l = w_out[l].astype(BF16)
        w_up_l = ffn_w_up[l].astype(BF16)
        w_dn_l = ffn_w_down[l].astype(BF16)
        g1 = g_norm1[l].reshape(1, D_MODEL)
        g2 = g_norm2[l].reshape(1, D_MODEL)
        mod_ctx = mod[l, 0:1]
        mod_lat = mod[l, 1:1 + N_LAT_SEQ]
        final = l == DEPTH - 1

        yc, qkv, d, gates, kv_state = _inproj(xp, mod_ctx, g1, w_in_l, conv_w[l], kv_state, seq_len=CTX_LEN,
                                              tile=CTX_IN_TILE, kv_mode="new" if l == 0 else "stack")
        ya = _ctx_attention(qkv)
        xp = _mix(xp, mod_ctx, yc, ya, d, gates, pool_w_l, pool_s_l, w_br_l, w_out_l)
        xp = _ffn(xp, mod_ctx, g2, w_up_l, ffn_conv[l], w_dn_l, gf, seq_len=CTX_LEN, final=final)

        yc, qkv, d, gates = _inproj(xs, mod_lat, g1, w_in_l, conv_w[l], seq_len=LAT_LEN, tile=TOK_TILE,
                                    kv_mode=None)
        ya = _lat_attention(qkv, cache_kv, l, _row_pair_table(rpb[l]))
        xs = _mix(xs, mod_lat, yc, ya, d, gates, pool_w_l, pool_s_l, w_br_l, w_out_l)
        xs = _ffn(xs, mod_lat, g2, w_up_l, ffn_conv[l], w_dn_l, gf, seq_len=LAT_LEN, final=final)

    y_prompt = xp.reshape(N_CTX_SEQ, CTX_LEN, D_MODEL)
    y_sample = xs.reshape(N_LAT_SEQ, LAT_LEN, D_MODEL)
    return (y_prompt, y_sample, kv_state)
```

```python
import functools

import jax
import jax.numpy as jnp
from jax import lax
from jax.experimental import pallas as pl
from jax.experimental.pallas import tpu as pltpu

F32 = jnp.float32
BF16 = jnp.bfloat16

D_MODEL = 1024
N_CTX_SEQ = 16
CTX_LEN = 256
DEPTH = 2
N_LAT_SEQ = 4
LAT_LEN = 1024
PAST_LEN = 256
GRID_W = 64
GRID_ROWS = LAT_LEN // GRID_W
CONV_W = 512
N_HEADS = 8
HEAD_DIM = 64
ATTN_W = N_HEADS * HEAD_DIM
POOL_W = 512
POOL_SIZES = (2, 4, 8, 16)
POOL_GROUP = POOL_W // len(POOL_SIZES)
N_BRANCH = 3
WIN_H = 8
WIN_W = 16
D_FF = 2816
EPS = 1e-6
IN_W = 3 * CONV_W + 3 * ATTN_W + POOL_W + N_BRANCH * D_MODEL

LANES = 128
TOK_TILE = 1024
CTX_IN_TILE = 512
IN_CHUNK = 512
FF_CHUNK = 256
N_FF_CHUNKS = D_FF // FF_CHUNK
MIX_TILE = 512
VMEM_LIMIT = 56 * 1024 * 1024


def _params(*sem):
    return pltpu.CompilerParams(dimension_semantics=sem, vmem_limit_bytes=VMEM_LIMIT)


def _resident(shape):
    return pl.BlockSpec(shape, lambda *_: (0,) * len(shape), pipeline_mode=pl.Buffered(1))


def _resident_layer(shape, layer):
    return pl.BlockSpec((None,) + shape, lambda *_: (layer,) + (0,) * len(shape), pipeline_mode=pl.Buffered(1))


def _mod_spec(mod, ntok, tile):
    tiles_per_mod = ntok // mod.shape[0] // tile
    return pl.BlockSpec((1, 6, D_MODEL), lambda i, *_: (i // tiles_per_mod, 0, 0))


def _norm_mod(x, g, shift, scale):
    ms = jnp.mean(x * x, axis=-1, keepdims=True)
    return (x * lax.rsqrt(ms + EPS) * g) * (1.0 + scale) + shift


def _row_in_seq(shape, seq_len):
    return lax.broadcasted_iota(jnp.int32, shape, 0) & (seq_len - 1)


def _shift_rows(x, s, row, seq_len):
    y = pltpu.roll(x, s % x.shape[0], 0)
    ok = (row >= s) if s > 0 else (row < seq_len + s)
    return jnp.where(ok, y, 0.0)


def _dwconv3(x, w_ref, row, seq_len):
    return (_shift_rows(x, 1, row, seq_len) * w_ref[0:1, :] + x * w_ref[1:2, :]
            + _shift_rows(x, -1, row, seq_len) * w_ref[2:3, :])


def _pool_delta(p, seq_len):
    row = _row_in_seq((p.shape[0], POOL_GROUP), seq_len)
    outs = []
    for gi, w in enumerate(POOL_SIZES):
        pg = p[:, gi * POOL_GROUP:(gi + 1) * POOL_GROUP]
        fwd = bwd = pg
        k = 1
        while k < w // 2:
            fwd = fwd + _shift_rows(fwd, -k, row, seq_len)
            bwd = bwd + _shift_rows(bwd, k, row, seq_len)
            k *= 2
        acc = fwd + _shift_rows(bwd, 1, row, seq_len)
        lo = jnp.maximum(row - w // 2, 0)
        hi = jnp.minimum(row - w // 2 + w, seq_len)
        outs.append(acc / (hi - lo).astype(F32) - pg)
    return jnp.concatenate(outs, axis=-1)


CAST_STEPS = 16


def _cast_kernel(*refs):
    n = len(refs) // 2
    for src, dst in zip(refs[:n], refs[n:]):
        dst[...] = src[...].astype(BF16)


def _cast_weights(*weights):
    flat = [w.reshape(-1, w.shape[-1]) for w in weights]
    specs = [pl.BlockSpec((w.shape[0] // CAST_STEPS, w.shape[1]), lambda i: (i, 0)) for w in flat]
    outs = pl.pallas_call(
        _cast_kernel,
        grid=(CAST_STEPS,),
        in_specs=specs,
        out_specs=specs,
        out_shape=[jax.ShapeDtypeStruct(w.shape, BF16) for w in flat],
        compiler_params=_params("arbitrary"),
        name="cast_weights",
    )(*flat)
    return [o.reshape(w.shape) for o, w in zip(outs, weights)]


def _adaln_kernel(c_ref, w_ref, b_ref, o_ref):
    c = c_ref[...]
    s = c * jax.nn.sigmoid(c)
    o_ref[0] = jnp.dot(s.astype(BF16), w_ref[0].astype(BF16), preferred_element_type=F32) + b_ref[0]


def _adaln(cvec, w_mod, b_mod):
    nb = cvec.shape[0]
    n_out = 6 * D_MODEL
    blk = 1024
    return pl.pallas_call(
        _adaln_kernel,
        grid=(DEPTH, n_out // blk),
        in_specs=[
            pl.BlockSpec((nb, D_MODEL), lambda l, n: (0, 0)),
            pl.BlockSpec((1, D_MODEL, blk), lambda l, n: (l, 0, n)),
            pl.BlockSpec((1, 1, blk), lambda l, n: (l, 0, n)),
        ],
        out_specs=pl.BlockSpec((1, nb, blk), lambda l, n: (l, 0, n)),
        out_shape=jax.ShapeDtypeStruct((DEPTH, nb, n_out), F32),
        compiler_params=_params("arbitrary", "arbitrary"),
        name="adaln",
    )(cvec, w_mod, b_mod.reshape(DEPTH, 1, n_out))


COL_CONV = 0
COL_QKV = 3 * CONV_W
COL_POOL = COL_QKV + 3 * ATTN_W
COL_GATE = COL_POOL + POOL_W
N_GATE_COLS = N_BRANCH * D_MODEL


def _inproj_kernel(*refs, seq_len, kv_mode, n_steps):
    x_ref, mod_ref, g_ref, w_ref, cw_ref = refs[:5]
    refs = refs[5:]
    if kv_mode == "stack":
        kv_prev_ref, refs = refs[0], refs[1:]
    yconv_ref, qkv_ref, d_ref, gates_ref = refs[:4]
    refs = refs[4:]
    if kv_mode is not None:
        kv_ref, refs = refs[0], refs[1:]
    h_scr = refs[0]
    kv_dst = None
    if kv_mode == "new":
        kv_dst = kv_ref
    elif kv_mode == "stack":
        kv_dst = refs[1]
    j = pl.program_id(1)
    n_gate_chunks = N_GATE_COLS // IN_CHUNK
    chunks_per_step = n_gate_chunks // max(n_steps - 1, 1)

    def proj(col):
        return jnp.dot(h_scr[...], w_ref[:, col:col + IN_CHUNK], preferred_element_type=F32)

    def kv_step(step):
        if kv_mode == "stack":
            if step % DEPTH == 0:
                kv_ref[...] = kv_prev_ref[...]
            else:
                kv_ref[0] = kv_dst[step // DEPTH]

    def gate_chunks(first_chunk):
        for c in range(chunks_per_step):
            z = proj(COL_GATE + (first_chunk + c) * IN_CHUNK)
            gates_ref[:, c * IN_CHUNK:(c + 1) * IN_CHUNK] = jax.nn.sigmoid(z).astype(BF16)

    def mixers():
        h = _norm_mod(x_ref[...], g_ref[...], mod_ref[0, 0:1, :], mod_ref[0, 1:2, :])
        h_scr[...] = h.astype(BF16)
        row = _row_in_seq((x_ref.shape[0], IN_CHUNK), seq_len)
        b_gate = proj(COL_CONV)
        conv = _dwconv3(proj(COL_CONV + CONV_W) * proj(COL_CONV + 2 * CONV_W), cw_ref, row, seq_len)
        yconv_ref[...] = (b_gate * conv).astype(BF16)
        qkv_ref[:, 0:ATTN_W] = (proj(COL_QKV) * (HEAD_DIM ** -0.5)).astype(BF16)
        for kv in range(2):
            z = proj(COL_QKV + (1 + kv) * ATTN_W)
            qkv_ref[:, (1 + kv) * ATTN_W:(2 + kv) * ATTN_W] = z.astype(BF16)
            if kv_dst is not None:
                for s in range(z.shape[0] // seq_len):
                    for h_i in range(N_HEADS):
                        kv_dst[s, kv, h_i] = z[s * seq_len:(s + 1) * seq_len,
                                               h_i * HEAD_DIM:(h_i + 1) * HEAD_DIM]
        d_ref[...] = _pool_delta(proj(COL_POOL), seq_len).astype(BF16)
        kv_step(0)

    if n_steps == 1:
        mixers()
        gate_chunks(0)
    else:
        pl.when(j == 0)(mixers)
        for step in range(1, n_steps):
            @pl.when(j == step)
            def _():
                gate_chunks((step - 1) * chunks_per_step)
                kv_step(step)


def _inproj(x, mod, g1, w_in, conv_w, kv_prev=None, *, seq_len, tile, kv_mode, layer):
    ntok = x.shape[0]
    first = lambda i, j: (i, 0)
    assert (kv_prev is not None) == (kv_mode == "stack")
    spt = tile // seq_len
    n_steps = spt * DEPTH if kv_mode == "stack" else 1
    gate_block = N_GATE_COLS // max(n_steps - 1, 1)
    out_shape = [
        jax.ShapeDtypeStruct((ntok, CONV_W), BF16),
        jax.ShapeDtypeStruct((ntok, 3 * ATTN_W), BF16),
        jax.ShapeDtypeStruct((ntok, POOL_W), BF16),
        jax.ShapeDtypeStruct((ntok, N_GATE_COLS), BF16),
    ]
    out_specs = [
        pl.BlockSpec((tile, CONV_W), first),
        pl.BlockSpec((tile, 3 * ATTN_W), first),
        pl.BlockSpec((tile, POOL_W), first),
        pl.BlockSpec((tile, gate_block), lambda i, j: (i, jnp.maximum(j - 1, 0))),
    ]
    in_specs = [
        pl.BlockSpec((tile, D_MODEL), first),
        _mod_spec(mod, ntok, tile),
        _resident((1, D_MODEL)),
        _resident_layer((D_MODEL, IN_W), layer),
        _resident((3, CONV_W)),
    ]
    operands = [x, mod, g1, w_in, conv_w]
    scratch = [pltpu.VMEM((tile, D_MODEL), BF16)]
    nseq = ntok // seq_len
    one_layer = (2, N_HEADS, seq_len, HEAD_DIM)
    if kv_mode == "new":
        out_shape.append(jax.ShapeDtypeStruct((nseq,) + one_layer, F32))
        out_specs.append(pl.BlockSpec((spt,) + one_layer, lambda i, j: (i, 0, 0, 0, 0)))
    elif kv_mode == "stack":
        assert DEPTH == 2
        in_specs.append(pl.BlockSpec((1,) + one_layer, lambda i, j: (i * spt + j // DEPTH, 0, 0, 0, 0)))
        operands.append(kv_prev)
        out_shape.append(jax.ShapeDtypeStruct((nseq, DEPTH) + one_layer, F32))
        out_specs.append(pl.BlockSpec((1, None) + one_layer,
                                      lambda i, j: (i * spt + j // DEPTH, j % DEPTH, 0, 0, 0, 0)))
        scratch.append(pltpu.VMEM((spt,) + one_layer, F32))
    return pl.pallas_call(
        functools.partial(_inproj_kernel, seq_len=seq_len, kv_mode=kv_mode, n_steps=n_steps),
        grid=(ntok // tile, n_steps),
        in_specs=in_specs,
        out_specs=out_specs,
        out_shape=out_shape,
        scratch_shapes=scratch,
        compiler_params=_params("arbitrary", "arbitrary"),
        name="inproj_lat" if kv_mode is None else "inproj_ctx",
    )(*operands)


def _attend_pair(q2, kvb):
    lane = lax.broadcasted_iota(jnp.int32, (1, LANES), 1)
    out = None
    for half in range(2):
        m = (lane >= half * HEAD_DIM) & (lane < (half + 1) * HEAD_DIM)
        qh = jnp.where(m, q2, jnp.zeros_like(q2))
        scores = []
        for k2, _, bias in kvb:
            s = lax.dot_general(qh, k2, (((1,), (1,)), ((), ())), preferred_element_type=F32)
            if bias is not None:
                s = s + bias[half]
            scores.append(s)
        mx = functools.reduce(jnp.maximum, [jnp.max(s, axis=-1, keepdims=True) for s in scores])
        es = [jnp.exp(s - mx) for s in scores]
        den = functools.reduce(lambda a, b: a + b, [jnp.sum(e, axis=-1, keepdims=True) for e in es])
        acc = None
        for e, (_, v2, _) in zip(es, kvb):
            vh = jnp.where(m, v2, jnp.zeros_like(v2))
            o = jnp.dot(e.astype(BF16), vh, preferred_element_type=F32)
            acc = o if acc is None else acc + o
        acc = acc * (1.0 / den)
        out = acc if out is None else out + acc
    return out


def _ctx_attn_kernel(q_ref, k_ref, v_ref, o_ref):
    for hp in range(N_HEADS // 2):
        sl = slice(hp * LANES, (hp + 1) * LANES)
        o_ref[:, sl] = _attend_pair(q_ref[:, sl], [(k_ref[:, sl], v_ref[:, sl], None)]).astype(BF16)


def _ctx_attention(qkv):
    ntok = qkv.shape[0]
    return pl.pallas_call(
        _ctx_attn_kernel,
        grid=(ntok // CTX_LEN,),
        in_specs=[
            pl.BlockSpec((CTX_LEN, ATTN_W), lambda b: (b, 0)),
            pl.BlockSpec((CTX_LEN, ATTN_W), lambda b: (b, 1)),
            pl.BlockSpec((CTX_LEN, ATTN_W), lambda b: (b, 2)),
        ],
        out_specs=pl.BlockSpec((CTX_LEN, ATTN_W), lambda b: (b, 0)),
        out_shape=jax.ShapeDtypeStruct((ntok, ATTN_W), BF16),
        compiler_params=_params("arbitrary"),
        name="attn_ctx",
    )(qkv, qkv, qkv)


Q_ROWS = 4
Q_BLOCK = Q_ROWS * GRID_W
N_Q_BLOCKS = GRID_ROWS // Q_ROWS
KEY_ROWS = 12
N_KEYS = KEY_ROWS * GRID_W
N_REL_ROWS = 2 * WIN_H - 1
N_ROW_PAIRS = N_REL_ROWS + 1


def _first_key_row(m):
    return min(max(Q_ROWS * m - WIN_H // 2, 0), GRID_ROWS - KEY_ROWS)


def _fill_bias(rp_ref, bias_scr):
    shape = (N_ROW_PAIRS * GRID_W, LANES)
    qc = lax.broadcasted_iota(jnp.int32, shape, 0) & (GRID_W - 1)
    lane = lax.broadcasted_iota(jnp.int32, shape, 1)
    kc = lane & (GRID_W - 1)
    col_start = jnp.clip(qc - WIN_W // 2, 0, GRID_W - WIN_W)
    valid = (kc >= col_start) & (kc < col_start + WIN_W)
    first_half = lax.broadcasted_iota(jnp.int32, (GRID_W, LANES), 1) < GRID_W
    for h in range(N_HEADS):
        t = jnp.broadcast_to(rp_ref[h][:, None, :], (N_ROW_PAIRS, GRID_W, LANES)).reshape(shape)
        t = pltpu.roll(t, LANES - (WIN_W - 1), 1)
        for bit in range(6):
            t = jnp.where(((qc >> bit) & 1) == 1, pltpu.roll(t, 1 << bit, 1), t)
        t = jnp.where(valid, t, -jnp.inf)
        for m in range(N_Q_BLOCKS):
            for ql in range(Q_ROWS):
                qr = Q_ROWS * m + ql
                win0 = min(max(qr - WIN_H // 2, 0), GRID_ROWS - WIN_H)
                for u in range(KEY_ROWS // 2):
                    kr = _first_key_row(m) + 2 * u
                    in0 = win0 <= kr < win0 + WIN_H
                    in1 = win0 <= kr + 1 < win0 + WIN_H
                    if in0 or in1:
                        pair = kr - qr + WIN_H
                        tile = t[pair * GRID_W:(pair + 1) * GRID_W]
                        if not in0:
                            tile = jnp.where(first_half, -jnp.inf, tile)
                        if not in1:
                            tile = jnp.where(first_half, tile, -jnp.inf)
                    else:
                        tile = jnp.full((GRID_W, LANES), -jnp.inf, F32)
                    bias_scr[h, m, ql * GRID_W:(ql + 1) * GRID_W, u * LANES:(u + 1) * LANES] = tile


def _lat_attn_kernel(q_ref, k_ref, v_ref, cache_ref, rp_ref, o_ref, bias_scr):
    hp = pl.program_id(0)
    m = pl.program_id(1)

    @pl.when((hp == 0) & (m == 0))
    def _():
        _fill_bias(rp_ref, bias_scr)

    row0 = jnp.clip(Q_ROWS * m - WIN_H // 2, 0, GRID_ROWS - KEY_ROWS)
    start = pl.multiple_of(row0 * GRID_W, Q_BLOCK)
    for s in range(q_ref.shape[0]):
        kvb = [
            (k_ref[s, pl.ds(start, N_KEYS), :], v_ref[s, pl.ds(start, N_KEYS), :],
             (bias_scr[2 * hp, m], bias_scr[2 * hp + 1, m])),
            (cache_ref[0, s], cache_ref[1, s], None),
        ]
        o_ref[s] = _attend_pair(q_ref[s], kvb).astype(BF16)


def _lat_attention(qkv, cache_kv, layer, rp):
    nseq = qkv.shape[0] // LAT_LEN
    qkv = qkv.reshape(nseq, LAT_LEN, 3 * ATTN_W)
    n_hp = ATTN_W // LANES
    out = pl.pallas_call(
        _lat_attn_kernel,
        grid=(n_hp, N_Q_BLOCKS),
        in_specs=[
            pl.BlockSpec((nseq, Q_BLOCK, LANES), lambda hp, m: (0, m, hp)),
            pl.BlockSpec((nseq, LAT_LEN, LANES), lambda hp, m: (0, 0, n_hp + hp)),
            pl.BlockSpec((nseq, LAT_LEN, LANES), lambda hp, m: (0, 0, 2 * n_hp + hp)),
            pl.BlockSpec((None, 2, nseq, PAST_LEN, LANES), lambda hp, m: (layer, 0, 0, 0, hp)),
            _resident(rp.shape),
        ],
        out_specs=pl.BlockSpec((nseq, Q_BLOCK, LANES), lambda hp, m: (0, m, hp)),
        out_shape=jax.ShapeDtypeStruct((nseq, LAT_LEN, ATTN_W), BF16),
        scratch_shapes=[pltpu.VMEM((N_HEADS, N_Q_BLOCKS, Q_BLOCK, N_KEYS), F32)],
        compiler_params=_params("arbitrary", "arbitrary"),
        name="attn_lat",
    )(qkv, qkv, qkv, cache_kv, rp)
    return out.reshape(nseq * LAT_LEN, ATTN_W)


def _row_pair_table(rpb_l):
    lane_pad = GRID_W - rpb_l.shape[-1]
    lo = jnp.pad(rpb_l, ((0, 0), (1, 0), (0, lane_pad)))
    hi = jnp.pad(rpb_l, ((0, 0), (0, 1), (0, lane_pad)))
    return jnp.concatenate([lo, hi], axis=-1)


def _mix_kernel(x_ref, mod_ref, yc_ref, ya_ref, d_ref, gates_ref, pw_ref, ps_ref, wb_ref, wo_ref, o_ref):
    yp = [jnp.dot(d_ref[:, g * POOL_GROUP:(g + 1) * POOL_GROUP], pw_ref[g], preferred_element_type=F32)
          for g in range(len(POOL_SIZES))]
    y_pool = (jnp.concatenate(yp, axis=-1) * ps_ref[...]).astype(BF16)
    merged = None
    for i, y in enumerate((yc_ref[...], ya_ref[...], y_pool)):
        proj = jnp.dot(y, wb_ref[i], preferred_element_type=F32)
        term = gates_ref[:, i * D_MODEL:(i + 1) * D_MODEL].astype(F32) * proj
        merged = term if merged is None else merged + term
    mix = jnp.dot(merged.astype(BF16), wo_ref[...], preferred_element_type=F32)
    o_ref[...] = x_ref[...] + mod_ref[0, 2:3, :] * mix


def _mix(x, mod, yconv, yattn, d, gates, pool_w, pool_scale, w_branch, w_out, *, layer):
    ntok = x.shape[0]
    row = lambda i: (i, 0)
    return pl.pallas_call(
        _mix_kernel,
        grid=(ntok // MIX_TILE,),
        in_specs=[
            pl.BlockSpec((MIX_TILE, D_MODEL), row),
            _mod_spec(mod, ntok, MIX_TILE),
            pl.BlockSpec((MIX_TILE, CONV_W), row),
            pl.BlockSpec((MIX_TILE, ATTN_W), row),
            pl.BlockSpec((MIX_TILE, POOL_W), row),
            pl.BlockSpec((MIX_TILE, N_BRANCH * D_MODEL), row),
            _resident_layer((len(POOL_SIZES), POOL_GROUP, POOL_GROUP), layer),
            _resident((1, POOL_W)),
            _resident_layer((N_BRANCH, CONV_W, D_MODEL), layer),
            _resident_layer((D_MODEL, D_MODEL), layer),
        ],
        out_specs=pl.BlockSpec((MIX_TILE, D_MODEL), row),
        out_shape=jax.ShapeDtypeStruct((ntok, D_MODEL), F32),
        compiler_params=_params("arbitrary"),
        name="mix",
    )(x, mod, yconv, yattn, d, gates, pool_w, pool_scale, w_branch, w_out)


def _ffn_kernel(x_ref, mod_ref, g2_ref, wu_ref, fc_ref, wd_ref, gf_ref, o_ref, h_scr, act_scr,
                *, seq_len, final):
    h = _norm_mod(x_ref[...], g2_ref[...], mod_ref[0, 3:4, :], mod_ref[0, 4:5, :])
    h_scr[...] = h.astype(BF16)
    row = _row_in_seq((x_ref.shape[0], FF_CHUNK), seq_len)
    for c in range(N_FF_CHUNKS):
        sl = slice(c * FF_CHUNK, (c + 1) * FF_CHUNK)
        sl_val = slice(D_FF + c * FF_CHUNK, D_FF + (c + 1) * FF_CHUNK)
        u = jnp.dot(h_scr[...], wu_ref[:, sl], preferred_element_type=F32)
        val = jnp.dot(h_scr[...], wu_ref[:, sl_val], preferred_element_type=F32)
        u = _dwconv3(u, fc_ref.at[:, sl], row, seq_len)
        act_scr[:, sl] = (jax.nn.gelu(u, approximate=True) * val).astype(BF16)
    y = jnp.dot(act_scr[...], wd_ref[...], preferred_element_type=F32)
    xn = x_ref[...] + mod_ref[0, 5:6, :] * y
    if final:
        ms = jnp.mean(xn * xn, axis=-1, keepdims=True)
        xn = xn * lax.rsqrt(ms + EPS) * gf_ref[...]
    o_ref[...] = xn


def _ffn(x, mod, g2, w_up, f_conv, w_down, g_final, *, seq_len, final, layer):
    ntok = x.shape[0]
    return pl.pallas_call(
        functools.partial(_ffn_kernel, seq_len=seq_len, final=final),
        grid=(ntok // TOK_TILE,),
        in_specs=[
            pl.BlockSpec((TOK_TILE, D_MODEL), lambda i: (i, 0)),
            _mod_spec(mod, ntok, TOK_TILE),
            _resident((1, D_MODEL)),
            _resident_layer((D_MODEL, 2 * D_FF), layer),
            _resident((3, D_FF)),
            _resident_layer((D_FF, D_MODEL), layer),
            _resident((1, D_MODEL)),
        ],
        out_specs=pl.BlockSpec((TOK_TILE, D_MODEL), lambda i: (i, 0)),
        out_shape=jax.ShapeDtypeStruct((ntok, D_MODEL), F32),
        scratch_shapes=[pltpu.VMEM((TOK_TILE, D_MODEL), BF16), pltpu.VMEM((TOK_TILE, D_FF), BF16)],
        compiler_params=_params("arbitrary"),
        name="ffn",
    )(x, mod, g2, w_up, f_conv, w_down, g_final)


def kernel(x_prompt, x_sample, cache_kv, c, c_ctx, w_mod, b_mod, g_norm1, g_norm2, w_in, conv_w, rpb,
           pool_w, pool_scale, w_branch, w_out, ffn_w_up, ffn_conv, ffn_w_down, g_final):
    n_mod = 8
    cvec = jnp.concatenate([c_ctx[None, :], c, jnp.zeros((n_mod - 1 - N_LAT_SEQ, D_MODEL), F32)], axis=0)
    mod = _adaln(cvec, w_mod, b_mod).reshape(DEPTH, n_mod, 6, D_MODEL)

    xp = x_prompt.reshape(N_CTX_SEQ * CTX_LEN, D_MODEL)
    xs = x_sample.reshape(N_LAT_SEQ * LAT_LEN, D_MODEL)
    gf = g_final.reshape(1, D_MODEL)
    cache_tok = cache_kv.transpose(1, 2, 0, 4, 3, 5).reshape(DEPTH, 2, N_LAT_SEQ, PAST_LEN, ATTN_W).astype(BF16)
    w_in_b, pool_w_b, w_br_b, w_out_b, w_up_b, w_dn_b = _cast_weights(
        w_in, pool_w, w_branch, w_out, ffn_w_up, ffn_w_down)
    kv_state = None
    for l in range(DEPTH):
        pool_s_l = pool_scale[l].reshape(1, POOL_W)
        g1 = g_norm1[l].reshape(1, D_MODEL)
        g2 = g_norm2[l].reshape(1, D_MODEL)
        mod_ctx = mod[l, 0:1]
        mod_lat = mod[l, 1:1 + N_LAT_SEQ]
        final = l == DEPTH - 1

        yc, qkv, d, gates, kv_state = _inproj(xp, mod_ctx, g1, w_in_b, conv_w[l], kv_state, seq_len=CTX_LEN,
                                              tile=CTX_IN_TILE, kv_mode="new" if l == 0 else "stack", layer=l)
        ya = _ctx_attention(qkv)
        xp = _mix(xp, mod_ctx, yc, ya, d, gates, pool_w_b, pool_s_l, w_br_b, w_out_b, layer=l)
        xp = _ffn(xp, mod_ctx, g2, w_up_b, ffn_conv[l], w_dn_b, gf, seq_len=CTX_LEN, final=final, layer=l)

        yc, qkv, d, gates = _inproj(xs, mod_lat, g1, w_in_b, conv_w[l], seq_len=LAT_LEN, tile=TOK_TILE,
                                    kv_mode=None, layer=l)
        ya = _lat_attention(qkv, cache_tok, l, _row_pair_table(rpb[l]))
        xs = _mix(xs, mod_lat, yc, ya, d, gates, pool_w_b, pool_s_l, w_br_b, w_out_b, layer=l)
        xs = _ffn(xs, mod_lat, g2, w_up_b, ffn_conv[l], w_dn_b, gf, seq_len=LAT_LEN, final=final, layer=l)

    y_prompt = xp.reshape(N_CTX_SEQ, CTX_LEN, D_MODEL)
    y_sample = xs.reshape(N_LAT_SEQ, LAT_LEN, D_MODEL)
    return (y_prompt, y_sample, kv_state)
```

```python
import functools

import jax
import jax.numpy as jnp
from jax import lax
from jax.experimental import pallas as pl
from jax.experimental.pallas import tpu as pltpu

F32 = jnp.float32
BF16 = jnp.bfloat16

D_MODEL = 1024
N_CTX_SEQ = 16
CTX_LEN = 256
DEPTH = 2
N_LAT_SEQ = 4
LAT_LEN = 1024
PAST_LEN = 256
GRID_W = 64
GRID_ROWS = LAT_LEN // GRID_W
CONV_W = 512
N_HEADS = 8
HEAD_DIM = 64
ATTN_W = N_HEADS * HEAD_DIM
POOL_W = 512
POOL_SIZES = (2, 4, 8, 16)
POOL_GROUP = POOL_W // len(POOL_SIZES)
N_BRANCH = 3
WIN_H = 8
WIN_W = 16
D_FF = 2816
EPS = 1e-6
IN_W = 3 * CONV_W + 3 * ATTN_W + POOL_W + N_BRANCH * D_MODEL

LANES = 128
TOK_TILE = 1024
CTX_IN_TILE = 512
IN_CHUNK = 512
FF_CHUNK = 256
N_FF_CHUNKS = D_FF // FF_CHUNK
MIX_TILE = 1024
VMEM_LIMIT = 56 * 1024 * 1024


def _params(*sem):
    return pltpu.CompilerParams(dimension_semantics=sem, vmem_limit_bytes=VMEM_LIMIT)


def _resident(shape):
    return pl.BlockSpec(shape, lambda *_: (0,) * len(shape), pipeline_mode=pl.Buffered(1))


def _resident_layer(shape, layer):
    return pl.BlockSpec((None,) + shape, lambda *_: (layer,) + (0,) * len(shape), pipeline_mode=pl.Buffered(1))


def _mod_spec(mod, ntok, tile):
    tiles_per_mod = ntok // mod.shape[0] // tile
    return pl.BlockSpec((1, 6, D_MODEL), lambda i, *_: (i // tiles_per_mod, 0, 0))


def _norm_mod(x, g, shift, scale):
    ms = jnp.mean(x * x, axis=-1, keepdims=True)
    return (x * lax.rsqrt(ms + EPS) * g) * (1.0 + scale) + shift


def _row_in_seq(shape, seq_len):
    return lax.broadcasted_iota(jnp.int32, shape, 0) & (seq_len - 1)


def _shift_rows(x, s, row, seq_len):
    y = pltpu.roll(x, s % x.shape[0], 0)
    ok = (row >= s) if s > 0 else (row < seq_len + s)
    return jnp.where(ok, y, 0.0)


def _dwconv3(x, w_ref, row, seq_len):
    return (_shift_rows(x, 1, row, seq_len) * w_ref[0:1, :] + x * w_ref[1:2, :]
            + _shift_rows(x, -1, row, seq_len) * w_ref[2:3, :])


def _pool_delta(p, seq_len):
    row = _row_in_seq((p.shape[0], POOL_GROUP), seq_len)
    outs = []
    for gi, w in enumerate(POOL_SIZES):
        pg = p[:, gi * POOL_GROUP:(gi + 1) * POOL_GROUP]
        fwd = bwd = pg
        k = 1
        while k < w // 2:
            fwd = fwd + _shift_rows(fwd, -k, row, seq_len)
            bwd = bwd + _shift_rows(bwd, k, row, seq_len)
            k *= 2
        acc = fwd + _shift_rows(bwd, 1, row, seq_len)
        lo = jnp.maximum(row - w // 2, 0)
        hi = jnp.minimum(row - w // 2 + w, seq_len)
        outs.append(acc / (hi - lo).astype(F32) - pg)
    return jnp.concatenate(outs, axis=-1)


CAST_STEPS = 16


def _cast_kernel(*refs):
    n = len(refs) // 2
    for src, dst in zip(refs[:n], refs[n:]):
        dst[...] = src[...].astype(BF16)


def _cast_weights(*weights):
    flat = [w.reshape(-1, w.shape[-1]) for w in weights]
    specs = [pl.BlockSpec((w.shape[0] // CAST_STEPS, w.shape[1]), lambda i: (i, 0)) for w in flat]
    outs = pl.pallas_call(
        _cast_kernel,
        grid=(CAST_STEPS,),
        in_specs=specs,
        out_specs=specs,
        out_shape=[jax.ShapeDtypeStruct(w.shape, BF16) for w in flat],
        compiler_params=_params("arbitrary"),
        name="cast_weights",
    )(*flat)
    return [o.reshape(w.shape) for o, w in zip(outs, weights)]


def _adaln_kernel(c_ref, w_ref, b_ref, o_ref):
    c = c_ref[...]
    s = c * jax.nn.sigmoid(c)
    o_ref[0] = jnp.dot(s.astype(BF16), w_ref[0].astype(BF16), preferred_element_type=F32) + b_ref[0]


def _adaln(cvec, w_mod, b_mod):
    nb = cvec.shape[0]
    n_out = 6 * D_MODEL
    blk = 1024
    return pl.pallas_call(
        _adaln_kernel,
        grid=(DEPTH, n_out // blk),
        in_specs=[
            pl.BlockSpec((nb, D_MODEL), lambda l, n: (0, 0)),
            pl.BlockSpec((1, D_MODEL, blk), lambda l, n: (l, 0, n)),
            pl.BlockSpec((1, 1, blk), lambda l, n: (l, 0, n)),
        ],
        out_specs=pl.BlockSpec((1, nb, blk), lambda l, n: (l, 0, n)),
        out_shape=jax.ShapeDtypeStruct((DEPTH, nb, n_out), F32),
        compiler_params=_params("arbitrary", "arbitrary"),
        name="adaln",
    )(cvec, w_mod, b_mod.reshape(DEPTH, 1, n_out))


COL_CONV = 0
COL_QKV = 3 * CONV_W
COL_POOL = COL_QKV + 3 * ATTN_W
COL_GATE = COL_POOL + POOL_W
N_GATE_COLS = N_BRANCH * D_MODEL


def _inproj_kernel(x_ref, mod_ref, g_ref, w_ref, cw_ref, yconv_ref, qkv_ref, d_ref, gates_ref, *rest,
                   seq_len, with_kv):
    if with_kv:
        kv_ref, h_scr = rest
    else:
        (h_scr,) = rest

    def proj(col):
        return jnp.dot(h_scr[...], w_ref[:, col:col + IN_CHUNK], preferred_element_type=F32)

    h = _norm_mod(x_ref[...], g_ref[...], mod_ref[0, 0:1, :], mod_ref[0, 1:2, :])
    h_scr[...] = h.astype(BF16)
    row = _row_in_seq((x_ref.shape[0], IN_CHUNK), seq_len)
    b_gate = proj(COL_CONV)
    conv = _dwconv3(proj(COL_CONV + CONV_W) * proj(COL_CONV + 2 * CONV_W), cw_ref, row, seq_len)
    yconv_ref[...] = (b_gate * conv).astype(BF16)
    qkv_ref[:, 0:ATTN_W] = (proj(COL_QKV) * (HEAD_DIM ** -0.5)).astype(BF16)
    for kv in range(2):
        z = proj(COL_QKV + (1 + kv) * ATTN_W)
        qkv_ref[:, (1 + kv) * ATTN_W:(2 + kv) * ATTN_W] = z.astype(BF16)
        if with_kv:
            for s in range(z.shape[0] // seq_len):
                for h_i in range(N_HEADS):
                    kv_ref[s, kv, h_i] = z[s * seq_len:(s + 1) * seq_len, h_i * HEAD_DIM:(h_i + 1) * HEAD_DIM]
    d_ref[...] = _pool_delta(proj(COL_POOL), seq_len).astype(BF16)
    for c in range(N_GATE_COLS // IN_CHUNK):
        z = proj(COL_GATE + c * IN_CHUNK)
        gates_ref[:, c * IN_CHUNK:(c + 1) * IN_CHUNK] = jax.nn.sigmoid(z).astype(BF16)


def _inproj(x, mod, g1, w_in, conv_w, *, seq_len, tile, with_kv, layer):
    ntok = x.shape[0]
    row = lambda i: (i, 0)
    widths = (CONV_W, 3 * ATTN_W, POOL_W, N_GATE_COLS)
    out_shape = [jax.ShapeDtypeStruct((ntok, w), BF16) for w in widths]
    out_specs = [pl.BlockSpec((tile, w), row) for w in widths]
    if with_kv:
        one_layer = (2, N_HEADS, seq_len, HEAD_DIM)
        out_shape.append(jax.ShapeDtypeStruct((ntok // seq_len,) + one_layer, F32))
        out_specs.append(pl.BlockSpec((tile // seq_len,) + one_layer, lambda i: (i, 0, 0, 0, 0)))
    return pl.pallas_call(
        functools.partial(_inproj_kernel, seq_len=seq_len, with_kv=with_kv),
        grid=(ntok // tile,),
        in_specs=[
            pl.BlockSpec((tile, D_MODEL), row),
            _mod_spec(mod, ntok, tile),
            _resident((1, D_MODEL)),
            _resident_layer((D_MODEL, IN_W), layer),
            _resident((3, CONV_W)),
        ],
        out_specs=out_specs,
        out_shape=out_shape,
        scratch_shapes=[pltpu.VMEM((tile, D_MODEL), BF16)],
        compiler_params=_params("arbitrary"),
        name="inproj_ctx" if with_kv else "inproj_lat",
    )(x, mod, g1, w_in, conv_w)


def _attend_pair(q2, kvb):
    lane = lax.broadcasted_iota(jnp.int32, (1, LANES), 1)
    out = None
    for half in range(2):
        m = (lane >= half * HEAD_DIM) & (lane < (half + 1) * HEAD_DIM)
        qh = jnp.where(m, q2, jnp.zeros_like(q2))
        scores = []
        for k2, _, bias in kvb:
            s = lax.dot_general(qh, k2, (((1,), (1,)), ((), ())), preferred_element_type=F32)
            if bias is not None:
                s = s + bias[half]
            scores.append(s)
        mx = functools.reduce(jnp.maximum, [jnp.max(s, axis=-1, keepdims=True) for s in scores])
        es = [jnp.exp(s - mx) for s in scores]
        den = functools.reduce(lambda a, b: a + b, [jnp.sum(e, axis=-1, keepdims=True) for e in es])
        acc = None
        for e, (_, v2, _) in zip(es, kvb):
            vh = jnp.where(m, v2, jnp.zeros_like(v2))
            o = jnp.dot(e.astype(BF16), vh, preferred_element_type=F32)
            acc = o if acc is None else acc + o
        acc = acc * (1.0 / den)
        out = acc if out is None else out + acc
    return out


CTX_ATTN_SEQS = 4


def _ctx_attn_kernel(q_ref, k_ref, v_ref, o_ref):
    for s in range(q_ref.shape[0] // CTX_LEN):
        rows = slice(s * CTX_LEN, (s + 1) * CTX_LEN)
        for hp in range(N_HEADS // 2):
            sl = slice(hp * LANES, (hp + 1) * LANES)
            o = _attend_pair(q_ref[rows, sl], [(k_ref[rows, sl], v_ref[rows, sl], None)])
            o_ref[rows, sl] = o.astype(BF16)


def _ctx_attention(qkv):
    ntok = qkv.shape[0]
    blk = CTX_ATTN_SEQS * CTX_LEN
    return pl.pallas_call(
        _ctx_attn_kernel,
        grid=(ntok // blk,),
        in_specs=[
            pl.BlockSpec((blk, ATTN_W), lambda b: (b, 0)),
            pl.BlockSpec((blk, ATTN_W), lambda b: (b, 1)),
            pl.BlockSpec((blk, ATTN_W), lambda b: (b, 2)),
        ],
        out_specs=pl.BlockSpec((blk, ATTN_W), lambda b: (b, 0)),
        out_shape=jax.ShapeDtypeStruct((ntok, ATTN_W), BF16),
        compiler_params=_params("arbitrary"),
        name="attn_ctx",
    )(qkv, qkv, qkv)


Q_ROWS = 4
Q_BLOCK = Q_ROWS * GRID_W
N_Q_BLOCKS = GRID_ROWS // Q_ROWS
KEY_ROWS = 12
N_KEYS = KEY_ROWS * GRID_W
N_REL_ROWS = 2 * WIN_H - 1
N_ROW_PAIRS = N_REL_ROWS + 1


def _first_key_row(m):
    return min(max(Q_ROWS * m - WIN_H // 2, 0), GRID_ROWS - KEY_ROWS)


def _fill_bias(rp_ref, bias_scr):
    shape = (N_ROW_PAIRS * GRID_W, LANES)
    qc = lax.broadcasted_iota(jnp.int32, shape, 0) & (GRID_W - 1)
    lane = lax.broadcasted_iota(jnp.int32, shape, 1)
    kc = lane & (GRID_W - 1)
    col_start = jnp.clip(qc - WIN_W // 2, 0, GRID_W - WIN_W)
    valid = (kc >= col_start) & (kc < col_start + WIN_W)
    first_half = lax.broadcasted_iota(jnp.int32, (GRID_W, LANES), 1) < GRID_W
    for h in range(N_HEADS):
        t = jnp.broadcast_to(rp_ref[h][:, None, :], (N_ROW_PAIRS, GRID_W, LANES)).reshape(shape)
        t = pltpu.roll(t, LANES - (WIN_W - 1), 1)
        for bit in range(6):
            t = jnp.where(((qc >> bit) & 1) == 1, pltpu.roll(t, 1 << bit, 1), t)
        t = jnp.where(valid, t, -jnp.inf)
        for m in range(N_Q_BLOCKS):
            for ql in range(Q_ROWS):
                qr = Q_ROWS * m + ql
                win0 = min(max(qr - WIN_H // 2, 0), GRID_ROWS - WIN_H)
                for u in range(KEY_ROWS // 2):
                    kr = _first_key_row(m) + 2 * u
                    in0 = win0 <= kr < win0 + WIN_H
                    in1 = win0 <= kr + 1 < win0 + WIN_H
                    if in0 or in1:
                        pair = kr - qr + WIN_H
                        tile = t[pair * GRID_W:(pair + 1) * GRID_W]
                        if not in0:
                            tile = jnp.where(first_half, -jnp.inf, tile)
                        if not in1:
                            tile = jnp.where(first_half, tile, -jnp.inf)
                    else:
                        tile = jnp.full((GRID_W, LANES), -jnp.inf, F32)
                    bias_scr[h, m, ql * GRID_W:(ql + 1) * GRID_W, u * LANES:(u + 1) * LANES] = tile


def _lat_attn_kernel(q_ref, k_ref, v_ref, cache_ref, rp_ref, *rest, n_kv):
    kv_in, o_ref, bias_scr = rest[:n_kv], rest[n_kv], rest[-1]
    if n_kv:
        kv_out = rest[n_kv + 1]
        for layer, src in enumerate(kv_in):
            kv_out[0, layer] = src[0]
    hp = pl.program_id(0)
    m = pl.program_id(1)

    @pl.when((hp == 0) & (m == 0))
    def _():
        _fill_bias(rp_ref, bias_scr)

    row0 = jnp.clip(Q_ROWS * m - WIN_H // 2, 0, GRID_ROWS - KEY_ROWS)
    start = pl.multiple_of(row0 * GRID_W, Q_BLOCK)
    for s in range(q_ref.shape[0]):
        kvb = [
            (k_ref[s, pl.ds(start, N_KEYS), :], v_ref[s, pl.ds(start, N_KEYS), :],
             (bias_scr[2 * hp, m], bias_scr[2 * hp + 1, m])),
            (cache_ref[0, s], cache_ref[1, s], None),
        ]
        o_ref[s] = _attend_pair(q_ref[s], kvb).astype(BF16)


def _lat_attention(qkv, cache_kv, layer, rp, kv_layers=()):
    nseq = qkv.shape[0] // LAT_LEN
    qkv = qkv.reshape(nseq, LAT_LEN, 3 * ATTN_W)
    n_hp = ATTN_W // LANES
    in_specs = [
        pl.BlockSpec((nseq, Q_BLOCK, LANES), lambda hp, m: (0, m, hp)),
        pl.BlockSpec((nseq, LAT_LEN, LANES), lambda hp, m: (0, 0, n_hp + hp)),
        pl.BlockSpec((nseq, LAT_LEN, LANES), lambda hp, m: (0, 0, 2 * n_hp + hp)),
        pl.BlockSpec((None, 2, nseq, PAST_LEN, LANES), lambda hp, m: (layer, 0, 0, 0, hp)),
        _resident(rp.shape),
    ]
    out_specs = [pl.BlockSpec((nseq, Q_BLOCK, LANES), lambda hp, m: (0, m, hp))]
    out_shape = [jax.ShapeDtypeStruct((nseq, LAT_LEN, ATTN_W), BF16)]
    if kv_layers:
        n_ctx, *slab = kv_layers[0].shape
        assert n_ctx == n_hp * N_Q_BLOCKS
        step = lambda hp, m: hp * N_Q_BLOCKS + m
        in_specs += [pl.BlockSpec((1, *slab), lambda hp, m: (step(hp, m), 0, 0, 0, 0))] * len(kv_layers)
        out_specs.append(pl.BlockSpec((1, len(kv_layers), *slab), lambda hp, m: (step(hp, m), 0, 0, 0, 0, 0)))
        out_shape.append(jax.ShapeDtypeStruct((n_ctx, len(kv_layers), *slab), F32))
    outs = pl.pallas_call(
        functools.partial(_lat_attn_kernel, n_kv=len(kv_layers)),
        grid=(n_hp, N_Q_BLOCKS),
        in_specs=in_specs,
        out_specs=out_specs,
        out_shape=out_shape,
        scratch_shapes=[pltpu.VMEM((N_HEADS, N_Q_BLOCKS, Q_BLOCK, N_KEYS), F32)],
        compiler_params=_params("arbitrary", "arbitrary"),
        name="attn_lat",
    )(qkv, qkv, qkv, cache_kv, rp, *kv_layers)
    y = outs[0].reshape(nseq * LAT_LEN, ATTN_W)
    return (y, outs[1]) if kv_layers else y


def _row_pair_table(rpb_l):
    lane_pad = GRID_W - rpb_l.shape[-1]
    lo = jnp.pad(rpb_l, ((0, 0), (1, 0), (0, lane_pad)))
    hi = jnp.pad(rpb_l, ((0, 0), (0, 1), (0, lane_pad)))
    return jnp.concatenate([lo, hi], axis=-1)


def _mix_kernel(x_ref, mod_ref, yc_ref, ya_ref, d_ref, gates_ref, pw_ref, ps_ref, wb_ref, wo_ref, o_ref):
    yp = [jnp.dot(d_ref[:, g * POOL_GROUP:(g + 1) * POOL_GROUP], pw_ref[g], preferred_element_type=F32)
          for g in range(len(POOL_SIZES))]
    y_pool = (jnp.concatenate(yp, axis=-1) * ps_ref[...]).astype(BF16)
    merged = None
    for i, y in enumerate((yc_ref[...], ya_ref[...], y_pool)):
        proj = jnp.dot(y, wb_ref[i], preferred_element_type=F32)
        term = gates_ref[:, i * D_MODEL:(i + 1) * D_MODEL].astype(F32) * proj
        merged = term if merged is None else merged + term
    mix = jnp.dot(merged.astype(BF16), wo_ref[...], preferred_element_type=F32)
    o_ref[...] = x_ref[...] + mod_ref[0, 2:3, :] * mix


def _mix(x, mod, yconv, yattn, d, gates, pool_w, pool_scale, w_branch, w_out, *, layer):
    ntok = x.shape[0]
    row = lambda i: (i, 0)
    return pl.pallas_call(
        _mix_kernel,
        grid=(ntok // MIX_TILE,),
        in_specs=[
            pl.BlockSpec((MIX_TILE, D_MODEL), row),
            _mod_spec(mod, ntok, MIX_TILE),
            pl.BlockSpec((MIX_TILE, CONV_W), row),
            pl.BlockSpec((MIX_TILE, ATTN_W), row),
            pl.BlockSpec((MIX_TILE, POOL_W), row),
            pl.BlockSpec((MIX_TILE, N_BRANCH * D_MODEL), row),
            _resident_layer((len(POOL_SIZES), POOL_GROUP, POOL_GROUP), layer),
            _resident((1, POOL_W)),
            _resident_layer((N_BRANCH, CONV_W, D_MODEL), layer),
            _resident_layer((D_MODEL, D_MODEL), layer),
        ],
        out_specs=pl.BlockSpec((MIX_TILE, D_MODEL), row),
        out_shape=jax.ShapeDtypeStruct((ntok, D_MODEL), F32),
        compiler_params=_params("arbitrary"),
        name="mix",
    )(x, mod, yconv, yattn, d, gates, pool_w, pool_scale, w_branch, w_out)


def _ffn_kernel(x_ref, mod_ref, g2_ref, wu_ref, fc_ref, wd_ref, gf_ref, o_ref, h_scr, act_scr,
                *, seq_len, final):
    h = _norm_mod(x_ref[...], g2_ref[...], mod_ref[0, 3:4, :], mod_ref[0, 4:5, :])
    h_scr[...] = h.astype(BF16)
    row = _row_in_seq((x_ref.shape[0], FF_CHUNK), seq_len)
    for c in range(N_FF_CHUNKS):
        sl = slice(c * FF_CHUNK, (c + 1) * FF_CHUNK)
        sl_val = slice(D_FF + c * FF_CHUNK, D_FF + (c + 1) * FF_CHUNK)
        u = jnp.dot(h_scr[...], wu_ref[:, sl], preferred_element_type=F32)
        val = jnp.dot(h_scr[...], wu_ref[:, sl_val], preferred_element_type=F32)
        u = _dwconv3(u, fc_ref.at[:, sl], row, seq_len)
        act_scr[:, sl] = (jax.nn.gelu(u, approximate=True) * val).astype(BF16)
    y = jnp.dot(act_scr[...], wd_ref[...], preferred_element_type=F32)
    xn = x_ref[...] + mod_ref[0, 5:6, :] * y
    if final:
        ms = jnp.mean(xn * xn, axis=-1, keepdims=True)
        xn = xn * lax.rsqrt(ms + EPS) * gf_ref[...]
    o_ref[...] = xn


def _ffn(x, mod, g2, w_up, f_conv, w_down, g_final, *, seq_len, final, layer):
    ntok = x.shape[0]
    return pl.pallas_call(
        functools.partial(_ffn_kernel, seq_len=seq_len, final=final),
        grid=(ntok // TOK_TILE,),
        in_specs=[
            pl.BlockSpec((TOK_TILE, D_MODEL), lambda i: (i, 0)),
            _mod_spec(mod, ntok, TOK_TILE),
            _resident((1, D_MODEL)),
            _resident_layer((D_MODEL, 2 * D_FF), layer),
            _resident((3, D_FF)),
            _resident_layer((D_FF, D_MODEL), layer),
            _resident((1, D_MODEL)),
        ],
        out_specs=pl.BlockSpec((TOK_TILE, D_MODEL), lambda i: (i, 0)),
        out_shape=jax.ShapeDtypeStruct((ntok, D_MODEL), F32),
        scratch_shapes=[pltpu.VMEM((TOK_TILE, D_MODEL), BF16), pltpu.VMEM((TOK_TILE, D_FF), BF16)],
        compiler_params=_params("arbitrary"),
        name="ffn",
    )(x, mod, g2, w_up, f_conv, w_down, g_final)


def kernel(x_prompt, x_sample, cache_kv, c, c_ctx, w_mod, b_mod, g_norm1, g_norm2, w_in, conv_w, rpb,
           pool_w, pool_scale, w_branch, w_out, ffn_w_up, ffn_conv, ffn_w_down, g_final):
    n_mod = 8
    cvec = jnp.concatenate([c_ctx[None, :], c, jnp.zeros((n_mod - 1 - N_LAT_SEQ, D_MODEL), F32)], axis=0)
    mod = _adaln(cvec, w_mod, b_mod).reshape(DEPTH, n_mod, 6, D_MODEL)

    xp = x_prompt.reshape(N_CTX_SEQ * CTX_LEN, D_MODEL)
    xs = x_sample.reshape(N_LAT_SEQ * LAT_LEN, D_MODEL)
    gf = g_final.reshape(1, D_MODEL)
    cache_tok = cache_kv.transpose(1, 2, 0, 4, 3, 5).reshape(DEPTH, 2, N_LAT_SEQ, PAST_LEN, ATTN_W).astype(BF16)
    w_in_b, pool_w_b, w_br_b, w_out_b, w_up_b, w_dn_b = _cast_weights(
        w_in, pool_w, w_branch, w_out, ffn_w_up, ffn_w_down)
    kv_layers = []
    kv_state = None
    for l in range(DEPTH):
        pool_s_l = pool_scale[l].reshape(1, POOL_W)
        g1 = g_norm1[l].reshape(1, D_MODEL)
        g2 = g_norm2[l].reshape(1, D_MODEL)
        mod_ctx = mod[l, 0:1]
        mod_lat = mod[l, 1:1 + N_LAT_SEQ]
        final = l == DEPTH - 1

        yc, qkv, d, gates, kv = _inproj(xp, mod_ctx, g1, w_in_b, conv_w[l], seq_len=CTX_LEN, tile=CTX_IN_TILE,
                                        with_kv=True, layer=l)
        kv_layers.append(kv)
        ya = _ctx_attention(qkv)
        xp = _mix(xp, mod_ctx, yc, ya, d, gates, pool_w_b, pool_s_l, w_br_b, w_out_b, layer=l)
        xp = _ffn(xp, mod_ctx, g2, w_up_b, ffn_conv[l], w_dn_b, gf, seq_len=CTX_LEN, final=final, layer=l)

        yc, qkv, d, gates = _inproj(xs, mod_lat, g1, w_in_b, conv_w[l], seq_len=LAT_LEN, tile=TOK_TILE,
                                    with_kv=False, layer=l)
        if final:
            ya, kv_state = _lat_attention(qkv, cache_tok, l, _row_pair_table(rpb[l]), tuple(kv_layers))
        else:
            ya = _lat_attention(qkv, cache_tok, l, _row_pair_table(rpb[l]))
        xs = _mix(xs, mod_lat, yc, ya, d, gates, pool_w_b, pool_s_l, w_br_b, w_out_b, layer=l)
        xs = _ffn(xs, mod_lat, g2, w_up_b, ffn_conv[l], w_dn_b, gf, seq_len=LAT_LEN, final=final, layer=l)

    y_prompt = xp.reshape(N_CTX_SEQ, CTX_LEN, D_MODEL)
    y_sample = xs.reshape(N_LAT_SEQ, LAT_LEN, D_MODEL)
    return (y_prompt, y_sample, kv_state)
```

```python
import functools

import jax
import jax.numpy as jnp
from jax import lax
from jax.experimental import pallas as pl
from jax.experimental.pallas import tpu as pltpu

F32 = jnp.float32
BF16 = jnp.bfloat16

D_MODEL = 1024
N_CTX_SEQ = 16
CTX_LEN = 256
DEPTH = 2
N_LAT_SEQ = 4
LAT_LEN = 1024
PAST_LEN = 256
GRID_W = 64
GRID_ROWS = LAT_LEN // GRID_W
CONV_W = 512
N_HEADS = 8
HEAD_DIM = 64
ATTN_W = N_HEADS * HEAD_DIM
POOL_W = 512
POOL_SIZES = (2, 4, 8, 16)
POOL_GROUP = POOL_W // len(POOL_SIZES)
N_BRANCH = 3
WIN_H = 8
WIN_W = 16
D_FF = 2816
EPS = 1e-6
IN_W = 3 * CONV_W + 3 * ATTN_W + POOL_W + N_BRANCH * D_MODEL

LANES = 128
TOK_TILE = 1024
CTX_IN_TILE = 512
IN_CHUNK = 512
FF_CHUNK = 256
N_FF_CHUNKS = D_FF // FF_CHUNK
MIX_TILE = 1024
NORM_ROWS = 256
VMEM_LIMIT = 56 * 1024 * 1024


def _params(*sem):
    return pltpu.CompilerParams(dimension_semantics=sem, vmem_limit_bytes=VMEM_LIMIT)


def _resident(shape):
    return pl.BlockSpec(shape, lambda *_: (0,) * len(shape), pipeline_mode=pl.Buffered(1))


def _resident_layer(shape, layer):
    return pl.BlockSpec((None,) + shape, lambda *_: (layer,) + (0,) * len(shape), pipeline_mode=pl.Buffered(1))


def _mod_spec(mod, ntok, tile):
    tiles_per_mod = ntok // mod.shape[0] // tile
    return pl.BlockSpec((1, 6, D_MODEL), lambda i, *_: (i // tiles_per_mod, 0, 0))


def _norm_mod(x, g, shift, scale):
    ms = jnp.mean(x * x, axis=-1, keepdims=True)
    return (x * lax.rsqrt(ms + EPS) * g) * (1.0 + scale) + shift


def _row_in_seq(shape, seq_len):
    return lax.broadcasted_iota(jnp.int32, shape, 0) & (seq_len - 1)


def _shift_rows(x, s, row, seq_len):
    y = pltpu.roll(x, s % x.shape[0], 0)
    ok = (row >= s) if s > 0 else (row < seq_len + s)
    return jnp.where(ok, y, 0.0)


def _dwconv3(x, w_ref, row, seq_len):
    return (_shift_rows(x, 1, row, seq_len) * w_ref[0:1, :] + x * w_ref[1:2, :]
            + _shift_rows(x, -1, row, seq_len) * w_ref[2:3, :])


def _pool_delta(pg, w, seq_len):
    row = _row_in_seq(pg.shape, seq_len)
    fwd = bwd = pg
    k = 1
    while k < w // 2:
        fwd = fwd + _shift_rows(fwd, -k, row, seq_len)
        bwd = bwd + _shift_rows(bwd, k, row, seq_len)
        k *= 2
    acc = fwd + _shift_rows(bwd, 1, row, seq_len)
    lo = jnp.maximum(row - w // 2, 0)
    hi = jnp.minimum(row - w // 2 + w, seq_len)
    return acc / (hi - lo).astype(F32) - pg


CAST_STEPS = 16


def _cast_kernel(*refs):
    n = len(refs) // 2
    for src, dst in zip(refs[:n], refs[n:]):
        dst[...] = src[...].astype(BF16)


def _cast_weights(*weights):
    flat = [w.reshape(-1, w.shape[-1]) for w in weights]
    specs = [pl.BlockSpec((w.shape[0] // CAST_STEPS, w.shape[1]), lambda i: (i, 0)) for w in flat]
    outs = pl.pallas_call(
        _cast_kernel,
        grid=(CAST_STEPS,),
        in_specs=specs,
        out_specs=specs,
        out_shape=[jax.ShapeDtypeStruct(w.shape, BF16) for w in flat],
        compiler_params=_params("arbitrary"),
        name="cast_weights",
    )(*flat)
    return [o.reshape(w.shape) for o, w in zip(outs, weights)]


def _adaln_kernel(c_ref, w_ref, b_ref, o_ref):
    c = c_ref[...]
    s = c * jax.nn.sigmoid(c)
    o_ref[0] = jnp.dot(s.astype(BF16), w_ref[0].astype(BF16), preferred_element_type=F32) + b_ref[0]


def _adaln(cvec, w_mod, b_mod):
    nb = cvec.shape[0]
    n_out = 6 * D_MODEL
    blk = 1024
    return pl.pallas_call(
        _adaln_kernel,
        grid=(DEPTH, n_out // blk),
        in_specs=[
            pl.BlockSpec((nb, D_MODEL), lambda l, n: (0, 0)),
            pl.BlockSpec((1, D_MODEL, blk), lambda l, n: (l, 0, n)),
            pl.BlockSpec((1, 1, blk), lambda l, n: (l, 0, n)),
        ],
        out_specs=pl.BlockSpec((1, nb, blk), lambda l, n: (l, 0, n)),
        out_shape=jax.ShapeDtypeStruct((DEPTH, nb, n_out), F32),
        compiler_params=_params("arbitrary", "arbitrary"),
        name="adaln",
    )(cvec, w_mod, b_mod.reshape(DEPTH, 1, n_out))


COL_CONV = 0
COL_QKV = 3 * CONV_W
COL_POOL = COL_QKV + 3 * ATTN_W
COL_GATE = COL_POOL + POOL_W
N_GATE_COLS = N_BRANCH * D_MODEL


def _inproj_kernel(x_ref, mod_ref, g_ref, w_ref, cw_ref, yconv_ref, qkv_ref, d_ref, gates_ref, *rest,
                   seq_len, with_kv):
    if with_kv:
        kv_ref, h_scr = rest
    else:
        (h_scr,) = rest

    def proj(col):
        return jnp.dot(h_scr[...], w_ref[:, col:col + IN_CHUNK], preferred_element_type=F32)

    parts = []
    for r in range(x_ref.shape[0] // NORM_ROWS):
        rows = slice(r * NORM_ROWS, (r + 1) * NORM_ROWS)
        h = _norm_mod(x_ref[rows, :], g_ref[...], mod_ref[0, 0:1, :], mod_ref[0, 1:2, :]).astype(BF16)
        h_scr[rows, :] = h
        parts.append(jnp.dot(h, w_ref[:, COL_CONV:COL_CONV + IN_CHUNK], preferred_element_type=F32))
    b_gate = jnp.concatenate(parts, axis=0)
    row = _row_in_seq((x_ref.shape[0], IN_CHUNK), seq_len)
    conv = _dwconv3(proj(COL_CONV + CONV_W) * proj(COL_CONV + 2 * CONV_W), cw_ref, row, seq_len)
    yconv_ref[...] = (b_gate * conv).astype(BF16)
    p = proj(COL_POOL)
    for c in range(N_GATE_COLS // IN_CHUNK):
        z = proj(COL_GATE + c * IN_CHUNK)
        if c < len(POOL_SIZES):
            lanes = slice(c * POOL_GROUP, (c + 1) * POOL_GROUP)
            d_ref[:, lanes] = _pool_delta(p[:, lanes], POOL_SIZES[c], seq_len).astype(BF16)
        gates_ref[:, c * IN_CHUNK:(c + 1) * IN_CHUNK] = jax.nn.sigmoid(z).astype(BF16)
    for kv in range(2):
        z = proj(COL_QKV + (1 + kv) * ATTN_W)
        qkv_ref[:, (1 + kv) * ATTN_W:(2 + kv) * ATTN_W] = z.astype(BF16)
        if with_kv:
            for s in range(z.shape[0] // seq_len):
                for h_i in range(N_HEADS):
                    kv_ref[s, kv, h_i] = z[s * seq_len:(s + 1) * seq_len, h_i * HEAD_DIM:(h_i + 1) * HEAD_DIM]
    qkv_ref[:, 0:ATTN_W] = (proj(COL_QKV) * (HEAD_DIM ** -0.5)).astype(BF16)


def _inproj(x, mod, g1, w_in, conv_w, *, seq_len, tile, with_kv, layer):
    ntok = x.shape[0]
    row = lambda i: (i, 0)
    widths = (CONV_W, 3 * ATTN_W, POOL_W, N_GATE_COLS)
    out_shape = [jax.ShapeDtypeStruct((ntok, w), BF16) for w in widths]
    out_specs = [pl.BlockSpec((tile, w), row) for w in widths]
    if with_kv:
        one_layer = (2, N_HEADS, seq_len, HEAD_DIM)
        out_shape.append(jax.ShapeDtypeStruct((ntok // seq_len,) + one_layer, F32))
        out_specs.append(pl.BlockSpec((tile // seq_len,) + one_layer, lambda i: (i, 0, 0, 0, 0)))
    return pl.pallas_call(
        functools.partial(_inproj_kernel, seq_len=seq_len, with_kv=with_kv),
        grid=(ntok // tile,),
        in_specs=[
            pl.BlockSpec((tile, D_MODEL), row),
            _mod_spec(mod, ntok, tile),
            _resident((1, D_MODEL)),
            _resident_layer((D_MODEL, IN_W), layer),
            _resident((3, CONV_W)),
        ],
        out_specs=out_specs,
        out_shape=out_shape,
        scratch_shapes=[pltpu.VMEM((tile, D_MODEL), BF16)],
        compiler_params=_params("arbitrary"),
        name="inproj_ctx" if with_kv else "inproj_lat",
    )(x, mod, g1, w_in, conv_w)


def _attend_pair(q2, kvb):
    lane = lax.broadcasted_iota(jnp.int32, (1, LANES), 1)
    out = None
    for half in range(2):
        m = (lane >= half * HEAD_DIM) & (lane < (half + 1) * HEAD_DIM)
        qh = jnp.where(m, q2, jnp.zeros_like(q2))
        scores = []
        for k2, _, bias in kvb:
            s = lax.dot_general(qh, k2, (((1,), (1,)), ((), ())), preferred_element_type=F32)
            if bias is not None:
                s = s + bias[half]
            scores.append(s)
        mx = functools.reduce(jnp.maximum, [jnp.max(s, axis=-1, keepdims=True) for s in scores])
        es = [jnp.exp(s - mx) for s in scores]
        den = functools.reduce(lambda a, b: a + b, [jnp.sum(e, axis=-1, keepdims=True) for e in es])
        acc = None
        for e, (_, v2, _) in zip(es, kvb):
            vh = jnp.where(m, v2, jnp.zeros_like(v2))
            o = jnp.dot(e.astype(BF16), vh, preferred_element_type=F32)
            acc = o if acc is None else acc + o
        acc = acc * (1.0 / den)
        out = acc if out is None else out + acc
    return out


CTX_ATTN_SEQS = 4


def _ctx_attn_kernel(q_ref, k_ref, v_ref, o_ref):
    for s in range(q_ref.shape[0] // CTX_LEN):
        rows = slice(s * CTX_LEN, (s + 1) * CTX_LEN)
        for hp in range(N_HEADS // 2):
            sl = slice(hp * LANES, (hp + 1) * LANES)
            o = _attend_pair(q_ref[rows, sl], [(k_ref[rows, sl], v_ref[rows, sl], None)])
            o_ref[rows, sl] = o.astype(BF16)


def _ctx_attention(qkv):
    ntok = qkv.shape[0]
    blk = CTX_ATTN_SEQS * CTX_LEN
    return pl.pallas_call(
        _ctx_attn_kernel,
        grid=(ntok // blk,),
        in_specs=[
            pl.BlockSpec((blk, ATTN_W), lambda b: (b, 0)),
            pl.BlockSpec((blk, ATTN_W), lambda b: (b, 1)),
            pl.BlockSpec((blk, ATTN_W), lambda b: (b, 2)),
        ],
        out_specs=pl.BlockSpec((blk, ATTN_W), lambda b: (b, 0)),
        out_shape=jax.ShapeDtypeStruct((ntok, ATTN_W), BF16),
        compiler_params=_params("arbitrary"),
        name="attn_ctx",
    )(qkv, qkv, qkv)


Q_ROWS = 4
Q_BLOCK = Q_ROWS * GRID_W
N_Q_BLOCKS = GRID_ROWS // Q_ROWS
KEY_ROWS = 12
N_KEYS = KEY_ROWS * GRID_W
N_REL_ROWS = 2 * WIN_H - 1
N_ROW_PAIRS = N_REL_ROWS + 1


def _first_key_row(m):
    return min(max(Q_ROWS * m - WIN_H // 2, 0), GRID_ROWS - KEY_ROWS)


def _fill_bias(rp_ref, bias_scr):
    shape = (N_ROW_PAIRS * GRID_W, LANES)
    qc = lax.broadcasted_iota(jnp.int32, shape, 0) & (GRID_W - 1)
    lane = lax.broadcasted_iota(jnp.int32, shape, 1)
    kc = lane & (GRID_W - 1)
    col_start = jnp.clip(qc - WIN_W // 2, 0, GRID_W - WIN_W)
    valid = (kc >= col_start) & (kc < col_start + WIN_W)
    first_half = lax.broadcasted_iota(jnp.int32, (GRID_W, LANES), 1) < GRID_W
    for h in range(N_HEADS):
        t = jnp.broadcast_to(rp_ref[h][:, None, :], (N_ROW_PAIRS, GRID_W, LANES)).reshape(shape)
        t = pltpu.roll(t, LANES - (WIN_W - 1), 1)
        for bit in range(6):
            t = jnp.where(((qc >> bit) & 1) == 1, pltpu.roll(t, 1 << bit, 1), t)
        t = jnp.where(valid, t, -jnp.inf)
        for m in range(N_Q_BLOCKS):
            for ql in range(Q_ROWS):
                qr = Q_ROWS * m + ql
                win0 = min(max(qr - WIN_H // 2, 0), GRID_ROWS - WIN_H)
                for u in range(KEY_ROWS // 2):
                    kr = _first_key_row(m) + 2 * u
                    in0 = win0 <= kr < win0 + WIN_H
                    in1 = win0 <= kr + 1 < win0 + WIN_H
                    if in0 or in1:
                        pair = kr - qr + WIN_H
                        tile = t[pair * GRID_W:(pair + 1) * GRID_W]
                        if not in0:
                            tile = jnp.where(first_half, -jnp.inf, tile)
                        if not in1:
                            tile = jnp.where(first_half, tile, -jnp.inf)
                    else:
                        tile = jnp.full((GRID_W, LANES), -jnp.inf, F32)
                    bias_scr[h, m, ql * GRID_W:(ql + 1) * GRID_W, u * LANES:(u + 1) * LANES] = tile


def _lat_attn_kernel(q_ref, k_ref, v_ref, cache_ref, rp_ref, *rest, n_kv):
    kv_in, o_ref, bias_scr = rest[:n_kv], rest[n_kv], rest[-1]
    if n_kv:
        kv_out = rest[n_kv + 1]
        for layer, src in enumerate(kv_in):
            kv_out[0, layer] = src[0]
    hp = pl.program_id(0)
    m = pl.program_id(1)

    @pl.when((hp == 0) & (m == 0))
    def _():
        _fill_bias(rp_ref, bias_scr)

    row0 = jnp.clip(Q_ROWS * m - WIN_H // 2, 0, GRID_ROWS - KEY_ROWS)
    start = pl.multiple_of(row0 * GRID_W, Q_BLOCK)
    for s in range(q_ref.shape[0]):
        kvb = [
            (k_ref[s, pl.ds(start, N_KEYS), :], v_ref[s, pl.ds(start, N_KEYS), :],
             (bias_scr[2 * hp, m], bias_scr[2 * hp + 1, m])),
            (cache_ref[0, s], cache_ref[1, s], None),
        ]
        o_ref[s] = _attend_pair(q_ref[s], kvb).astype(BF16)


def _lat_attention(qkv, cache_kv, layer, rp, kv_layers=()):
    nseq = qkv.shape[0] // LAT_LEN
    qkv = qkv.reshape(nseq, LAT_LEN, 3 * ATTN_W)
    n_hp = ATTN_W // LANES
    in_specs = [
        pl.BlockSpec((nseq, Q_BLOCK, LANES), lambda hp, m: (0, m, hp)),
        pl.BlockSpec((nseq, LAT_LEN, LANES), lambda hp, m: (0, 0, n_hp + hp)),
        pl.BlockSpec((nseq, LAT_LEN, LANES), lambda hp, m: (0, 0, 2 * n_hp + hp)),
        pl.BlockSpec((None, 2, nseq, PAST_LEN, LANES), lambda hp, m: (layer, 0, 0, 0, hp)),
        _resident(rp.shape),
    ]
    out_specs = [pl.BlockSpec((nseq, Q_BLOCK, LANES), lambda hp, m: (0, m, hp))]
    out_shape = [jax.ShapeDtypeStruct((nseq, LAT_LEN, ATTN_W), BF16)]
    if kv_layers:
        n_ctx, *slab = kv_layers[0].shape
        assert n_ctx == n_hp * N_Q_BLOCKS
        step = lambda hp, m: hp * N_Q_BLOCKS + m
        in_specs += [pl.BlockSpec((1, *slab), lambda hp, m: (step(hp, m), 0, 0, 0, 0))] * len(kv_layers)
        out_specs.append(pl.BlockSpec((1, len(kv_layers), *slab), lambda hp, m: (step(hp, m), 0, 0, 0, 0, 0)))
        out_shape.append(jax.ShapeDtypeStruct((n_ctx, len(kv_layers), *slab), F32))
    outs = pl.pallas_call(
        functools.partial(_lat_attn_kernel, n_kv=len(kv_layers)),
        grid=(n_hp, N_Q_BLOCKS),
        in_specs=in_specs,
        out_specs=out_specs,
        out_shape=out_shape,
        scratch_shapes=[pltpu.VMEM((N_HEADS, N_Q_BLOCKS, Q_BLOCK, N_KEYS), F32)],
        compiler_params=_params("arbitrary", "arbitrary"),
        name="attn_lat",
    )(qkv, qkv, qkv, cache_kv, rp, *kv_layers)
    y = outs[0].reshape(nseq * LAT_LEN, ATTN_W)
    return (y, outs[1]) if kv_layers else y


def _row_pair_table(rpb_l):
    lane_pad = GRID_W - rpb_l.shape[-1]
    lo = jnp.pad(rpb_l, ((0, 0), (1, 0), (0, lane_pad)))
    hi = jnp.pad(rpb_l, ((0, 0), (0, 1), (0, lane_pad)))
    return jnp.concatenate([lo, hi], axis=-1)


def _mix_kernel(x_ref, mod_ref, yc_ref, ya_ref, d_ref, gates_ref, pw_ref, ps_ref, wb_ref, wo_ref, o_ref):
    yp = [jnp.dot(d_ref[:, g * POOL_GROUP:(g + 1) * POOL_GROUP], pw_ref[g], preferred_element_type=F32)
          for g in range(len(POOL_SIZES))]
    y_pool = (jnp.concatenate(yp, axis=-1) * ps_ref[...]).astype(BF16)
    merged = None
    for i, y in enumerate((yc_ref[...], ya_ref[...], y_pool)):
        proj = jnp.dot(y, wb_ref[i], preferred_element_type=F32)
        term = gates_ref[:, i * D_MODEL:(i + 1) * D_MODEL].astype(F32) * proj
        merged = term if merged is None else merged + term
    mix = jnp.dot(merged.astype(BF16), wo_ref[...], preferred_element_type=F32)
    o_ref[...] = x_ref[...] + mod_ref[0, 2:3, :] * mix


def _mix(x, mod, yconv, yattn, d, gates, pool_w, pool_scale, w_branch, w_out, *, layer):
    ntok = x.shape[0]
    row = lambda i: (i, 0)
    return pl.pallas_call(
        _mix_kernel,
        grid=(ntok // MIX_TILE,),
        in_specs=[
            pl.BlockSpec((MIX_TILE, D_MODEL), row),
            _mod_spec(mod, ntok, MIX_TILE),
            pl.BlockSpec((MIX_TILE, CONV_W), row),
            pl.BlockSpec((MIX_TILE, ATTN_W), row),
            pl.BlockSpec((MIX_TILE, POOL_W), row),
            pl.BlockSpec((MIX_TILE, N_BRANCH * D_MODEL), row),
            _resident_layer((len(POOL_SIZES), POOL_GROUP, POOL_GROUP), layer),
            _resident((1, POOL_W)),
            _resident_layer((N_BRANCH, CONV_W, D_MODEL), layer),
            _resident_layer((D_MODEL, D_MODEL), layer),
        ],
        out_specs=pl.BlockSpec((MIX_TILE, D_MODEL), row),
        out_shape=jax.ShapeDtypeStruct((ntok, D_MODEL), F32),
        compiler_params=_params("arbitrary"),
        name="mix",
    )(x, mod, yconv, yattn, d, gates, pool_w, pool_scale, w_branch, w_out)


def _ffn_kernel(x_ref, mod_ref, g2_ref, wu_ref, fc_ref, wd_ref, gf_ref, o_ref, h_scr, act_scr,
                *, seq_len, final):
    u0, val0 = [], []
    for r in range(x_ref.shape[0] // NORM_ROWS):
        rows = slice(r * NORM_ROWS, (r + 1) * NORM_ROWS)
        h = _norm_mod(x_ref[rows, :], g2_ref[...], mod_ref[0, 3:4, :], mod_ref[0, 4:5, :]).astype(BF16)
        h_scr[rows, :] = h
        u0.append(jnp.dot(h, wu_ref[:, 0:FF_CHUNK], preferred_element_type=F32))
        val0.append(jnp.dot(h, wu_ref[:, D_FF:D_FF + FF_CHUNK], preferred_element_type=F32))
    row = _row_in_seq((x_ref.shape[0], FF_CHUNK), seq_len)
    for c in range(N_FF_CHUNKS):
        sl = slice(c * FF_CHUNK, (c + 1) * FF_CHUNK)
        sl_val = slice(D_FF + c * FF_CHUNK, D_FF + (c + 1) * FF_CHUNK)
        if c == 0:
            u, val = jnp.concatenate(u0, axis=0), jnp.concatenate(val0, axis=0)
        else:
            u = jnp.dot(h_scr[...], wu_ref[:, sl], preferred_element_type=F32)
            val = jnp.dot(h_scr[...], wu_ref[:, sl_val], preferred_element_type=F32)
        u = _dwconv3(u, fc_ref.at[:, sl], row, seq_len)
        act_scr[:, sl] = (jax.nn.gelu(u, approximate=True) * val).astype(BF16)
    y = jnp.dot(act_scr[...], wd_ref[...], preferred_element_type=F32)
    xn = x_ref[...] + mod_ref[0, 5:6, :] * y
    if final:
        ms = jnp.mean(xn * xn, axis=-1, keepdims=True)
        xn = xn * lax.rsqrt(ms + EPS) * gf_ref[...]
    o_ref[...] = xn


def _ffn(x, mod, g2, w_up, f_conv, w_down, g_final, *, seq_len, final, layer):
    ntok = x.shape[0]
    return pl.pallas_call(
        functools.partial(_ffn_kernel, seq_len=seq_len, final=final),
        grid=(ntok // TOK_TILE,),
        in_specs=[
            pl.BlockSpec((TOK_TILE, D_MODEL), lambda i: (i, 0)),
            _mod_spec(mod, ntok, TOK_TILE),
            _resident((1, D_MODEL)),
            _resident_layer((D_MODEL, 2 * D_FF), layer),
            _resident((3, D_FF)),
            _resident_layer((D_FF, D_MODEL), layer),
            _resident((1, D_MODEL)),
        ],
        out_specs=pl.BlockSpec((TOK_TILE, D_MODEL), lambda i: (i, 0)),
        out_shape=jax.ShapeDtypeStruct((ntok, D_MODEL), F32),
        scratch_shapes=[pltpu.VMEM((TOK_TILE, D_MODEL), BF16), pltpu.VMEM((TOK_TILE, D_FF), BF16)],
        compiler_params=_params("arbitrary"),
        name="ffn",
    )(x, mod, g2, w_up, f_conv, w_down, g_final)


def kernel(x_prompt, x_sample, cache_kv, c, c_ctx, w_mod, b_mod, g_norm1, g_norm2, w_in, conv_w, rpb,
           pool_w, pool_scale, w_branch, w_out, ffn_w_up, ffn_conv, ffn_w_down, g_final):
    n_mod = 8
    cvec = jnp.concatenate([c_ctx[None, :], c, jnp.zeros((n_mod - 1 - N_LAT_SEQ, D_MODEL), F32)], axis=0)
    mod = _adaln(cvec, w_mod, b_mod).reshape(DEPTH, n_mod, 6, D_MODEL)

    xp = x_prompt.reshape(N_CTX_SEQ * CTX_LEN, D_MODEL)
    xs = x_sample.reshape(N_LAT_SEQ * LAT_LEN, D_MODEL)
    gf = g_final.reshape(1, D_MODEL)
    cache_tok = cache_kv.transpose(1, 2, 0, 4, 3, 5).reshape(DEPTH, 2, N_LAT_SEQ, PAST_LEN, ATTN_W).astype(BF16)
    w_in_b, pool_w_b, w_br_b, w_out_b, w_up_b, w_dn_b = _cast_weights(
        w_in, pool_w, w_branch, w_out, ffn_w_up, ffn_w_down)
    kv_layers = []
    kv_state = None
    for l in range(DEPTH):
        pool_s_l = pool_scale[l].reshape(1, POOL_W)
        g1 = g_norm1[l].reshape(1, D_MODEL)
        g2 = g_norm2[l].reshape(1, D_MODEL)
        mod_ctx = mod[l, 0:1]
        mod_lat = mod[l, 1:1 + N_LAT_SEQ]
        final = l == DEPTH - 1

        yc, qkv, d, gates, kv = _inproj(xp, mod_ctx, g1, w_in_b, conv_w[l], seq_len=CTX_LEN, tile=CTX_IN_TILE,
                                        with_kv=True, layer=l)
        kv_layers.append(kv)
        ya = _ctx_attention(qkv)
        xp = _mix(xp, mod_ctx, yc, ya, d, gates, pool_w_b, pool_s_l, w_br_b, w_out_b, layer=l)
        xp = _ffn(xp, mod_ctx, g2, w_up_b, ffn_conv[l], w_dn_b, gf, seq_len=CTX_LEN, final=final, layer=l)

        yc, qkv, d, gates = _inproj(xs, mod_lat, g1, w_in_b, conv_w[l], seq_len=LAT_LEN, tile=TOK_TILE,
                                    with_kv=False, layer=l)
        if final:
            ya, kv_state = _lat_attention(qkv, cache_tok, l, _row_pair_table(rpb[l]), tuple(kv_layers))
        else:
            ya = _lat_attention(qkv, cache_tok, l, _row_pair_table(rpb[l]))
        xs = _mix(xs, mod_lat, yc, ya, d, gates, pool_w_b, pool_s_l, w_br_b, w_out_b, layer=l)
        xs = _ffn(xs, mod_lat, g2, w_up_b, ffn_conv[l], w_dn_b, gf, seq_len=LAT_LEN, final=final, layer=l)

    y_prompt = xp.reshape(N_CTX_SEQ, CTX_LEN, D_MODEL)
    y_sample = xs.reshape(N_LAT_SEQ, LAT_LEN, D_MODEL)
    return (y_prompt, y_sample, kv_state)
```

```python
import functools

import numpy as np
import jax
import jax.numpy as jnp
from jax import lax
from jax.experimental import pallas as pl
from jax.experimental.pallas import tpu as pltpu

F32 = jnp.float32
BF16 = jnp.bfloat16

D_MODEL = 1024
N_CTX_SEQ = 16
CTX_LEN = 256
DEPTH = 2
N_LAT_SEQ = 4
LAT_LEN = 1024
PAST_LEN = 256
GRID_W = 64
GRID_ROWS = LAT_LEN // GRID_W
CONV_W = 512
N_HEADS = 8
HEAD_DIM = 64
ATTN_W = N_HEADS * HEAD_DIM
POOL_W = 512
POOL_SIZES = (2, 4, 8, 16)
POOL_GROUP = POOL_W // len(POOL_SIZES)
N_BRANCH = 3
WIN_H = 8
WIN_W = 16
D_FF = 2816
EPS = 1e-6
IN_W = 3 * CONV_W + 3 * ATTN_W + POOL_W + N_BRANCH * D_MODEL

LANES = 128
TOK_TILE = 1024
CTX_IN_TILE = 512
IN_CHUNK = 512
FF_CHUNK = 256
N_FF_CHUNKS = D_FF // FF_CHUNK
MIX_TILE = 1024
NORM_ROWS = 256
VMEM_LIMIT = 56 * 1024 * 1024


def _params(*sem):
    return pltpu.CompilerParams(dimension_semantics=sem, vmem_limit_bytes=VMEM_LIMIT)


def _resident(shape):
    return pl.BlockSpec(shape, lambda *_: (0,) * len(shape), pipeline_mode=pl.Buffered(1))


def _resident_layer(shape, layer):
    return pl.BlockSpec((None,) + shape, lambda *_: (layer,) + (0,) * len(shape), pipeline_mode=pl.Buffered(1))


def _mod_spec(mod, ntok, tile):
    tiles_per_mod = ntok // mod.shape[0] // tile
    return pl.BlockSpec((1, 6, D_MODEL), lambda i, *_: (i // tiles_per_mod, 0, 0))


def _norm_mod(x, g, shift, scale):
    ms = jnp.mean(x * x, axis=-1, keepdims=True)
    return (x * lax.rsqrt(ms + EPS) * g) * (1.0 + scale) + shift


def _row_in_seq(shape, seq_len):
    return lax.broadcasted_iota(jnp.int32, shape, 0) & (seq_len - 1)


def _shift_rows(x, s, row, seq_len):
    y = pltpu.roll(x, s % x.shape[0], 0)
    ok = (row >= s) if s > 0 else (row < seq_len + s)
    return jnp.where(ok, y, 0.0)


def _dwconv3(x, w_ref, row, seq_len):
    return (_shift_rows(x, 1, row, seq_len) * w_ref[0:1, :] + x * w_ref[1:2, :]
            + _shift_rows(x, -1, row, seq_len) * w_ref[2:3, :])


def _pool_inv_counts(tile, seq_len):
    r = np.arange(tile) % seq_len
    cols = []
    for w in POOL_SIZES:
        lo = np.maximum(r - w // 2, 0)
        hi = np.minimum(r - w // 2 + w, seq_len)
        cols.append(1.0 / (hi - lo))
    return jnp.asarray(np.stack(cols, axis=1), F32)


def _pool_delta(pg, w, inv_cnt, seq_len):
    row = _row_in_seq(pg.shape, seq_len)
    fwd = bwd = pg
    k = 1
    while k < w // 2:
        fwd = fwd + _shift_rows(fwd, -k, row, seq_len)
        bwd = bwd + _shift_rows(bwd, k, row, seq_len)
        k *= 2
    acc = fwd + _shift_rows(bwd, 1, row, seq_len)
    return acc * inv_cnt - pg


CAST_STEPS = 16


def _cast_kernel(*refs):
    n = len(refs) // 2
    for src, dst in zip(refs[:n], refs[n:]):
        dst[...] = src[...].astype(BF16)


def _cast_weights(*weights):
    flat = [w.reshape(-1, w.shape[-1]) for w in weights]
    specs = [pl.BlockSpec((w.shape[0] // CAST_STEPS, w.shape[1]), lambda i: (i, 0)) for w in flat]
    outs = pl.pallas_call(
        _cast_kernel,
        grid=(CAST_STEPS,),
        in_specs=specs,
        out_specs=specs,
        out_shape=[jax.ShapeDtypeStruct(w.shape, BF16) for w in flat],
        compiler_params=_params("arbitrary"),
        name="cast_weights",
    )(*flat)
    return [o.reshape(w.shape) for o, w in zip(outs, weights)]


def _adaln_kernel(c_ref, w_ref, b_ref, o_ref):
    c = c_ref[...]
    s = c * jax.nn.sigmoid(c)
    o_ref[0] = jnp.dot(s.astype(BF16), w_ref[0].astype(BF16), preferred_element_type=F32) + b_ref[0]


def _adaln(cvec, w_mod, b_mod):
    nb = cvec.shape[0]
    n_out = 6 * D_MODEL
    blk = 2048
    return pl.pallas_call(
        _adaln_kernel,
        grid=(DEPTH, n_out // blk),
        in_specs=[
            pl.BlockSpec((nb, D_MODEL), lambda l, n: (0, 0)),
            pl.BlockSpec((1, D_MODEL, blk), lambda l, n: (l, 0, n)),
            pl.BlockSpec((1, 1, blk), lambda l, n: (l, 0, n)),
        ],
        out_specs=pl.BlockSpec((1, nb, blk), lambda l, n: (l, 0, n)),
        out_shape=jax.ShapeDtypeStruct((DEPTH, nb, n_out), F32),
        compiler_params=_params("arbitrary", "arbitrary"),
        name="adaln",
    )(cvec, w_mod, b_mod.reshape(DEPTH, 1, n_out))


COL_CONV = 0
COL_QKV = 3 * CONV_W
COL_POOL = COL_QKV + 3 * ATTN_W
COL_GATE = COL_POOL + POOL_W
N_GATE_COLS = N_BRANCH * D_MODEL


def _inproj_kernel(x_ref, mod_ref, g_ref, w_ref, cw_ref, inv_ref, yconv_ref, qkv_ref, d_ref, gates_ref, *rest,
                   seq_len, with_kv):
    if with_kv:
        kv_ref, h_scr = rest
    else:
        (h_scr,) = rest

    def proj(col):
        return jnp.dot(h_scr[...], w_ref[:, col:col + IN_CHUNK], preferred_element_type=F32)

    parts = []
    for r in range(x_ref.shape[0] // NORM_ROWS):
        rows = slice(r * NORM_ROWS, (r + 1) * NORM_ROWS)
        h = _norm_mod(x_ref[rows, :], g_ref[...], mod_ref[0, 0:1, :], mod_ref[0, 1:2, :]).astype(BF16)
        h_scr[rows, :] = h
        parts.append(jnp.dot(h, w_ref[:, COL_CONV:COL_CONV + IN_CHUNK], preferred_element_type=F32))
    b_gate = jnp.concatenate(parts, axis=0)
    row = _row_in_seq((x_ref.shape[0], IN_CHUNK), seq_len)
    conv = _dwconv3(proj(COL_CONV + CONV_W) * proj(COL_CONV + 2 * CONV_W), cw_ref, row, seq_len)
    yconv_ref[...] = (b_gate * conv).astype(BF16)
    p = proj(COL_POOL)
    for c in range(N_GATE_COLS // IN_CHUNK):
        z = proj(COL_GATE + c * IN_CHUNK)
        if c < len(POOL_SIZES):
            lanes = slice(c * POOL_GROUP, (c + 1) * POOL_GROUP)
            d_ref[:, lanes] = _pool_delta(p[:, lanes], POOL_SIZES[c], inv_ref[:, c:c + 1], seq_len).astype(BF16)
        gates_ref[:, c * IN_CHUNK:(c + 1) * IN_CHUNK] = jax.nn.sigmoid(z).astype(BF16)
    for kv in range(2):
        z = proj(COL_QKV + (1 + kv) * ATTN_W)
        qkv_ref[:, (1 + kv) * ATTN_W:(2 + kv) * ATTN_W] = z.astype(BF16)
        if with_kv:
            for s in range(z.shape[0] // seq_len):
                for h_i in range(N_HEADS):
                    kv_ref[s, kv, h_i] = z[s * seq_len:(s + 1) * seq_len, h_i * HEAD_DIM:(h_i + 1) * HEAD_DIM]
    qkv_ref[:, 0:ATTN_W] = (proj(COL_QKV) * (HEAD_DIM ** -0.5)).astype(BF16)


def _inproj(x, mod, g1, w_in, conv_w, *, seq_len, tile, with_kv, layer):
    ntok = x.shape[0]
    row = lambda i: (i, 0)
    widths = (CONV_W, 3 * ATTN_W, POOL_W, N_GATE_COLS)
    out_shape = [jax.ShapeDtypeStruct((ntok, w), BF16) for w in widths]
    out_specs = [pl.BlockSpec((tile, w), row) for w in widths]
    if with_kv:
        one_layer = (2, N_HEADS, seq_len, HEAD_DIM)
        out_shape.append(jax.ShapeDtypeStruct((ntok // seq_len,) + one_layer, F32))
        out_specs.append(pl.BlockSpec((tile // seq_len,) + one_layer, lambda i: (i, 0, 0, 0, 0)))
    return pl.pallas_call(
        functools.partial(_inproj_kernel, seq_len=seq_len, with_kv=with_kv),
        grid=(ntok // tile,),
        in_specs=[
            pl.BlockSpec((tile, D_MODEL), row),
            _mod_spec(mod, ntok, tile),
            _resident((1, D_MODEL)),
            _resident_layer((D_MODEL, IN_W), layer),
            _resident((3, CONV_W)),
            _resident((tile, len(POOL_SIZES))),
        ],
        out_specs=out_specs,
        out_shape=out_shape,
        scratch_shapes=[pltpu.VMEM((tile, D_MODEL), BF16)],
        compiler_params=_params("arbitrary"),
        name="inproj_ctx" if with_kv else "inproj_lat",
    )(x, mod, g1, w_in, conv_w, _pool_inv_counts(tile, seq_len))


def _attend_pair(q2, kvb):
    lane = lax.broadcasted_iota(jnp.int32, (1, LANES), 1)
    out = None
    for half in range(2):
        m = (lane >= half * HEAD_DIM) & (lane < (half + 1) * HEAD_DIM)
        qh = jnp.where(m, q2, jnp.zeros_like(q2))
        scores = []
        for k2, _, bias in kvb:
            s = lax.dot_general(qh, k2, (((1,), (1,)), ((), ())), preferred_element_type=F32)
            if bias is not None:
                s = s + bias[half]
            scores.append(s)
        mx = functools.reduce(jnp.maximum, [jnp.max(s, axis=-1, keepdims=True) for s in scores])
        es = [jnp.exp(s - mx) for s in scores]
        den = functools.reduce(lambda a, b: a + b, [jnp.sum(e, axis=-1, keepdims=True) for e in es])
        acc = None
        for e, (_, v2, _) in zip(es, kvb):
            vh = jnp.where(m, v2, jnp.zeros_like(v2))
            o = jnp.dot(e.astype(BF16), vh, preferred_element_type=F32)
            acc = o if acc is None else acc + o
        acc = acc * (1.0 / den)
        out = acc if out is None else out + acc
    return out


CTX_ATTN_SEQS = 4


def _ctx_attn_kernel(q_ref, k_ref, v_ref, o_ref):
    for s in range(q_ref.shape[0] // CTX_LEN):
        rows = slice(s * CTX_LEN, (s + 1) * CTX_LEN)
        for hp in range(N_HEADS // 2):
            sl = slice(hp * LANES, (hp + 1) * LANES)
            o = _attend_pair(q_ref[rows, sl], [(k_ref[rows, sl], v_ref[rows, sl], None)])
            o_ref[rows, sl] = o.astype(BF16)


def _ctx_attention(qkv):
    ntok = qkv.shape[0]
    blk = CTX_ATTN_SEQS * CTX_LEN
    return pl.pallas_call(
        _ctx_attn_kernel,
        grid=(ntok // blk,),
        in_specs=[
            pl.BlockSpec((blk, ATTN_W), lambda b: (b, 0)),
            pl.BlockSpec((blk, ATTN_W), lambda b: (b, 1)),
            pl.BlockSpec((blk, ATTN_W), lambda b: (b, 2)),
        ],
        out_specs=pl.BlockSpec((blk, ATTN_W), lambda b: (b, 0)),
        out_shape=jax.ShapeDtypeStruct((ntok, ATTN_W), BF16),
        compiler_params=_params("arbitrary"),
        name="attn_ctx",
    )(qkv, qkv, qkv)


Q_ROWS = 4
Q_BLOCK = Q_ROWS * GRID_W
N_Q_BLOCKS = GRID_ROWS // Q_ROWS
KEY_ROWS = 12
N_KEYS = KEY_ROWS * GRID_W
N_REL_ROWS = 2 * WIN_H - 1
N_ROW_PAIRS = N_REL_ROWS + 1


def _first_key_row(m):
    return min(max(Q_ROWS * m - WIN_H // 2, 0), GRID_ROWS - KEY_ROWS)


def _fill_bias(rp_ref, bias_scr):
    shape = (N_ROW_PAIRS * GRID_W, LANES)
    qc = lax.broadcasted_iota(jnp.int32, shape, 0) & (GRID_W - 1)
    lane = lax.broadcasted_iota(jnp.int32, shape, 1)
    kc = lane & (GRID_W - 1)
    col_start = jnp.clip(qc - WIN_W // 2, 0, GRID_W - WIN_W)
    valid = (kc >= col_start) & (kc < col_start + WIN_W)
    first_half = lax.broadcasted_iota(jnp.int32, (GRID_W, LANES), 1) < GRID_W
    for h in range(N_HEADS):
        t = jnp.broadcast_to(rp_ref[h][:, None, :], (N_ROW_PAIRS, GRID_W, LANES)).reshape(shape)
        t = pltpu.roll(t, LANES - (WIN_W - 1), 1)
        for bit in range(6):
            t = jnp.where(((qc >> bit) & 1) == 1, pltpu.roll(t, 1 << bit, 1), t)
        t = jnp.where(valid, t, -jnp.inf)
        for m in range(N_Q_BLOCKS):
            for ql in range(Q_ROWS):
                qr = Q_ROWS * m + ql
                win0 = min(max(qr - WIN_H // 2, 0), GRID_ROWS - WIN_H)
                for u in range(KEY_ROWS // 2):
                    kr = _first_key_row(m) + 2 * u
                    in0 = win0 <= kr < win0 + WIN_H
                    in1 = win0 <= kr + 1 < win0 + WIN_H
                    if in0 or in1:
                        pair = kr - qr + WIN_H
                        tile = t[pair * GRID_W:(pair + 1) * GRID_W]
                        if not in0:
                            tile = jnp.where(first_half, -jnp.inf, tile)
                        if not in1:
                            tile = jnp.where(first_half, tile, -jnp.inf)
                    else:
                        tile = jnp.full((GRID_W, LANES), -jnp.inf, F32)
                    bias_scr[h, m, ql * GRID_W:(ql + 1) * GRID_W, u * LANES:(u + 1) * LANES] = tile


def _lat_attn_kernel(q_ref, k_ref, v_ref, cache_ref, rp_ref, *rest, n_kv):
    kv_in, o_ref, bias_scr = rest[:n_kv], rest[n_kv], rest[-1]
    if n_kv:
        kv_out = rest[n_kv + 1]
        for layer, src in enumerate(kv_in):
            kv_out[0, layer] = src[0]
    hp = pl.program_id(0)
    m = pl.program_id(1)

    @pl.when((hp == 0) & (m == 0))
    def _():
        _fill_bias(rp_ref, bias_scr)

    def attend(m_idx, first_key, n_rows, col0=0):
        keys = pl.ds(first_key, n_rows * GRID_W)
        cols = slice(col0, col0 + n_rows * GRID_W)
        for s in range(q_ref.shape[0]):
            kvb = [
                (k_ref[s, keys, :], v_ref[s, keys, :],
                 (bias_scr[2 * hp, m_idx, :, cols], bias_scr[2 * hp + 1, m_idx, :, cols])),
                (cache_ref[0, s], cache_ref[1, s], None),
            ]
            o_ref[s] = _attend_pair(q_ref[s], kvb).astype(BF16)

    last = N_Q_BLOCKS - 1
    last_row0 = GRID_ROWS - WIN_H
    pl.when(m == 0)(lambda: attend(0, 0, WIN_H))
    pl.when(m == last)(lambda: attend(last, last_row0 * GRID_W, WIN_H,
                                      col0=(last_row0 - _first_key_row(last)) * GRID_W))

    @pl.when((m > 0) & (m < last))
    def _():
        row0 = jnp.clip(Q_ROWS * m - WIN_H // 2, 0, GRID_ROWS - KEY_ROWS)
        attend(m, pl.multiple_of(row0 * GRID_W, Q_BLOCK), KEY_ROWS)


def _lat_attention(qkv, cache_kv, layer, rp, kv_layers=()):
    nseq = qkv.shape[0] // LAT_LEN
    qkv = qkv.reshape(nseq, LAT_LEN, 3 * ATTN_W)
    n_hp = ATTN_W // LANES
    in_specs = [
        pl.BlockSpec((nseq, Q_BLOCK, LANES), lambda hp, m: (0, m, hp)),
        pl.BlockSpec((nseq, LAT_LEN, LANES), lambda hp, m: (0, 0, n_hp + hp)),
        pl.BlockSpec((nseq, LAT_LEN, LANES), lambda hp, m: (0, 0, 2 * n_hp + hp)),
        pl.BlockSpec((None, 2, nseq, PAST_LEN, LANES), lambda hp, m: (layer, 0, 0, 0, hp)),
        _resident(rp.shape),
    ]
    out_specs = [pl.BlockSpec((nseq, Q_BLOCK, LANES), lambda hp, m: (0, m, hp))]
    out_shape = [jax.ShapeDtypeStruct((nseq, LAT_LEN, ATTN_W), BF16)]
    if kv_layers:
        n_ctx, *slab = kv_layers[0].shape
        assert n_ctx == n_hp * N_Q_BLOCKS
        step = lambda hp, m: hp * N_Q_BLOCKS + m
        in_specs += [pl.BlockSpec((1, *slab), lambda hp, m: (step(hp, m), 0, 0, 0, 0))] * len(kv_layers)
        out_specs.append(pl.BlockSpec((1, len(kv_layers), *slab), lambda hp, m: (step(hp, m), 0, 0, 0, 0, 0)))
        out_shape.append(jax.ShapeDtypeStruct((n_ctx, len(kv_layers), *slab), F32))
    outs = pl.pallas_call(
        functools.partial(_lat_attn_kernel, n_kv=len(kv_layers)),
        grid=(n_hp, N_Q_BLOCKS),
        in_specs=in_specs,
        out_specs=out_specs,
        out_shape=out_shape,
        scratch_shapes=[pltpu.VMEM((N_HEADS, N_Q_BLOCKS, Q_BLOCK, N_KEYS), F32)],
        compiler_params=_params("arbitrary", "arbitrary"),
        name="attn_lat",
    )(qkv, qkv, qkv, cache_kv, rp, *kv_layers)
    y = outs[0].reshape(nseq * LAT_LEN, ATTN_W)
    return (y, outs[1]) if kv_layers else y


def _row_pair_table(rpb_l):
    lane_pad = GRID_W - rpb_l.shape[-1]
    lo = jnp.pad(rpb_l, ((0, 0), (1, 0), (0, lane_pad)))
    hi = jnp.pad(rpb_l, ((0, 0), (0, 1), (0, lane_pad)))
    return jnp.concatenate([lo, hi], axis=-1)


def _mix_kernel(x_ref, mod_ref, yc_ref, ya_ref, d_ref, gates_ref, pw_ref, ps_ref, wb_ref, wo_ref, o_ref):
    yp = [jnp.dot(d_ref[:, g * POOL_GROUP:(g + 1) * POOL_GROUP], pw_ref[g], preferred_element_type=F32)
          for g in range(len(POOL_SIZES))]
    y_pool = (jnp.concatenate(yp, axis=-1) * ps_ref[...]).astype(BF16)
    merged = None
    for i, y in enumerate((yc_ref[...], ya_ref[...], y_pool)):
        proj = jnp.dot(y, wb_ref[i], preferred_element_type=F32)
        term = gates_ref[:, i * D_MODEL:(i + 1) * D_MODEL].astype(F32) * proj
        merged = term if merged is None else merged + term
    mix = jnp.dot(merged.astype(BF16), wo_ref[...], preferred_element_type=F32)
    o_ref[...] = x_ref[...] + mod_ref[0, 2:3, :] * mix


def _mix(x, mod, yconv, yattn, d, gates, pool_w, pool_scale, w_branch, w_out, *, layer):
    ntok = x.shape[0]
    row = lambda i: (i, 0)
    return pl.pallas_call(
        _mix_kernel,
        grid=(ntok // MIX_TILE,),
        in_specs=[
            pl.BlockSpec((MIX_TILE, D_MODEL), row),
            _mod_spec(mod, ntok, MIX_TILE),
            pl.BlockSpec((MIX_TILE, CONV_W), row),
            pl.BlockSpec((MIX_TILE, ATTN_W), row),
            pl.BlockSpec((MIX_TILE, POOL_W), row),
            pl.BlockSpec((MIX_TILE, N_BRANCH * D_MODEL), row),
            _resident_layer((len(POOL_SIZES), POOL_GROUP, POOL_GROUP), layer),
            _resident((1, POOL_W)),
            _resident_layer((N_BRANCH, CONV_W, D_MODEL), layer),
            _resident_layer((D_MODEL, D_MODEL), layer),
        ],
        out_specs=pl.BlockSpec((MIX_TILE, D_MODEL), row),
        out_shape=jax.ShapeDtypeStruct((ntok, D_MODEL), F32),
        compiler_params=_params("arbitrary"),
        name="mix",
    )(x, mod, yconv, yattn, d, gates, pool_w, pool_scale, w_branch, w_out)


def _ffn_kernel(x_ref, mod_ref, g2_ref, wu_ref, fc_ref, wd_ref, gf_ref, o_ref, h_scr, act_scr,
                *, seq_len, final):
    u0, val0 = [], []
    for r in range(x_ref.shape[0] // NORM_ROWS):
        rows = slice(r * NORM_ROWS, (r + 1) * NORM_ROWS)
        h = _norm_mod(x_ref[rows, :], g2_ref[...], mod_ref[0, 3:4, :], mod_ref[0, 4:5, :]).astype(BF16)
        h_scr[rows, :] = h
        u0.append(jnp.dot(h, wu_ref[:, 0:FF_CHUNK], preferred_element_type=F32))
        val0.append(jnp.dot(h, wu_ref[:, D_FF:D_FF + FF_CHUNK], preferred_element_type=F32))
    row = _row_in_seq((x_ref.shape[0], FF_CHUNK), seq_len)
    for c in range(N_FF_CHUNKS):
        sl = slice(c * FF_CHUNK, (c + 1) * FF_CHUNK)
        sl_val = slice(D_FF + c * FF_CHUNK, D_FF + (c + 1) * FF_CHUNK)
        if c == 0:
            u, val = jnp.concatenate(u0, axis=0), jnp.concatenate(val0, axis=0)
        else:
            u = jnp.dot(h_scr[...], wu_ref[:, sl], preferred_element_type=F32)
            val = jnp.dot(h_scr[...], wu_ref[:, sl_val], preferred_element_type=F32)
        u = _dwconv3(u, fc_ref.at[:, sl], row, seq_len)
        act_scr[:, sl] = (jax.nn.gelu(u, approximate=True) * val).astype(BF16)
    y = jnp.dot(act_scr[...], wd_ref[...], preferred_element_type=F32)
    xn = x_ref[...] + mod_ref[0, 5:6, :] * y
    if final:
        ms = jnp.mean(xn * xn, axis=-1, keepdims=True)
        xn = xn * lax.rsqrt(ms + EPS) * gf_ref[...]
    o_ref[...] = xn


def _ffn(x, mod, g2, w_up, f_conv, w_down, g_final, *, seq_len, final, layer):
    ntok = x.shape[0]
    return pl.pallas_call(
        functools.partial(_ffn_kernel, seq_len=seq_len, final=final),
        grid=(ntok // TOK_TILE,),
        in_specs=[
            pl.BlockSpec((TOK_TILE, D_MODEL), lambda i: (i, 0)),
            _mod_spec(mod, ntok, TOK_TILE),
            _resident((1, D_MODEL)),
            _resident_layer((D_MODEL, 2 * D_FF), layer),
            _resident((3, D_FF)),
            _resident_layer((D_FF, D_MODEL), layer),
            _resident((1, D_MODEL)),
        ],
        out_specs=pl.BlockSpec((TOK_TILE, D_MODEL), lambda i: (i, 0)),
        out_shape=jax.ShapeDtypeStruct((ntok, D_MODEL), F32),
        scratch_shapes=[pltpu.VMEM((TOK_TILE, D_MODEL), BF16), pltpu.VMEM((TOK_TILE, D_FF), BF16)],
        compiler_params=_params("arbitrary"),
        name="ffn",
    )(x, mod, g2, w_up, f_conv, w_down, g_final)


def kernel(x_prompt, x_sample, cache_kv, c, c_ctx, w_mod, b_mod, g_norm1, g_norm2, w_in, conv_w, rpb,
           pool_w, pool_scale, w_branch, w_out, ffn_w_up, ffn_conv, ffn_w_down, g_final):
    n_mod = 8
    cvec = jnp.concatenate([c_ctx[None, :], c, jnp.zeros((n_mod - 1 - N_LAT_SEQ, D_MODEL), F32)], axis=0)
    mod = _adaln(cvec, w_mod, b_mod).reshape(DEPTH, n_mod, 6, D_MODEL)

    xp = x_prompt.reshape(N_CTX_SEQ * CTX_LEN, D_MODEL)
    xs = x_sample.reshape(N_LAT_SEQ * LAT_LEN, D_MODEL)
    gf = g_final.reshape(1, D_MODEL)
    cache_tok = cache_kv.transpose(1, 2, 0, 4, 3, 5).reshape(DEPTH, 2, N_LAT_SEQ, PAST_LEN, ATTN_W).astype(BF16)
    w_in_b, pool_w_b, w_br_b, w_out_b, w_up_b, w_dn_b = _cast_weights(
        w_in, pool_w, w_branch, w_out, ffn_w_up, ffn_w_down)
    kv_layers = []
    kv_state = None
    for l in range(DEPTH):
        pool_s_l = pool_scale[l].reshape(1, POOL_W)
        g1 = g_norm1[l].reshape(1, D_MODEL)
        g2 = g_norm2[l].reshape(1, D_MODEL)
        mod_ctx = mod[l, 0:1]
        mod_lat = mod[l, 1:1 + N_LAT_SEQ]
        final = l == DEPTH - 1

        yc, qkv, d, gates, kv = _inproj(xp, mod_ctx, g1, w_in_b, conv_w[l], seq_len=CTX_LEN, tile=CTX_IN_TILE,
                                        with_kv=True, layer=l)
        kv_layers.append(kv)
        ya = _ctx_attention(qkv)
        xp = _mix(xp, mod_ctx, yc, ya, d, gates, pool_w_b, pool_s_l, w_br_b, w_out_b, layer=l)
        xp = _ffn(xp, mod_ctx, g2, w_up_b, ffn_conv[l], w_dn_b, gf, seq_len=CTX_LEN, final=final, layer=l)

        yc, qkv, d, gates = _inproj(xs, mod_lat, g1, w_in_b, conv_w[l], seq_len=LAT_LEN, tile=TOK_TILE,
                                    with_kv=False, layer=l)
        if final:
            ya, kv_state = _lat_attention(qkv, cache_tok, l, _row_pair_table(rpb[l]), tuple(kv_layers))
        else:
            ya = _lat_attention(qkv, cache_tok, l, _row_pair_table(rpb[l]))
        xs = _mix(xs, mod_lat, yc, ya, d, gates, pool_w_b, pool_s_l, w_br_b, w_out_b, layer=l)
        xs = _ffn(xs, mod_lat, g2, w_up_b, ffn_conv[l], w_dn_b, gf, seq_len=LAT_LEN, final=final, layer=l)

    y_prompt = xp.reshape(N_CTX_SEQ, CTX_LEN, D_MODEL)
    y_sample = xs.reshape(N_LAT_SEQ, LAT_LEN, D_MODEL)
    return (y_prompt, y_sample, kv_state)
```

```python
import functools

import numpy as np
import jax
import jax.numpy as jnp
from jax import lax
from jax.experimental import pallas as pl
from jax.experimental.pallas import tpu as pltpu

F32 = jnp.float32
BF16 = jnp.bfloat16

D_MODEL = 1024
N_CTX_SEQ = 16
CTX_LEN = 256
DEPTH = 2
N_LAT_SEQ = 4
LAT_LEN = 1024
PAST_LEN = 256
GRID_W = 64
GRID_ROWS = LAT_LEN // GRID_W
CONV_W = 512
N_HEADS = 8
HEAD_DIM = 64
ATTN_W = N_HEADS * HEAD_DIM
POOL_W = 512
POOL_SIZES = (2, 4, 8, 16)
POOL_GROUP = POOL_W // len(POOL_SIZES)
N_BRANCH = 3
WIN_H = 8
WIN_W = 16
D_FF = 2816
EPS = 1e-6
IN_W = 3 * CONV_W + 3 * ATTN_W + POOL_W + N_BRANCH * D_MODEL

LANES = 128
TOK_TILE = 1024
CTX_IN_TILE = 512
IN_CHUNK = 512
FF_CHUNK = 256
N_FF_CHUNKS = D_FF // FF_CHUNK
MIX_TILE = 512
NORM_ROWS = 256
VMEM_LIMIT = 56 * 1024 * 1024


def _params(*sem):
    return pltpu.CompilerParams(dimension_semantics=sem, vmem_limit_bytes=VMEM_LIMIT)


def _resident(shape):
    return pl.BlockSpec(shape, lambda *_: (0,) * len(shape), pipeline_mode=pl.Buffered(1))


def _resident_layer(shape, layer):
    return pl.BlockSpec((None,) + shape, lambda *_: (layer,) + (0,) * len(shape), pipeline_mode=pl.Buffered(1))


def _mod_spec(mod, ntok, tile):
    tiles_per_mod = ntok // mod.shape[0] // tile
    return pl.BlockSpec((1, 6, D_MODEL), lambda i, *_: (i // tiles_per_mod, 0, 0))


N_CTX_TOK = N_CTX_SEQ * CTX_LEN
N_LAT_TOK = N_LAT_SEQ * LAT_LEN


def _mod_spec_both(tile):
    n_ctx = N_CTX_TOK // tile
    per_seq = LAT_LEN // tile
    return pl.BlockSpec((1, 6, D_MODEL), lambda i: (jnp.where(i < n_ctx, 0, 1 + (i - n_ctx) // per_seq), 0, 0))


def _norm_mod(x, g, shift, scale):
    ms = jnp.mean(x * x, axis=-1, keepdims=True)
    return (x * lax.rsqrt(ms + EPS) * g) * (1.0 + scale) + shift


def _row_in_seq(shape, seq_len):
    return lax.broadcasted_iota(jnp.int32, shape, 0) & (seq_len - 1)


def _shift_rows(x, s, row, seq_len):
    y = pltpu.roll(x, s % x.shape[0], 0)
    ok = (row >= s) if s > 0 else (row < seq_len + s)
    return jnp.where(ok, y, 0.0)


def _dwconv3(x, w_ref, row, seq_len):
    return (_shift_rows(x, 1, row, seq_len) * w_ref[0:1, :] + x * w_ref[1:2, :]
            + _shift_rows(x, -1, row, seq_len) * w_ref[2:3, :])


def _pool_inv_counts(tile, seq_len):
    r = np.arange(tile) % seq_len
    cols = []
    for w in POOL_SIZES:
        lo = np.maximum(r - w // 2, 0)
        hi = np.minimum(r - w // 2 + w, seq_len)
        cols.append(1.0 / (hi - lo))
    return jnp.asarray(np.stack(cols, axis=1), F32)


def _pool_delta(pg, w, inv_cnt, seq_len):
    row = _row_in_seq(pg.shape, seq_len)
    fwd = bwd = pg
    k = 1
    while k < w // 2:
        fwd = fwd + _shift_rows(fwd, -k, row, seq_len)
        bwd = bwd + _shift_rows(bwd, k, row, seq_len)
        k *= 2
    acc = fwd + _shift_rows(bwd, 1, row, seq_len)
    return acc * inv_cnt - pg


CAST_STEPS = 16


def _cast_kernel(*refs):
    n = len(refs) // 2
    for src, dst in zip(refs[:n], refs[n:]):
        dst[...] = src[...].astype(BF16)


def _cast_weights(*weights):
    flat = [w.reshape(-1, w.shape[-1]) for w in weights]
    specs = [pl.BlockSpec((w.shape[0] // CAST_STEPS, w.shape[1]), lambda i: (i, 0)) for w in flat]
    outs = pl.pallas_call(
        _cast_kernel,
        grid=(CAST_STEPS,),
        in_specs=specs,
        out_specs=specs,
        out_shape=[jax.ShapeDtypeStruct(w.shape, BF16) for w in flat],
        compiler_params=_params("arbitrary"),
        name="cast_weights",
    )(*flat)
    return [o.reshape(w.shape) for o, w in zip(outs, weights)]


def _adaln_kernel(c_ref, w_ref, b_ref, o_ref):
    c = c_ref[...]
    s = c * jax.nn.sigmoid(c)
    o_ref[0] = jnp.dot(s.astype(BF16), w_ref[0].astype(BF16), preferred_element_type=F32) + b_ref[0]


def _adaln(cvec, w_mod, b_mod):
    nb = cvec.shape[0]
    n_out = 6 * D_MODEL
    blk = 2048
    return pl.pallas_call(
        _adaln_kernel,
        grid=(DEPTH, n_out // blk),
        in_specs=[
            pl.BlockSpec((nb, D_MODEL), lambda l, n: (0, 0)),
            pl.BlockSpec((1, D_MODEL, blk), lambda l, n: (l, 0, n)),
            pl.BlockSpec((1, 1, blk), lambda l, n: (l, 0, n)),
        ],
        out_specs=pl.BlockSpec((1, nb, blk), lambda l, n: (l, 0, n)),
        out_shape=jax.ShapeDtypeStruct((DEPTH, nb, n_out), F32),
        compiler_params=_params("arbitrary", "arbitrary"),
        name="adaln",
    )(cvec, w_mod, b_mod.reshape(DEPTH, 1, n_out))


COL_CONV = 0
COL_QKV = 3 * CONV_W
COL_POOL = COL_QKV + 3 * ATTN_W
COL_GATE = COL_POOL + POOL_W
N_GATE_COLS = N_BRANCH * D_MODEL


def _inproj_kernel(x_ref, mod_ref, g_ref, w_ref, cw_ref, inv_ref, yconv_ref, qkv_ref, d_ref, gates_ref, *rest,
                   seq_len, with_kv):
    if with_kv:
        kv_ref, h_scr = rest
    else:
        (h_scr,) = rest

    def proj(col):
        return jnp.dot(h_scr[...], w_ref[:, col:col + IN_CHUNK], preferred_element_type=F32)

    parts = []
    for r in range(x_ref.shape[0] // NORM_ROWS):
        rows = slice(r * NORM_ROWS, (r + 1) * NORM_ROWS)
        h = _norm_mod(x_ref[rows, :], g_ref[...], mod_ref[0, 0:1, :], mod_ref[0, 1:2, :]).astype(BF16)
        h_scr[rows, :] = h
        parts.append(jnp.dot(h, w_ref[:, COL_CONV:COL_CONV + IN_CHUNK], preferred_element_type=F32))
    b_gate = jnp.concatenate(parts, axis=0)
    row = _row_in_seq((x_ref.shape[0], IN_CHUNK), seq_len)
    conv = _dwconv3(proj(COL_CONV + CONV_W) * proj(COL_CONV + 2 * CONV_W), cw_ref, row, seq_len)
    yconv_ref[...] = (b_gate * conv).astype(BF16)
    p = proj(COL_POOL)
    for c in range(N_GATE_COLS // IN_CHUNK):
        z = proj(COL_GATE + c * IN_CHUNK)
        if c < len(POOL_SIZES):
            lanes = slice(c * POOL_GROUP, (c + 1) * POOL_GROUP)
            d_ref[:, lanes] = _pool_delta(p[:, lanes], POOL_SIZES[c], inv_ref[:, c:c + 1], seq_len).astype(BF16)
        gates_ref[:, c * IN_CHUNK:(c + 1) * IN_CHUNK] = jax.nn.sigmoid(z).astype(BF16)
    for kv in range(2):
        z = proj(COL_QKV + (1 + kv) * ATTN_W)
        qkv_ref[:, (1 + kv) * ATTN_W:(2 + kv) * ATTN_W] = z.astype(BF16)
        if with_kv:
            for s in range(z.shape[0] // seq_len):
                for h_i in range(N_HEADS):
                    kv_ref[s, kv, h_i] = z[s * seq_len:(s + 1) * seq_len, h_i * HEAD_DIM:(h_i + 1) * HEAD_DIM]
    qkv_ref[:, 0:ATTN_W] = (proj(COL_QKV) * (HEAD_DIM ** -0.5)).astype(BF16)


def _inproj(x, mod, g1, w_in, conv_w, *, seq_len, tile, with_kv, layer, ntok, first_row=0):
    row = lambda i: (i, 0)
    first_tile = first_row // tile
    widths = (CONV_W, 3 * ATTN_W, POOL_W, N_GATE_COLS)
    out_shape = [jax.ShapeDtypeStruct((ntok, w), BF16) for w in widths]
    out_specs = [pl.BlockSpec((tile, w), row) for w in widths]
    if with_kv:
        one_layer = (2, N_HEADS, seq_len, HEAD_DIM)
        out_shape.append(jax.ShapeDtypeStruct((ntok // seq_len,) + one_layer, F32))
        out_specs.append(pl.BlockSpec((tile // seq_len,) + one_layer, lambda i: (i, 0, 0, 0, 0)))
    return pl.pallas_call(
        functools.partial(_inproj_kernel, seq_len=seq_len, with_kv=with_kv),
        grid=(ntok // tile,),
        in_specs=[
            pl.BlockSpec((tile, D_MODEL), lambda i: (first_tile + i, 0)),
            _mod_spec(mod, ntok, tile),
            _resident((1, D_MODEL)),
            _resident_layer((D_MODEL, IN_W), layer),
            _resident((3, CONV_W)),
            _resident((tile, len(POOL_SIZES))),
        ],
        out_specs=out_specs,
        out_shape=out_shape,
        scratch_shapes=[pltpu.VMEM((tile, D_MODEL), BF16)],
        compiler_params=_params("arbitrary"),
        name="inproj_ctx" if with_kv else "inproj_lat",
    )(x, mod, g1, w_in, conv_w, _pool_inv_counts(tile, seq_len))


def _attend_pair(q2, kvb):
    lane = lax.broadcasted_iota(jnp.int32, (1, LANES), 1)
    out = None
    for half in range(2):
        m = (lane >= half * HEAD_DIM) & (lane < (half + 1) * HEAD_DIM)
        qh = jnp.where(m, q2, jnp.zeros_like(q2))
        scores = []
        for k2, _, bias in kvb:
            s = lax.dot_general(qh, k2, (((1,), (1,)), ((), ())), preferred_element_type=F32)
            if bias is not None:
                s = s + bias[half]
            scores.append(s)
        mx = functools.reduce(jnp.maximum, [jnp.max(s, axis=-1, keepdims=True) for s in scores])
        es = [jnp.exp(s - mx) for s in scores]
        den = functools.reduce(lambda a, b: a + b, [jnp.sum(e, axis=-1, keepdims=True) for e in es])
        acc = None
        for e, (_, v2, _) in zip(es, kvb):
            vh = jnp.where(m, v2, jnp.zeros_like(v2))
            o = jnp.dot(e.astype(BF16), vh, preferred_element_type=F32)
            acc = o if acc is None else acc + o
        acc = acc * (1.0 / den)
        out = acc if out is None else out + acc
    return out


CTX_ATTN_SEQS = 4


def _ctx_attn_kernel(q_ref, k_ref, v_ref, o_ref):
    for s in range(q_ref.shape[0] // CTX_LEN):
        rows = slice(s * CTX_LEN, (s + 1) * CTX_LEN)
        for hp in range(N_HEADS // 2):
            sl = slice(hp * LANES, (hp + 1) * LANES)
            o = _attend_pair(q_ref[rows, sl], [(k_ref[rows, sl], v_ref[rows, sl], None)])
            o_ref[rows, sl] = o.astype(BF16)


def _ctx_attention(qkv):
    ntok = qkv.shape[0]
    blk = CTX_ATTN_SEQS * CTX_LEN
    return pl.pallas_call(
        _ctx_attn_kernel,
        grid=(ntok // blk,),
        in_specs=[
            pl.BlockSpec((blk, ATTN_W), lambda b: (b, 0)),
            pl.BlockSpec((blk, ATTN_W), lambda b: (b, 1)),
            pl.BlockSpec((blk, ATTN_W), lambda b: (b, 2)),
        ],
        out_specs=pl.BlockSpec((blk, ATTN_W), lambda b: (b, 0)),
        out_shape=jax.ShapeDtypeStruct((ntok, ATTN_W), BF16),
        compiler_params=_params("arbitrary"),
        name="attn_ctx",
    )(qkv, qkv, qkv)


Q_ROWS = 4
Q_BLOCK = Q_ROWS * GRID_W
N_Q_BLOCKS = GRID_ROWS // Q_ROWS
KEY_ROWS = 12
N_KEYS = KEY_ROWS * GRID_W
N_REL_ROWS = 2 * WIN_H - 1
N_ROW_PAIRS = N_REL_ROWS + 1


def _first_key_row(m):
    return min(max(Q_ROWS * m - WIN_H // 2, 0), GRID_ROWS - KEY_ROWS)


def _fill_bias(rp_ref, bias_scr):
    shape = (N_ROW_PAIRS * GRID_W, LANES)
    qc = lax.broadcasted_iota(jnp.int32, shape, 0) & (GRID_W - 1)
    lane = lax.broadcasted_iota(jnp.int32, shape, 1)
    kc = lane & (GRID_W - 1)
    col_start = jnp.clip(qc - WIN_W // 2, 0, GRID_W - WIN_W)
    valid = (kc >= col_start) & (kc < col_start + WIN_W)
    first_half = lax.broadcasted_iota(jnp.int32, (GRID_W, LANES), 1) < GRID_W
    for h in range(N_HEADS):
        t = jnp.broadcast_to(rp_ref[h][:, None, :], (N_ROW_PAIRS, GRID_W, LANES)).reshape(shape)
        t = pltpu.roll(t, LANES - (WIN_W - 1), 1)
        for bit in range(6):
            t = jnp.where(((qc >> bit) & 1) == 1, pltpu.roll(t, 1 << bit, 1), t)
        t = jnp.where(valid, t, -jnp.inf)
        for m in range(N_Q_BLOCKS):
            for ql in range(Q_ROWS):
                qr = Q_ROWS * m + ql
                win0 = min(max(qr - WIN_H // 2, 0), GRID_ROWS - WIN_H)
                for u in range(KEY_ROWS // 2):
                    kr = _first_key_row(m) + 2 * u
                    in0 = win0 <= kr < win0 + WIN_H
                    in1 = win0 <= kr + 1 < win0 + WIN_H
                    if in0 or in1:
                        pair = kr - qr + WIN_H
                        tile = t[pair * GRID_W:(pair + 1) * GRID_W]
                        if not in0:
                            tile = jnp.where(first_half, -jnp.inf, tile)
                        if not in1:
                            tile = jnp.where(first_half, tile, -jnp.inf)
                    else:
                        tile = jnp.full((GRID_W, LANES), -jnp.inf, F32)
                    bias_scr[h, m, ql * GRID_W:(ql + 1) * GRID_W, u * LANES:(u + 1) * LANES] = tile


def _lat_attn_kernel(q_ref, k_ref, v_ref, cache_ref, rp_ref, *rest, n_kv):
    kv_in, o_ref, bias_scr = rest[:n_kv], rest[n_kv], rest[-1]
    if n_kv:
        kv_out = rest[n_kv + 1]
        for layer, src in enumerate(kv_in):
            kv_out[0, layer] = src[0]
    hp = pl.program_id(0)
    m = pl.program_id(1)

    @pl.when((hp == 0) & (m == 0))
    def _():
        _fill_bias(rp_ref, bias_scr)

    def attend(m_idx, first_key, n_rows, col0=0):
        keys = pl.ds(first_key, n_rows * GRID_W)
        cols = slice(col0, col0 + n_rows * GRID_W)
        for s in range(q_ref.shape[0]):
            kvb = [
                (k_ref[s, keys, :], v_ref[s, keys, :],
                 (bias_scr[2 * hp, m_idx, :, cols], bias_scr[2 * hp + 1, m_idx, :, cols])),
                (cache_ref[0, s], cache_ref[1, s], None),
            ]
            o_ref[s] = _attend_pair(q_ref[s], kvb).astype(BF16)

    last = N_Q_BLOCKS - 1
    last_row0 = GRID_ROWS - WIN_H
    pl.when(m == 0)(lambda: attend(0, 0, WIN_H))
    pl.when(m == last)(lambda: attend(last, last_row0 * GRID_W, WIN_H,
                                      col0=(last_row0 - _first_key_row(last)) * GRID_W))

    @pl.when((m > 0) & (m < last))
    def _():
        row0 = jnp.clip(Q_ROWS * m - WIN_H // 2, 0, GRID_ROWS - KEY_ROWS)
        attend(m, pl.multiple_of(row0 * GRID_W, Q_BLOCK), KEY_ROWS)


def _lat_attention(qkv, cache_kv, layer, rp, kv_layers=()):
    nseq = qkv.shape[0] // LAT_LEN
    qkv = qkv.reshape(nseq, LAT_LEN, 3 * ATTN_W)
    n_hp = ATTN_W // LANES
    in_specs = [
        pl.BlockSpec((nseq, Q_BLOCK, LANES), lambda hp, m: (0, m, hp)),
        pl.BlockSpec((nseq, LAT_LEN, LANES), lambda hp, m: (0, 0, n_hp + hp)),
        pl.BlockSpec((nseq, LAT_LEN, LANES), lambda hp, m: (0, 0, 2 * n_hp + hp)),
        pl.BlockSpec((None, 2, nseq, PAST_LEN, LANES), lambda hp, m: (layer, 0, 0, 0, hp)),
        _resident(rp.shape),
    ]
    out_specs = [pl.BlockSpec((nseq, Q_BLOCK, LANES), lambda hp, m: (0, m, hp))]
    out_shape = [jax.ShapeDtypeStruct((nseq, LAT_LEN, ATTN_W), BF16)]
    if kv_layers:
        n_ctx, *slab = kv_layers[0].shape
        assert n_ctx == n_hp * N_Q_BLOCKS
        step = lambda hp, m: hp * N_Q_BLOCKS + m
        in_specs += [pl.BlockSpec((1, *slab), lambda hp, m: (step(hp, m), 0, 0, 0, 0))] * len(kv_layers)
        out_specs.append(pl.BlockSpec((1, len(kv_layers), *slab), lambda hp, m: (step(hp, m), 0, 0, 0, 0, 0)))
        out_shape.append(jax.ShapeDtypeStruct((n_ctx, len(kv_layers), *slab), F32))
    outs = pl.pallas_call(
        functools.partial(_lat_attn_kernel, n_kv=len(kv_layers)),
        grid=(n_hp, N_Q_BLOCKS),
        in_specs=in_specs,
        out_specs=out_specs,
        out_shape=out_shape,
        scratch_shapes=[pltpu.VMEM((N_HEADS, N_Q_BLOCKS, Q_BLOCK, N_KEYS), F32)],
        compiler_params=_params("arbitrary", "arbitrary"),
        name="attn_lat",
    )(qkv, qkv, qkv, cache_kv, rp, *kv_layers)
    y = outs[0].reshape(nseq * LAT_LEN, ATTN_W)
    return (y, outs[1]) if kv_layers else y


def _row_pair_table(rpb_l):
    lane_pad = GRID_W - rpb_l.shape[-1]
    lo = jnp.pad(rpb_l, ((0, 0), (1, 0), (0, lane_pad)))
    hi = jnp.pad(rpb_l, ((0, 0), (0, 1), (0, lane_pad)))
    return jnp.concatenate([lo, hi], axis=-1)


N_MIX_ACTS = 5


def _mix_kernel(mod_ref, *rest):
    acts_ctx, acts_lat = rest[:N_MIX_ACTS], rest[N_MIX_ACTS:2 * N_MIX_ACTS]
    pw_ref, ps_ref, wb_ref, wo_ref, o_ref = rest[2 * N_MIX_ACTS:]

    def body(x_ref, yc_ref, ya_ref, d_ref, gates_ref):
        yp = [jnp.dot(d_ref[:, g * POOL_GROUP:(g + 1) * POOL_GROUP], pw_ref[g], preferred_element_type=F32)
              for g in range(len(POOL_SIZES))]
        y_pool = (jnp.concatenate(yp, axis=-1) * ps_ref[...]).astype(BF16)
        merged = None
        for i, y in enumerate((yc_ref[...], ya_ref[...], y_pool)):
            proj = jnp.dot(y, wb_ref[i], preferred_element_type=F32)
            term = gates_ref[:, i * D_MODEL:(i + 1) * D_MODEL].astype(F32) * proj
            merged = term if merged is None else merged + term
        mix = jnp.dot(merged.astype(BF16), wo_ref[...], preferred_element_type=F32)
        o_ref[...] = x_ref[...] + mod_ref[0, 2:3, :] * mix

    is_ctx = pl.program_id(0) < N_CTX_TOK // MIX_TILE
    pl.when(is_ctx)(lambda: body(*acts_ctx))
    pl.when(jnp.logical_not(is_ctx))(lambda: body(*acts_lat))


def _mix(x_ctx, x_lat, acts_ctx, acts_lat, mod, pool_w, pool_scale, w_branch, w_out, *, layer):
    n_ctx = N_CTX_TOK // MIX_TILE
    widths = (D_MODEL, CONV_W, ATTN_W, POOL_W, N_BRANCH * D_MODEL)

    def path_specs(first_rows, ctx):
        def index(i, first_tile):
            local = jnp.minimum(i, n_ctx - 1) if ctx else jnp.maximum(i - n_ctx, 0)
            return (first_tile + local, 0)
        return [pl.BlockSpec((MIX_TILE, w), functools.partial(index, first_tile=r // MIX_TILE))
                for w, r in zip(widths, first_rows)]

    (xa, ra), (xb, rb) = x_ctx, x_lat
    return pl.pallas_call(
        _mix_kernel,
        grid=((N_CTX_TOK + N_LAT_TOK) // MIX_TILE,),
        in_specs=[_mod_spec_both(MIX_TILE)]
        + path_specs((ra, 0, 0, 0, 0), True) + path_specs((rb, 0, 0, 0, 0), False)
        + [
            _resident_layer((len(POOL_SIZES), POOL_GROUP, POOL_GROUP), layer),
            _resident((1, POOL_W)),
            _resident_layer((N_BRANCH, CONV_W, D_MODEL), layer),
            _resident_layer((D_MODEL, D_MODEL), layer),
        ],
        out_specs=pl.BlockSpec((MIX_TILE, D_MODEL), lambda i: (i, 0)),
        out_shape=jax.ShapeDtypeStruct((N_CTX_TOK + N_LAT_TOK, D_MODEL), F32),
        compiler_params=_params("arbitrary"),
        name="mix",
    )(mod, xa, *acts_ctx, xb, *acts_lat, pool_w, pool_scale, w_branch, w_out)


def _ffn_kernel(x_ref, mod_ref, g2_ref, wu_ref, fc_ref, wd_ref, gf_ref, o_ref, h_scr, act_scr,
                *, seq_len, final):
    if seq_len is None:
        seq_len = jnp.where(pl.program_id(0) < N_CTX_TOK // TOK_TILE, CTX_LEN, LAT_LEN)
    u0, val0 = [], []
    for r in range(x_ref.shape[0] // NORM_ROWS):
        rows = slice(r * NORM_ROWS, (r + 1) * NORM_ROWS)
        h = _norm_mod(x_ref[rows, :], g2_ref[...], mod_ref[0, 3:4, :], mod_ref[0, 4:5, :]).astype(BF16)
        h_scr[rows, :] = h
        u0.append(jnp.dot(h, wu_ref[:, 0:FF_CHUNK], preferred_element_type=F32))
        val0.append(jnp.dot(h, wu_ref[:, D_FF:D_FF + FF_CHUNK], preferred_element_type=F32))
    row = _row_in_seq((x_ref.shape[0], FF_CHUNK), seq_len)
    for c in range(N_FF_CHUNKS):
        sl = slice(c * FF_CHUNK, (c + 1) * FF_CHUNK)
        sl_val = slice(D_FF + c * FF_CHUNK, D_FF + (c + 1) * FF_CHUNK)
        if c == 0:
            u, val = jnp.concatenate(u0, axis=0), jnp.concatenate(val0, axis=0)
        else:
            u = jnp.dot(h_scr[...], wu_ref[:, sl], preferred_element_type=F32)
            val = jnp.dot(h_scr[...], wu_ref[:, sl_val], preferred_element_type=F32)
        u = _dwconv3(u, fc_ref.at[:, sl], row, seq_len)
        act_scr[:, sl] = (jax.nn.gelu(u, approximate=True) * val).astype(BF16)
    y = jnp.dot(act_scr[...], wd_ref[...], preferred_element_type=F32)
    xn = x_ref[...] + mod_ref[0, 5:6, :] * y
    if final:
        ms = jnp.mean(xn * xn, axis=-1, keepdims=True)
        xn = xn * lax.rsqrt(ms + EPS) * gf_ref[...]
    o_ref[...] = xn


def _ffn(x, mod, g2, w_up, f_conv, w_down, g_final, *, seq_len, final, layer, ntok, first_row=0):
    first_tile = first_row // TOK_TILE
    return pl.pallas_call(
        functools.partial(_ffn_kernel, seq_len=seq_len, final=final),
        grid=(ntok // TOK_TILE,),
        in_specs=[
            pl.BlockSpec((TOK_TILE, D_MODEL), lambda i: (first_tile + i, 0)),
            _mod_spec_both(TOK_TILE) if seq_len is None else _mod_spec(mod, ntok, TOK_TILE),
            _resident((1, D_MODEL)),
            _resident_layer((D_MODEL, 2 * D_FF), layer),
            _resident((3, D_FF)),
            _resident_layer((D_FF, D_MODEL), layer),
            _resident((1, D_MODEL)),
        ],
        out_specs=pl.BlockSpec((TOK_TILE, D_MODEL), lambda i: (i, 0)),
        out_shape=jax.ShapeDtypeStruct((ntok, D_MODEL), F32),
        scratch_shapes=[pltpu.VMEM((TOK_TILE, D_MODEL), BF16), pltpu.VMEM((TOK_TILE, D_FF), BF16)],
        compiler_params=_params("arbitrary"),
        name="ffn",
    )(x, mod, g2, w_up, f_conv, w_down, g_final)


def kernel(x_prompt, x_sample, cache_kv, c, c_ctx, w_mod, b_mod, g_norm1, g_norm2, w_in, conv_w, rpb,
           pool_w, pool_scale, w_branch, w_out, ffn_w_up, ffn_conv, ffn_w_down, g_final):
    n_mod = 8
    cvec = jnp.concatenate([c_ctx[None, :], c, jnp.zeros((n_mod - 1 - N_LAT_SEQ, D_MODEL), F32)], axis=0)
    mod = _adaln(cvec, w_mod, b_mod).reshape(DEPTH, n_mod, 6, D_MODEL)

    xp = x_prompt.reshape(N_CTX_SEQ * CTX_LEN, D_MODEL)
    xs = x_sample.reshape(N_LAT_SEQ * LAT_LEN, D_MODEL)
    gf = g_final.reshape(1, D_MODEL)
    cache_tok = cache_kv.transpose(1, 2, 0, 4, 3, 5).reshape(DEPTH, 2, N_LAT_SEQ, PAST_LEN, ATTN_W).astype(BF16)
    w_in_b, pool_w_b, w_br_b, w_out_b, w_up_b, w_dn_b = _cast_weights(
        w_in, pool_w, w_branch, w_out, ffn_w_up, ffn_w_down)
    kv_layers = []
    kv_state = None
    x_ctx, x_lat = (xp, 0), (xs, 0)
    for l in range(DEPTH):
        pool_s_l = pool_scale[l].reshape(1, POOL_W)
        g1 = g_norm1[l].reshape(1, D_MODEL)
        g2 = g_norm2[l].reshape(1, D_MODEL)
        mod_ctx = mod[l, 0:1]
        mod_lat = mod[l, 1:1 + N_LAT_SEQ]
        final = l == DEPTH - 1

        *acts_ctx, kv = _inproj(x_ctx[0], mod_ctx, g1, w_in_b, conv_w[l], seq_len=CTX_LEN, tile=CTX_IN_TILE,
                                with_kv=True, layer=l, ntok=N_CTX_TOK, first_row=x_ctx[1])
        kv_layers.append(kv)
        acts_ctx[1] = _ctx_attention(acts_ctx[1])
        acts_lat = list(_inproj(x_lat[0], mod_lat, g1, w_in_b, conv_w[l], seq_len=LAT_LEN, tile=TOK_TILE,
                                with_kv=False, layer=l, ntok=N_LAT_TOK, first_row=x_lat[1]))
        if final:
            acts_lat[1], kv_state = _lat_attention(acts_lat[1], cache_tok, l, _row_pair_table(rpb[l]),
                                                   tuple(kv_layers))
        else:
            acts_lat[1] = _lat_attention(acts_lat[1], cache_tok, l, _row_pair_table(rpb[l]))
        x = _mix(x_ctx, x_lat, acts_ctx, acts_lat, mod[l], pool_w_b, pool_s_l, w_br_b, w_out_b, layer=l)

        ffn = functools.partial(_ffn, x, g2=g2, w_up=w_up_b, f_conv=ffn_conv[l], w_down=w_dn_b, g_final=gf,
                                final=final, layer=l)
        if final:
            xp = ffn(mod_ctx, seq_len=CTX_LEN, ntok=N_CTX_TOK)
            xs = ffn(mod_lat, seq_len=LAT_LEN, ntok=N_LAT_TOK, first_row=N_CTX_TOK)
        else:
            x = ffn(mod[l], seq_len=None, ntok=N_CTX_TOK + N_LAT_TOK)
            x_ctx, x_lat = (x, 0), (x, N_CTX_TOK)

    y_prompt = xp.reshape(N_CTX_SEQ, CTX_LEN, D_MODEL)
    y_sample = xs.reshape(N_LAT_SEQ, LAT_LEN, D_MODEL)
    return (y_prompt, y_sample, kv_state)
```

```python
import functools

import numpy as np
import jax
import jax.numpy as jnp
from jax import lax
from jax.experimental import pallas as pl
from jax.experimental.pallas import tpu as pltpu

F32 = jnp.float32
BF16 = jnp.bfloat16

D_MODEL = 1024
N_CTX_SEQ = 16
CTX_LEN = 256
DEPTH = 2
N_LAT_SEQ = 4
LAT_LEN = 1024
PAST_LEN = 256
GRID_W = 64
GRID_ROWS = LAT_LEN // GRID_W
CONV_W = 512
N_HEADS = 8
HEAD_DIM = 64
ATTN_W = N_HEADS * HEAD_DIM
POOL_W = 512
POOL_SIZES = (2, 4, 8, 16)
POOL_GROUP = POOL_W // len(POOL_SIZES)
N_BRANCH = 3
WIN_H = 8
WIN_W = 16
D_FF = 2816
EPS = 1e-6
IN_W = 3 * CONV_W + 3 * ATTN_W + POOL_W + N_BRANCH * D_MODEL

LANES = 128
TOK_TILE = 1024
CTX_IN_TILE = 512
IN_CHUNK = 512
FF_CHUNK = 256
N_FF_CHUNKS = D_FF // FF_CHUNK
MIX_TILE = 1024
NORM_ROWS = 256
VMEM_LIMIT = 56 * 1024 * 1024
INPROJ_VMEM = 49 * 1024 * 1024
MIX_VMEM = 40 * 1024 * 1024
FFN_VMEM = 34 * 1024 * 1024


def _params(*sem, vmem=VMEM_LIMIT):
    return pltpu.CompilerParams(dimension_semantics=sem, vmem_limit_bytes=vmem)


IN_VMEM = pl.BlockSpec(memory_space=pltpu.VMEM)


def _resident(shape):
    return pl.BlockSpec(shape, lambda *_: (0,) * len(shape), pipeline_mode=pl.Buffered(1))


def _resident_layer(shape, layer):
    return pl.BlockSpec((None,) + shape, lambda *_: (layer,) + (0,) * len(shape), pipeline_mode=pl.Buffered(1))


def _mod_spec(mod, ntok, tile):
    tiles_per_mod = ntok // mod.shape[0] // tile
    return pl.BlockSpec((1, 6, D_MODEL), lambda i, *_: (i // tiles_per_mod, 0, 0))


def _norm_mod(x, g, shift, scale):
    ms = jnp.mean(x * x, axis=-1, keepdims=True)
    return (x * lax.rsqrt(ms + EPS) * g) * (1.0 + scale) + shift


def _row_in_seq(shape, seq_len):
    return lax.broadcasted_iota(jnp.int32, shape, 0) & (seq_len - 1)


def _shift_rows(x, s, row, seq_len):
    y = pltpu.roll(x, s % x.shape[0], 0)
    ok = (row >= s) if s > 0 else (row < seq_len + s)
    return jnp.where(ok, y, 0.0)


def _dwconv3(x, w_ref, row, seq_len):
    return (_shift_rows(x, 1, row, seq_len) * w_ref[0:1, :] + x * w_ref[1:2, :]
            + _shift_rows(x, -1, row, seq_len) * w_ref[2:3, :])


def _pool_inv_counts(tile, seq_len):
    r = np.arange(tile) % seq_len
    cols = []
    for w in POOL_SIZES:
        lo = np.maximum(r - w // 2, 0)
        hi = np.minimum(r - w // 2 + w, seq_len)
        cols.append(1.0 / (hi - lo))
    return jnp.asarray(np.stack(cols, axis=1), F32)


def _pool_delta(pg, w, inv_cnt, seq_len):
    row = _row_in_seq(pg.shape, seq_len)
    fwd = bwd = pg
    k = 1
    while k < w // 2:
        fwd = fwd + _shift_rows(fwd, -k, row, seq_len)
        bwd = bwd + _shift_rows(bwd, k, row, seq_len)
        k *= 2
    acc = fwd + _shift_rows(bwd, 1, row, seq_len)
    return acc * inv_cnt - pg


CAST_STEPS = 8


def _cast_kernel(*refs):
    n = len(refs) // 2
    for src, dst in zip(refs[:n], refs[n:]):
        dst[...] = src[...].astype(BF16)


def _cast_weights(layer, *weights):
    flat = [w.reshape(DEPTH, -1, w.shape[-1]) for w in weights]
    outs = pl.pallas_call(
        _cast_kernel,
        grid=(CAST_STEPS,),
        in_specs=[pl.BlockSpec((None, w.shape[1] // CAST_STEPS, w.shape[2]), lambda i: (layer, i, 0)) for w in flat],
        out_specs=[pl.BlockSpec((w.shape[1] // CAST_STEPS, w.shape[2]), lambda i: (i, 0)) for w in flat],
        out_shape=[jax.ShapeDtypeStruct(w.shape[1:], BF16) for w in flat],
        compiler_params=_params("arbitrary"),
        name="cast_weights",
    )(*flat)
    return [o.reshape(w.shape[1:]) for o, w in zip(outs, weights)]


ADALN_BLOCK = 2048


def _adaln_kernel(c_ref, w_ref, b_ref, o_ref):
    c = c_ref[...]
    s = c * jax.nn.sigmoid(c)
    o_ref[0] = jnp.dot(s.astype(BF16), w_ref[0].astype(BF16), preferred_element_type=F32) + b_ref[0]


def _adaln(cvec, w_mod, b_mod):
    nb = cvec.shape[0]
    n_out = 6 * D_MODEL
    blk = ADALN_BLOCK
    return pl.pallas_call(
        _adaln_kernel,
        grid=(DEPTH, n_out // blk),
        in_specs=[
            pl.BlockSpec((nb, D_MODEL), lambda l, n: (0, 0)),
            pl.BlockSpec((1, D_MODEL, blk), lambda l, n: (l, 0, n)),
            pl.BlockSpec((1, 1, blk), lambda l, n: (l, 0, n)),
        ],
        out_specs=pl.BlockSpec((1, nb, blk), lambda l, n: (l, 0, n)),
        out_shape=jax.ShapeDtypeStruct((DEPTH, nb, n_out), F32),
        compiler_params=_params("arbitrary", "arbitrary"),
        name="adaln",
    )(cvec, w_mod, b_mod.reshape(DEPTH, 1, n_out))


COL_CONV = 0
COL_QKV = 3 * CONV_W
COL_POOL = COL_QKV + 3 * ATTN_W
COL_GATE = COL_POOL + POOL_W
N_GATE_COLS = N_BRANCH * D_MODEL


def _inproj_kernel(x_ref, mod_ref, g_ref, w_ref, cw_ref, inv_ref, yconv_ref, qkv_ref, d_ref, gates_ref, *rest,
                   seq_len, with_kv):
    if with_kv:
        kv_ref, h_scr = rest
    else:
        (h_scr,) = rest

    def proj(col):
        return jnp.dot(h_scr[...], w_ref[:, col:col + IN_CHUNK], preferred_element_type=F32)

    parts = []
    for r in range(x_ref.shape[0] // NORM_ROWS):
        rows = slice(r * NORM_ROWS, (r + 1) * NORM_ROWS)
        h = _norm_mod(x_ref[rows, :], g_ref[...], mod_ref[0, 0:1, :], mod_ref[0, 1:2, :]).astype(BF16)
        h_scr[rows, :] = h
        parts.append(jnp.dot(h, w_ref[:, COL_CONV:COL_CONV + IN_CHUNK], preferred_element_type=F32))
    b_gate = jnp.concatenate(parts, axis=0)
    row = _row_in_seq((x_ref.shape[0], IN_CHUNK), seq_len)
    conv = _dwconv3(proj(COL_CONV + CONV_W) * proj(COL_CONV + 2 * CONV_W), cw_ref, row, seq_len)
    yconv_ref[...] = (b_gate * conv).astype(BF16)
    p = proj(COL_POOL)
    for c in range(N_GATE_COLS // IN_CHUNK):
        z = proj(COL_GATE + c * IN_CHUNK)
        if c < len(POOL_SIZES):
            lanes = slice(c * POOL_GROUP, (c + 1) * POOL_GROUP)
            d_ref[:, lanes] = _pool_delta(p[:, lanes], POOL_SIZES[c], inv_ref[:, c:c + 1], seq_len).astype(BF16)
        gates_ref[:, c * IN_CHUNK:(c + 1) * IN_CHUNK] = jax.nn.sigmoid(z).astype(BF16)
    for kv in range(2):
        z = proj(COL_QKV + (1 + kv) * ATTN_W)
        qkv_ref[:, (1 + kv) * ATTN_W:(2 + kv) * ATTN_W] = z.astype(BF16)
        if with_kv:
            for s in range(z.shape[0] // seq_len):
                for h_i in range(N_HEADS):
                    kv_ref[s, kv, h_i] = z[s * seq_len:(s + 1) * seq_len, h_i * HEAD_DIM:(h_i + 1) * HEAD_DIM]
    qkv_ref[:, 0:ATTN_W] = (proj(COL_QKV) * (HEAD_DIM ** -0.5)).astype(BF16)


def _inproj(x, mod, g1, w_in, conv_w, *, seq_len, tile, with_kv, layer):
    ntok = x.shape[0]
    row = lambda i: (i, 0)
    widths = (CONV_W, 3 * ATTN_W, POOL_W, N_GATE_COLS)
    out_shape = [jax.ShapeDtypeStruct((ntok, w), BF16) for w in widths]
    out_specs = [pl.BlockSpec((tile, w), row) for w in widths]
    if with_kv:
        one_layer = (2, N_HEADS, seq_len, HEAD_DIM)
        out_shape.append(jax.ShapeDtypeStruct((ntok // seq_len,) + one_layer, F32))
        out_specs.append(pl.BlockSpec((tile // seq_len,) + one_layer, lambda i: (i, 0, 0, 0, 0)))
    return pl.pallas_call(
        functools.partial(_inproj_kernel, seq_len=seq_len, with_kv=with_kv),
        grid=(ntok // tile,),
        in_specs=[
            pl.BlockSpec((tile, D_MODEL), row),
            _mod_spec(mod, ntok, tile),
            _resident((1, D_MODEL)),
            IN_VMEM,
            _resident((3, CONV_W)),
            _resident((tile, len(POOL_SIZES))),
        ],
        out_specs=out_specs,
        out_shape=out_shape,
        scratch_shapes=[pltpu.VMEM((tile, D_MODEL), BF16)],
        compiler_params=_params("arbitrary", vmem=INPROJ_VMEM),
        name="inproj_ctx" if with_kv else "inproj_lat",
    )(x, mod, g1, w_in, conv_w, _pool_inv_counts(tile, seq_len))


def _attend_pair(q2, kvb):
    lane = lax.broadcasted_iota(jnp.int32, (1, LANES), 1)
    out = None
    for half in range(2):
        m = (lane >= half * HEAD_DIM) & (lane < (half + 1) * HEAD_DIM)
        qh = jnp.where(m, q2, jnp.zeros_like(q2))
        scores = []
        for k2, _, bias in kvb:
            s = lax.dot_general(qh, k2, (((1,), (1,)), ((), ())), preferred_element_type=F32)
            if bias is not None:
                s = s + bias[half]
            scores.append(s)
        mx = functools.reduce(jnp.maximum, [jnp.max(s, axis=-1, keepdims=True) for s in scores])
        es = [jnp.exp(s - mx) for s in scores]
        den = functools.reduce(lambda a, b: a + b, [jnp.sum(e, axis=-1, keepdims=True) for e in es])
        acc = None
        for e, (_, v2, _) in zip(es, kvb):
            vh = jnp.where(m, v2, jnp.zeros_like(v2))
            o = jnp.dot(e.astype(BF16), vh, preferred_element_type=F32)
            acc = o if acc is None else acc + o
        acc = acc * (1.0 / den)
        out = acc if out is None else out + acc
    return out


CTX_ATTN_SEQS = 4


def _ctx_attn_kernel(q_ref, k_ref, v_ref, o_ref):
    for s in range(q_ref.shape[0] // CTX_LEN):
        rows = slice(s * CTX_LEN, (s + 1) * CTX_LEN)
        for hp in range(N_HEADS // 2):
            sl = slice(hp * LANES, (hp + 1) * LANES)
            o = _attend_pair(q_ref[rows, sl], [(k_ref[rows, sl], v_ref[rows, sl], None)])
            o_ref[rows, sl] = o.astype(BF16)


def _ctx_attention(qkv):
    ntok = qkv.shape[0]
    blk = CTX_ATTN_SEQS * CTX_LEN
    return pl.pallas_call(
        _ctx_attn_kernel,
        grid=(ntok // blk,),
        in_specs=[
            pl.BlockSpec((blk, ATTN_W), lambda b: (b, 0)),
            pl.BlockSpec((blk, ATTN_W), lambda b: (b, 1)),
            pl.BlockSpec((blk, ATTN_W), lambda b: (b, 2)),
        ],
        out_specs=pl.BlockSpec((blk, ATTN_W), lambda b: (b, 0)),
        out_shape=jax.ShapeDtypeStruct((ntok, ATTN_W), BF16),
        compiler_params=_params("arbitrary"),
        name="attn_ctx",
    )(qkv, qkv, qkv)


Q_ROWS = 4
Q_BLOCK = Q_ROWS * GRID_W
N_Q_BLOCKS = GRID_ROWS // Q_ROWS
KEY_ROWS = 12
N_KEYS = KEY_ROWS * GRID_W
N_REL_ROWS = 2 * WIN_H - 1
N_ROW_PAIRS = N_REL_ROWS + 1


def _first_key_row(m):
    return min(max(Q_ROWS * m - WIN_H // 2, 0), GRID_ROWS - KEY_ROWS)


def _fill_bias(rp_ref, bias_scr):
    shape = (N_ROW_PAIRS * GRID_W, LANES)
    qc = lax.broadcasted_iota(jnp.int32, shape, 0) & (GRID_W - 1)
    lane = lax.broadcasted_iota(jnp.int32, shape, 1)
    kc = lane & (GRID_W - 1)
    col_start = jnp.clip(qc - WIN_W // 2, 0, GRID_W - WIN_W)
    valid = (kc >= col_start) & (kc < col_start + WIN_W)
    first_half = lax.broadcasted_iota(jnp.int32, (GRID_W, LANES), 1) < GRID_W
    for h in range(N_HEADS):
        t = jnp.broadcast_to(rp_ref[h][:, None, :], (N_ROW_PAIRS, GRID_W, LANES)).reshape(shape)
        t = pltpu.roll(t, LANES - (WIN_W - 1), 1)
        for bit in range(6):
            t = jnp.where(((qc >> bit) & 1) == 1, pltpu.roll(t, 1 << bit, 1), t)
        t = jnp.where(valid, t, -jnp.inf)
        for m in range(N_Q_BLOCKS):
            for ql in range(Q_ROWS):
                qr = Q_ROWS * m + ql
                win0 = min(max(qr - WIN_H // 2, 0), GRID_ROWS - WIN_H)
                for u in range(KEY_ROWS // 2):
                    kr = _first_key_row(m) + 2 * u
                    in0 = win0 <= kr < win0 + WIN_H
                    in1 = win0 <= kr + 1 < win0 + WIN_H
                    if in0 or in1:
                        pair = kr - qr + WIN_H
                        tile = t[pair * GRID_W:(pair + 1) * GRID_W]
                        if not in0:
                            tile = jnp.where(first_half, -jnp.inf, tile)
                        if not in1:
                            tile = jnp.where(first_half, tile, -jnp.inf)
                    else:
                        tile = jnp.full((GRID_W, LANES), -jnp.inf, F32)
                    bias_scr[h, m, ql * GRID_W:(ql + 1) * GRID_W, u * LANES:(u + 1) * LANES] = tile


def _lat_attn_kernel(q_ref, k_ref, v_ref, cache_ref, rp_ref, *rest, n_kv):
    kv_in, o_ref, bias_scr = rest[:n_kv], rest[n_kv], rest[-1]
    if n_kv:
        kv_out = rest[n_kv + 1]
        for layer, src in enumerate(kv_in):
            kv_out[0, layer] = src[0]
    hp = pl.program_id(0)
    m = pl.program_id(1)

    @pl.when((hp == 0) & (m == 0))
    def _():
        _fill_bias(rp_ref, bias_scr)

    def attend(m_idx, first_key, n_rows, col0=0):
        keys = pl.ds(first_key, n_rows * GRID_W)
        cols = slice(col0, col0 + n_rows * GRID_W)
        for s in range(q_ref.shape[0]):
            kvb = [
                (k_ref[s, keys, :], v_ref[s, keys, :],
                 (bias_scr[2 * hp, m_idx, :, cols], bias_scr[2 * hp + 1, m_idx, :, cols])),
                (cache_ref[0, s], cache_ref[1, s], None),
            ]
            o_ref[s] = _attend_pair(q_ref[s], kvb).astype(BF16)

    last = N_Q_BLOCKS - 1
    last_row0 = GRID_ROWS - WIN_H
    pl.when(m == 0)(lambda: attend(0, 0, WIN_H))
    pl.when(m == last)(lambda: attend(last, last_row0 * GRID_W, WIN_H,
                                      col0=(last_row0 - _first_key_row(last)) * GRID_W))

    @pl.when((m > 0) & (m < last))
    def _():
        row0 = jnp.clip(Q_ROWS * m - WIN_H // 2, 0, GRID_ROWS - KEY_ROWS)
        attend(m, pl.multiple_of(row0 * GRID_W, Q_BLOCK), KEY_ROWS)


def _lat_attention(qkv, cache_kv, layer, rp, kv_layers=()):
    nseq = qkv.shape[0] // LAT_LEN
    qkv = qkv.reshape(nseq, LAT_LEN, 3 * ATTN_W)
    n_hp = ATTN_W // LANES
    in_specs = [
        pl.BlockSpec((nseq, Q_BLOCK, LANES), lambda hp, m: (0, m, hp)),
        pl.BlockSpec((nseq, LAT_LEN, LANES), lambda hp, m: (0, 0, n_hp + hp)),
        pl.BlockSpec((nseq, LAT_LEN, LANES), lambda hp, m: (0, 0, 2 * n_hp + hp)),
        pl.BlockSpec((None, 2, nseq, PAST_LEN, LANES), lambda hp, m: (layer, 0, 0, 0, hp)),
        _resident(rp.shape),
    ]
    out_specs = [pl.BlockSpec((nseq, Q_BLOCK, LANES), lambda hp, m: (0, m, hp))]
    out_shape = [jax.ShapeDtypeStruct((nseq, LAT_LEN, ATTN_W), BF16)]
    if kv_layers:
        n_ctx, *slab = kv_layers[0].shape
        assert n_ctx == n_hp * N_Q_BLOCKS
        step = lambda hp, m: hp * N_Q_BLOCKS + m
        in_specs += [pl.BlockSpec((1, *slab), lambda hp, m: (step(hp, m), 0, 0, 0, 0))] * len(kv_layers)
        out_specs.append(pl.BlockSpec((1, len(kv_layers), *slab), lambda hp, m: (step(hp, m), 0, 0, 0, 0, 0)))
        out_shape.append(jax.ShapeDtypeStruct((n_ctx, len(kv_layers), *slab), F32))
    outs = pl.pallas_call(
        functools.partial(_lat_attn_kernel, n_kv=len(kv_layers)),
        grid=(n_hp, N_Q_BLOCKS),
        in_specs=in_specs,
        out_specs=out_specs,
        out_shape=out_shape,
        scratch_shapes=[pltpu.VMEM((N_HEADS, N_Q_BLOCKS, Q_BLOCK, N_KEYS), F32)],
        compiler_params=_params("arbitrary", "arbitrary"),
        name="attn_lat",
    )(qkv, qkv, qkv, cache_kv, rp, *kv_layers)
    y = outs[0].reshape(nseq * LAT_LEN, ATTN_W)
    return (y, outs[1]) if kv_layers else y


def _row_pair_table(rpb_l):
    lane_pad = GRID_W - rpb_l.shape[-1]
    lo = jnp.pad(rpb_l, ((0, 0), (1, 0), (0, lane_pad)))
    hi = jnp.pad(rpb_l, ((0, 0), (0, 1), (0, lane_pad)))
    return jnp.concatenate([lo, hi], axis=-1)


def _mix_kernel(x_ref, mod_ref, yc_ref, ya_ref, d_ref, gates_ref, pw_ref, ps_ref, wb_ref, wo_ref, o_ref):
    yp = [jnp.dot(d_ref[:, g * POOL_GROUP:(g + 1) * POOL_GROUP], pw_ref[g], preferred_element_type=F32)
          for g in range(len(POOL_SIZES))]
    y_pool = (jnp.concatenate(yp, axis=-1) * ps_ref[...]).astype(BF16)
    merged = None
    for i, y in enumerate((yc_ref[...], ya_ref[...], y_pool)):
        proj = jnp.dot(y, wb_ref[i], preferred_element_type=F32)
        term = gates_ref[:, i * D_MODEL:(i + 1) * D_MODEL].astype(F32) * proj
        merged = term if merged is None else merged + term
    mix = jnp.dot(merged.astype(BF16), wo_ref[...], preferred_element_type=F32)
    o_ref[...] = x_ref[...] + mod_ref[0, 2:3, :] * mix


def _mix(x, mod, yconv, yattn, d, gates, pool_w, pool_scale, w_branch, w_out, *, layer):
    ntok = x.shape[0]
    row = lambda i: (i, 0)
    return pl.pallas_call(
        _mix_kernel,
        grid=(ntok // MIX_TILE,),
        in_specs=[
            pl.BlockSpec((MIX_TILE, D_MODEL), row),
            _mod_spec(mod, ntok, MIX_TILE),
            pl.BlockSpec((MIX_TILE, CONV_W), row),
            pl.BlockSpec((MIX_TILE, ATTN_W), row),
            pl.BlockSpec((MIX_TILE, POOL_W), row),
            pl.BlockSpec((MIX_TILE, N_BRANCH * D_MODEL), row),
            IN_VMEM,
            _resident((1, POOL_W)),
            IN_VMEM,
            IN_VMEM,
        ],
        out_specs=pl.BlockSpec((MIX_TILE, D_MODEL), row),
        out_shape=jax.ShapeDtypeStruct((ntok, D_MODEL), F32),
        compiler_params=_params("arbitrary", vmem=MIX_VMEM),
        name="mix",
    )(x, mod, yconv, yattn, d, gates, pool_w, pool_scale, w_branch, w_out)


def _ffn_kernel(x_ref, mod_ref, g2_ref, wu_ref, fc_ref, wd_ref, gf_ref, o_ref, h_scr, act_scr,
                *, seq_len, final):
    u0, val0 = [], []
    for r in range(x_ref.shape[0] // NORM_ROWS):
        rows = slice(r * NORM_ROWS, (r + 1) * NORM_ROWS)
        h = _norm_mod(x_ref[rows, :], g2_ref[...], mod_ref[0, 3:4, :], mod_ref[0, 4:5, :]).astype(BF16)
        h_scr[rows, :] = h
        u0.append(jnp.dot(h, wu_ref[:, 0:FF_CHUNK], preferred_element_type=F32))
        val0.append(jnp.dot(h, wu_ref[:, D_FF:D_FF + FF_CHUNK], preferred_element_type=F32))
    row = _row_in_seq((x_ref.shape[0], FF_CHUNK), seq_len)
    for c in range(N_FF_CHUNKS):
        sl = slice(c * FF_CHUNK, (c + 1) * FF_CHUNK)
        sl_val = slice(D_FF + c * FF_CHUNK, D_FF + (c + 1) * FF_CHUNK)
        if c == 0:
            u, val = jnp.concatenate(u0, axis=0), jnp.concatenate(val0, axis=0)
        else:
            u = jnp.dot(h_scr[...], wu_ref[:, sl], preferred_element_type=F32)
            val = jnp.dot(h_scr[...], wu_ref[:, sl_val], preferred_element_type=F32)
        u = _dwconv3(u, fc_ref.at[:, sl], row, seq_len)
        act_scr[:, sl] = (jax.nn.gelu(u, approximate=True) * val).astype(BF16)
    y = jnp.dot(act_scr[...], wd_ref[...], preferred_element_type=F32)
    xn = x_ref[...] + mod_ref[0, 5:6, :] * y
    if final:
        ms = jnp.mean(xn * xn, axis=-1, keepdims=True)
        xn = xn * lax.rsqrt(ms + EPS) * gf_ref[...]
    o_ref[...] = xn


def _ffn(x, mod, g2, w_up, f_conv, w_down, g_final, *, seq_len, final, layer):
    ntok = x.shape[0]
    return pl.pallas_call(
        functools.partial(_ffn_kernel, seq_len=seq_len, final=final),
        grid=(ntok // TOK_TILE,),
        in_specs=[
            pl.BlockSpec((TOK_TILE, D_MODEL), lambda i: (i, 0)),
            _mod_spec(mod, ntok, TOK_TILE),
            _resident((1, D_MODEL)),
            IN_VMEM,
            _resident((3, D_FF)),
            IN_VMEM,
            _resident((1, D_MODEL)),
        ],
        out_specs=pl.BlockSpec((TOK_TILE, D_MODEL), lambda i: (i, 0)),
        out_shape=jax.ShapeDtypeStruct((ntok, D_MODEL), F32),
        scratch_shapes=[pltpu.VMEM((TOK_TILE, D_MODEL), BF16), pltpu.VMEM((TOK_TILE, D_FF), BF16)],
        compiler_params=_params("arbitrary", vmem=FFN_VMEM),
        name="ffn",
    )(x, mod, g2, w_up, f_conv, w_down, g_final)


def kernel(x_prompt, x_sample, cache_kv, c, c_ctx, w_mod, b_mod, g_norm1, g_norm2, w_in, conv_w, rpb,
           pool_w, pool_scale, w_branch, w_out, ffn_w_up, ffn_conv, ffn_w_down, g_final):
    n_mod = 8
    cvec = jnp.concatenate([c_ctx[None, :], c, jnp.zeros((n_mod - 1 - N_LAT_SEQ, D_MODEL), F32)], axis=0)
    mod = _adaln(cvec, w_mod, b_mod).reshape(DEPTH, n_mod, 6, D_MODEL)

    xp = x_prompt.reshape(N_CTX_SEQ * CTX_LEN, D_MODEL)
    xs = x_sample.reshape(N_LAT_SEQ * LAT_LEN, D_MODEL)
    gf = g_final.reshape(1, D_MODEL)
    cache_tok = cache_kv.transpose(1, 2, 0, 4, 3, 5).reshape(DEPTH, 2, N_LAT_SEQ, PAST_LEN, ATTN_W).astype(BF16)
    kv_layers = []
    kv_state = None
    for l in range(DEPTH):
        w_in_b, pool_w_b, w_br_b, w_out_b, w_up_b, w_dn_b = _cast_weights(
            l, w_in, pool_w, w_branch, w_out, ffn_w_up, ffn_w_down)
        pool_s_l = pool_scale[l].reshape(1, POOL_W)
        g1 = g_norm1[l].reshape(1, D_MODEL)
        g2 = g_norm2[l].reshape(1, D_MODEL)
        mod_ctx = mod[l, 0:1]
        mod_lat = mod[l, 1:1 + N_LAT_SEQ]
        final = l == DEPTH - 1

        yc, qkv, d, gates, kv = _inproj(xp, mod_ctx, g1, w_in_b, conv_w[l], seq_len=CTX_LEN, tile=CTX_IN_TILE,
                                        with_kv=True, layer=l)
        kv_layers.append(kv)
        ya = _ctx_attention(qkv)
        xp = _mix(xp, mod_ctx, yc, ya, d, gates, pool_w_b, pool_s_l, w_br_b, w_out_b, layer=l)
        xp = _ffn(xp, mod_ctx, g2, w_up_b, ffn_conv[l], w_dn_b, gf, seq_len=CTX_LEN, final=final, layer=l)

        yc, qkv, d, gates = _inproj(xs, mod_lat, g1, w_in_b, conv_w[l], seq_len=LAT_LEN, tile=TOK_TILE,
                                    with_kv=False, layer=l)
        if final:
            ya, kv_state = _lat_attention(qkv, cache_tok, l, _row_pair_table(rpb[l]), tuple(kv_layers))
        else:
            ya = _lat_attention(qkv, cache_tok, l, _row_pair_table(rpb[l]))
        xs = _mix(xs, mod_lat, yc, ya, d, gates, pool_w_b, pool_s_l, w_br_b, w_out_b, layer=l)
        xs = _ffn(xs, mod_lat, g2, w_up_b, ffn_conv[l], w_dn_b, gf, seq_len=LAT_LEN, final=final, layer=l)

    y_prompt = xp.reshape(N_CTX_SEQ, CTX_LEN, D_MODEL)
    y_sample = xs.reshape(N_LAT_SEQ, LAT_LEN, D_MODEL)
    return (y_prompt, y_sample, kv_state)
```

```python
import functools

import numpy as np
import jax
import jax.numpy as jnp
from jax import lax
from jax.experimental import pallas as pl
from jax.experimental.pallas import tpu as pltpu

F32 = jnp.float32
BF16 = jnp.bfloat16

D_MODEL = 1024
N_CTX_SEQ = 16
CTX_LEN = 256
DEPTH = 2
N_LAT_SEQ = 4
LAT_LEN = 1024
PAST_LEN = 256
GRID_W = 64
GRID_ROWS = LAT_LEN // GRID_W
CONV_W = 512
N_HEADS = 8
HEAD_DIM = 64
ATTN_W = N_HEADS * HEAD_DIM
POOL_W = 512
POOL_SIZES = (2, 4, 8, 16)
POOL_GROUP = POOL_W // len(POOL_SIZES)
N_BRANCH = 3
WIN_H = 8
WIN_W = 16
D_FF = 2816
EPS = 1e-6
IN_W = 3 * CONV_W + 3 * ATTN_W + POOL_W + N_BRANCH * D_MODEL

LANES = 128
TOK_TILE = 1024
CTX_IN_TILE = 512
IN_CHUNK = 512
FF_CHUNK = 256
N_FF_CHUNKS = D_FF // FF_CHUNK
MIX_TILE = 1024
NORM_ROWS = 256
VMEM_LIMIT = 56 * 1024 * 1024


def _params(*sem):
    return pltpu.CompilerParams(dimension_semantics=sem, vmem_limit_bytes=VMEM_LIMIT)


def _resident(shape):
    return pl.BlockSpec(shape, lambda *_: (0,) * len(shape), pipeline_mode=pl.Buffered(1))


def _resident_layer(shape, layer):
    return pl.BlockSpec((None,) + shape, lambda *_: (layer,) + (0,) * len(shape), pipeline_mode=pl.Buffered(1))


def _mod_spec(mod, ntok, tile):
    tiles_per_mod = ntok // mod.shape[0] // tile
    return pl.BlockSpec((1, 6, D_MODEL), lambda i, *_: (i // tiles_per_mod, 0, 0))


def _norm_mod(x, g, shift, scale):
    ms = jnp.mean(x * x, axis=-1, keepdims=True)
    return (x * lax.rsqrt(ms + EPS) * g) * (1.0 + scale) + shift


def _row_in_seq(shape, seq_len):
    return lax.broadcasted_iota(jnp.int32, shape, 0) & (seq_len - 1)


def _shift_rows(x, s, row, seq_len):
    y = pltpu.roll(x, s % x.shape[0], 0)
    ok = (row >= s) if s > 0 else (row < seq_len + s)
    return jnp.where(ok, y, 0.0)


def _dwconv3(x, w_ref, row, seq_len):
    return (_shift_rows(x, 1, row, seq_len) * w_ref[0:1, :] + x * w_ref[1:2, :]
            + _shift_rows(x, -1, row, seq_len) * w_ref[2:3, :])


def _pool_inv_counts(tile, seq_len):
    r = np.arange(tile) % seq_len
    cols = []
    for w in POOL_SIZES:
        lo = np.maximum(r - w // 2, 0)
        hi = np.minimum(r - w // 2 + w, seq_len)
        cols.append(1.0 / (hi - lo))
    return jnp.asarray(np.stack(cols, axis=1), F32)


def _pool_delta(pg, w, inv_cnt, seq_len):
    row = _row_in_seq(pg.shape, seq_len)
    fwd = bwd = pg
    k = 1
    while k < w // 2:
        fwd = fwd + _shift_rows(fwd, -k, row, seq_len)
        bwd = bwd + _shift_rows(bwd, k, row, seq_len)
        k *= 2
    acc = fwd + _shift_rows(bwd, 1, row, seq_len)
    return acc * inv_cnt - pg


CAST_STEPS = 16


def _cast_kernel(*refs):
    n = len(refs) // 2
    for src, dst in zip(refs[:n], refs[n:]):
        dst[...] = src[...].astype(BF16)


def _cast_weights(*weights):
    flat = [w.reshape(-1, w.shape[-1]) for w in weights]
    specs = [pl.BlockSpec((w.shape[0] // CAST_STEPS, w.shape[1]), lambda i: (i, 0)) for w in flat]
    outs = pl.pallas_call(
        _cast_kernel,
        grid=(CAST_STEPS,),
        in_specs=specs,
        out_specs=specs,
        out_shape=[jax.ShapeDtypeStruct(w.shape, BF16) for w in flat],
        compiler_params=_params("arbitrary"),
        name="cast_weights",
    )(*flat)
    return [o.reshape(w.shape) for o, w in zip(outs, weights)]


ADALN_BLOCK = 2048


def _adaln_kernel(c_ref, w_ref, b_ref, o_ref):
    c = c_ref[...]
    s = c * jax.nn.sigmoid(c)
    o_ref[0] = jnp.dot(s.astype(BF16), w_ref[0].astype(BF16), preferred_element_type=F32) + b_ref[0]


def _adaln(cvec, w_mod, b_mod):
    nb = cvec.shape[0]
    n_out = 6 * D_MODEL
    blk = ADALN_BLOCK
    return pl.pallas_call(
        _adaln_kernel,
        grid=(DEPTH, n_out // blk),
        in_specs=[
            pl.BlockSpec((nb, D_MODEL), lambda l, n: (0, 0)),
            pl.BlockSpec((1, D_MODEL, blk), lambda l, n: (l, 0, n)),
            pl.BlockSpec((1, 1, blk), lambda l, n: (l, 0, n)),
        ],
        out_specs=pl.BlockSpec((1, nb, blk), lambda l, n: (l, 0, n)),
        out_shape=jax.ShapeDtypeStruct((DEPTH, nb, n_out), F32),
        compiler_params=_params("arbitrary", "arbitrary"),
        name="adaln",
    )(cvec, w_mod, b_mod.reshape(DEPTH, 1, n_out))


COL_CONV = 0
COL_QKV = 3 * CONV_W
COL_POOL = COL_QKV + 3 * ATTN_W
COL_GATE = COL_POOL + POOL_W
N_GATE_COLS = N_BRANCH * D_MODEL


def _inproj_kernel(x_ref, mod_ref, g_ref, w_ref, cw_ref, inv_ref, yconv_ref, qkv_ref, d_ref, gates_ref, *rest,
                   seq_len, with_kv):
    if with_kv:
        kv_ref, h_scr = rest
    else:
        (h_scr,) = rest

    def proj(col):
        return jnp.dot(h_scr[...], w_ref[:, col:col + IN_CHUNK], preferred_element_type=F32)

    parts = []
    for r in range(x_ref.shape[0] // NORM_ROWS):
        rows = slice(r * NORM_ROWS, (r + 1) * NORM_ROWS)
        h = _norm_mod(x_ref[rows, :], g_ref[...], mod_ref[0, 0:1, :], mod_ref[0, 1:2, :]).astype(BF16)
        h_scr[rows, :] = h
        parts.append(jnp.dot(h, w_ref[:, COL_CONV:COL_CONV + IN_CHUNK], preferred_element_type=F32))
    b_gate = jnp.concatenate(parts, axis=0)
    row = _row_in_seq((x_ref.shape[0], IN_CHUNK), seq_len)
    conv = _dwconv3(proj(COL_CONV + CONV_W) * proj(COL_CONV + 2 * CONV_W), cw_ref, row, seq_len)
    yconv_ref[...] = (b_gate * conv).astype(BF16)
    p = proj(COL_POOL)
    for c in range(N_GATE_COLS // IN_CHUNK):
        z = proj(COL_GATE + c * IN_CHUNK)
        if c < len(POOL_SIZES):
            lanes = slice(c * POOL_GROUP, (c + 1) * POOL_GROUP)
            d_ref[:, lanes] = _pool_delta(p[:, lanes], POOL_SIZES[c], inv_ref[:, c:c + 1], seq_len).astype(BF16)
        gates_ref[:, c * IN_CHUNK:(c + 1) * IN_CHUNK] = jax.nn.sigmoid(z).astype(BF16)
    for kv in range(2):
        z = proj(COL_QKV + (1 + kv) * ATTN_W)
        qkv_ref[:, (1 + kv) * ATTN_W:(2 + kv) * ATTN_W] = z.astype(BF16)
        if with_kv:
            for s in range(z.shape[0] // seq_len):
                for h_i in range(N_HEADS):
                    kv_ref[s, kv, h_i] = z[s * seq_len:(s + 1) * seq_len, h_i * HEAD_DIM:(h_i + 1) * HEAD_DIM]
    qkv_ref[:, 0:ATTN_W] = (proj(COL_QKV) * (HEAD_DIM ** -0.5)).astype(BF16)


def _inproj(x, mod, g1, w_in, conv_w, *, seq_len, tile, with_kv, layer):
    ntok = x.shape[0]
    row = lambda i: (i, 0)
    widths = (CONV_W, 3 * ATTN_W, POOL_W, N_GATE_COLS)
    out_shape = [jax.ShapeDtypeStruct((ntok, w), BF16) for w in widths]
    out_specs = [pl.BlockSpec((tile, w), row) for w in widths]
    if with_kv:
        one_layer = (2, N_HEADS, seq_len, HEAD_DIM)
        out_shape.append(jax.ShapeDtypeStruct((ntok // seq_len,) + one_layer, F32))
        out_specs.append(pl.BlockSpec((tile // seq_len,) + one_layer, lambda i: (i, 0, 0, 0, 0)))
    return pl.pallas_call(
        functools.partial(_inproj_kernel, seq_len=seq_len, with_kv=with_kv),
        grid=(ntok // tile,),
        in_specs=[
            pl.BlockSpec((tile, D_MODEL), row),
            _mod_spec(mod, ntok, tile),
            _resident((1, D_MODEL)),
            _resident_layer((D_MODEL, IN_W), layer),
            _resident((3, CONV_W)),
            _resident((tile, len(POOL_SIZES))),
        ],
        out_specs=out_specs,
        out_shape=out_shape,
        scratch_shapes=[pltpu.VMEM((tile, D_MODEL), BF16)],
        compiler_params=_params("arbitrary"),
        name="inproj_ctx" if with_kv else "inproj_lat",
    )(x, mod, g1, w_in, conv_w, _pool_inv_counts(tile, seq_len))


def _attend_pair(q2, kvb):
    lane = lax.broadcasted_iota(jnp.int32, (1, LANES), 1)
    out = None
    for half in range(2):
        m = (lane >= half * HEAD_DIM) & (lane < (half + 1) * HEAD_DIM)
        qh = jnp.where(m, q2, jnp.zeros_like(q2))
        scores = []
        for k2, _, bias in kvb:
            s = lax.dot_general(qh, k2, (((1,), (1,)), ((), ())), preferred_element_type=F32)
            if bias is not None:
                s = s + bias[half]
            scores.append(s)
        mx = functools.reduce(jnp.maximum, [jnp.max(s, axis=-1, keepdims=True) for s in scores])
        es = [jnp.exp(s - mx) for s in scores]
        den = functools.reduce(lambda a, b: a + b, [jnp.sum(e, axis=-1, keepdims=True) for e in es])
        acc = None
        for e, (_, v2, _) in zip(es, kvb):
            vh = jnp.where(m, v2, jnp.zeros_like(v2))
            o = jnp.dot(e.astype(BF16), vh, preferred_element_type=F32)
            acc = o if acc is None else acc + o
        acc = acc * (1.0 / den)
        out = acc if out is None else out + acc
    return out


CTX_ATTN_SEQS = 4


def _ctx_attn_kernel(q_ref, k_ref, v_ref, o_ref):
    for s in range(q_ref.shape[0] // CTX_LEN):
        rows = slice(s * CTX_LEN, (s + 1) * CTX_LEN)
        for hp in range(N_HEADS // 2):
            sl = slice(hp * LANES, (hp + 1) * LANES)
            o = _attend_pair(q_ref[rows, sl], [(k_ref[rows, sl], v_ref[rows, sl], None)])
            o_ref[rows, sl] = o.astype(BF16)


def _ctx_attention(qkv):
    ntok = qkv.shape[0]
    blk = CTX_ATTN_SEQS * CTX_LEN
    return pl.pallas_call(
        _ctx_attn_kernel,
        grid=(ntok // blk,),
        in_specs=[
            pl.BlockSpec((blk, ATTN_W), lambda b: (b, 0)),
            pl.BlockSpec((blk, ATTN_W), lambda b: (b, 1)),
            pl.BlockSpec((blk, ATTN_W), lambda b: (b, 2)),
        ],
        out_specs=pl.BlockSpec((blk, ATTN_W), lambda b: (b, 0)),
        out_shape=jax.ShapeDtypeStruct((ntok, ATTN_W), BF16),
        compiler_params=_params("arbitrary"),
        name="attn_ctx",
    )(qkv, qkv, qkv)


Q_ROWS = 4
Q_BLOCK = Q_ROWS * GRID_W
N_Q_BLOCKS = GRID_ROWS // Q_ROWS
KEY_ROWS = 12
N_KEYS = KEY_ROWS * GRID_W
N_REL_ROWS = 2 * WIN_H - 1
N_ROW_PAIRS = N_REL_ROWS + 1


def _first_key_row(m):
    return min(max(Q_ROWS * m - WIN_H // 2, 0), GRID_ROWS - KEY_ROWS)


def _fill_bias(rp_ref, bias_scr):
    shape = (N_ROW_PAIRS * GRID_W, LANES)
    qc = lax.broadcasted_iota(jnp.int32, shape, 0) & (GRID_W - 1)
    lane = lax.broadcasted_iota(jnp.int32, shape, 1)
    kc = lane & (GRID_W - 1)
    col_start = jnp.clip(qc - WIN_W // 2, 0, GRID_W - WIN_W)
    valid = (kc >= col_start) & (kc < col_start + WIN_W)
    first_half = lax.broadcasted_iota(jnp.int32, (GRID_W, LANES), 1) < GRID_W
    for h in range(N_HEADS):
        t = jnp.concatenate(
            [pltpu.roll(jnp.broadcast_to(rp_ref[h, p:p + 1, :], (GRID_W, LANES)), LANES - (WIN_W - 1), 1,
                        stride=1, stride_axis=0) for p in range(N_ROW_PAIRS)], axis=0)
        t = jnp.where(valid, t, -jnp.inf)
        for m in range(N_Q_BLOCKS):
            for ql in range(Q_ROWS):
                qr = Q_ROWS * m + ql
                win0 = min(max(qr - WIN_H // 2, 0), GRID_ROWS - WIN_H)
                for u in range(KEY_ROWS // 2):
                    kr = _first_key_row(m) + 2 * u
                    in0 = win0 <= kr < win0 + WIN_H
                    in1 = win0 <= kr + 1 < win0 + WIN_H
                    if in0 or in1:
                        pair = kr - qr + WIN_H
                        tile = t[pair * GRID_W:(pair + 1) * GRID_W]
                        if not in0:
                            tile = jnp.where(first_half, -jnp.inf, tile)
                        if not in1:
                            tile = jnp.where(first_half, tile, -jnp.inf)
                    else:
                        tile = jnp.full((GRID_W, LANES), -jnp.inf, F32)
                    bias_scr[h, m, ql * GRID_W:(ql + 1) * GRID_W, u * LANES:(u + 1) * LANES] = tile


def _lat_attn_kernel(q_ref, k_ref, v_ref, cache_ref, rp_ref, *rest, n_kv):
    kv_in, o_ref, bias_scr = rest[:n_kv], rest[n_kv], rest[-1]
    if n_kv:
        kv_out = rest[n_kv + 1]
        for layer, src in enumerate(kv_in):
            kv_out[0, layer] = src[0]
    hp = pl.program_id(0)
    m = pl.program_id(1)

    @pl.when((hp == 0) & (m == 0))
    def _():
        _fill_bias(rp_ref, bias_scr)

    def attend(m_idx, first_key, n_rows, col0=0):
        keys = pl.ds(first_key, n_rows * GRID_W)
        cols = slice(col0, col0 + n_rows * GRID_W)
        for s in range(q_ref.shape[0]):
            kvb = [
                (k_ref[s, keys, :], v_ref[s, keys, :],
                 (bias_scr[2 * hp, m_idx, :, cols], bias_scr[2 * hp + 1, m_idx, :, cols])),
                (cache_ref[0, s], cache_ref[1, s], None),
            ]
            o_ref[s] = _attend_pair(q_ref[s], kvb).astype(BF16)

    last = N_Q_BLOCKS - 1
    last_row0 = GRID_ROWS - WIN_H
    pl.when(m == 0)(lambda: attend(0, 0, WIN_H))
    pl.when(m == last)(lambda: attend(last, last_row0 * GRID_W, WIN_H,
                                      col0=(last_row0 - _first_key_row(last)) * GRID_W))

    @pl.when((m > 0) & (m < last))
    def _():
        row0 = jnp.clip(Q_ROWS * m - WIN_H // 2, 0, GRID_ROWS - KEY_ROWS)
        attend(m, pl.multiple_of(row0 * GRID_W, Q_BLOCK), KEY_ROWS)


def _lat_attention(qkv, cache_kv, layer, rp, kv_layers=()):
    nseq = qkv.shape[0] // LAT_LEN
    qkv = qkv.reshape(nseq, LAT_LEN, 3 * ATTN_W)
    n_hp = ATTN_W // LANES
    in_specs = [
        pl.BlockSpec((nseq, Q_BLOCK, LANES), lambda hp, m: (0, m, hp)),
        pl.BlockSpec((nseq, LAT_LEN, LANES), lambda hp, m: (0, 0, n_hp + hp)),
        pl.BlockSpec((nseq, LAT_LEN, LANES), lambda hp, m: (0, 0, 2 * n_hp + hp)),
        pl.BlockSpec((None, 2, nseq, PAST_LEN, LANES), lambda hp, m: (layer, 0, 0, 0, hp)),
        _resident(rp.shape),
    ]
    out_specs = [pl.BlockSpec((nseq, Q_BLOCK, LANES), lambda hp, m: (0, m, hp))]
    out_shape = [jax.ShapeDtypeStruct((nseq, LAT_LEN, ATTN_W), BF16)]
    if kv_layers:
        n_ctx, *slab = kv_layers[0].shape
        assert n_ctx == n_hp * N_Q_BLOCKS
        step = lambda hp, m: hp * N_Q_BLOCKS + m
        in_specs += [pl.BlockSpec((1, *slab), lambda hp, m: (step(hp, m), 0, 0, 0, 0))] * len(kv_layers)
        out_specs.append(pl.BlockSpec((1, len(kv_layers), *slab), lambda hp, m: (step(hp, m), 0, 0, 0, 0, 0)))
        out_shape.append(jax.ShapeDtypeStruct((n_ctx, len(kv_layers), *slab), F32))
    outs = pl.pallas_call(
        functools.partial(_lat_attn_kernel, n_kv=len(kv_layers)),
        grid=(n_hp, N_Q_BLOCKS),
        in_specs=in_specs,
        out_specs=out_specs,
        out_shape=out_shape,
        scratch_shapes=[pltpu.VMEM((N_HEADS, N_Q_BLOCKS, Q_BLOCK, N_KEYS), F32)],
        compiler_params=_params("arbitrary", "arbitrary"),
        name="attn_lat",
    )(qkv, qkv, qkv, cache_kv, rp, *kv_layers)
    y = outs[0].reshape(nseq * LAT_LEN, ATTN_W)
    return (y, outs[1]) if kv_layers else y


def _row_pair_table(rpb_l):
    lane_pad = GRID_W - rpb_l.shape[-1]
    lo = jnp.pad(rpb_l, ((0, 0), (1, 0), (0, lane_pad)))
    hi = jnp.pad(rpb_l, ((0, 0), (0, 1), (0, lane_pad)))
    return jnp.concatenate([lo, hi], axis=-1)


def _mix_kernel(x_ref, mod_ref, yc_ref, ya_ref, d_ref, gates_ref, pw_ref, ps_ref, wb_ref, wo_ref, o_ref):
    yp = [jnp.dot(d_ref[:, g * POOL_GROUP:(g + 1) * POOL_GROUP], pw_ref[g], preferred_element_type=F32)
          for g in range(len(POOL_SIZES))]
    y_pool = (jnp.concatenate(yp, axis=-1) * ps_ref[...]).astype(BF16)
    merged = None
    for i, y in enumerate((yc_ref[...], ya_ref[...], y_pool)):
        proj = jnp.dot(y, wb_ref[i], preferred_element_type=F32)
        term = gates_ref[:, i * D_MODEL:(i + 1) * D_MODEL].astype(F32) * proj
        merged = term if merged is None else merged + term
    mix = jnp.dot(merged.astype(BF16), wo_ref[...], preferred_element_type=F32)
    o_ref[...] = x_ref[...] + mod_ref[0, 2:3, :] * mix


def _mix(x, mod, yconv, yattn, d, gates, pool_w, pool_scale, w_branch, w_out, *, layer):
    ntok = x.shape[0]
    row = lambda i: (i, 0)
    return pl.pallas_call(
        _mix_kernel,
        grid=(ntok // MIX_TILE,),
        in_specs=[
            pl.BlockSpec((MIX_TILE, D_MODEL), row),
            _mod_spec(mod, ntok, MIX_TILE),
            pl.BlockSpec((MIX_TILE, CONV_W), row),
            pl.BlockSpec((MIX_TILE, ATTN_W), row),
            pl.BlockSpec((MIX_TILE, POOL_W), row),
            pl.BlockSpec((MIX_TILE, N_BRANCH * D_MODEL), row),
            _resident_layer((len(POOL_SIZES), POOL_GROUP, POOL_GROUP), layer),
            _resident((1, POOL_W)),
            _resident_layer((N_BRANCH, CONV_W, D_MODEL), layer),
            _resident_layer((D_MODEL, D_MODEL), layer),
        ],
        out_specs=pl.BlockSpec((MIX_TILE, D_MODEL), row),
        out_shape=jax.ShapeDtypeStruct((ntok, D_MODEL), F32),
        compiler_params=_params("arbitrary"),
        name="mix",
    )(x, mod, yconv, yattn, d, gates, pool_w, pool_scale, w_branch, w_out)


def _ffn_kernel(x_ref, mod_ref, g2_ref, wu_ref, fc_ref, wd_ref, gf_ref, o_ref, h_scr, act_scr,
                *, seq_len, final):
    u0, val0 = [], []
    for r in range(x_ref.shape[0] // NORM_ROWS):
        rows = slice(r * NORM_ROWS, (r + 1) * NORM_ROWS)
        h = _norm_mod(x_ref[rows, :], g2_ref[...], mod_ref[0, 3:4, :], mod_ref[0, 4:5, :]).astype(BF16)
        h_scr[rows, :] = h
        u0.append(jnp.dot(h, wu_ref[:, 0:FF_CHUNK], preferred_element_type=F32))
        val0.append(jnp.dot(h, wu_ref[:, D_FF:D_FF + FF_CHUNK], preferred_element_type=F32))
    row = _row_in_seq((x_ref.shape[0], FF_CHUNK), seq_len)
    for c in range(N_FF_CHUNKS):
        sl = slice(c * FF_CHUNK, (c + 1) * FF_CHUNK)
        sl_val = slice(D_FF + c * FF_CHUNK, D_FF + (c + 1) * FF_CHUNK)
        if c == 0:
            u, val = jnp.concatenate(u0, axis=0), jnp.concatenate(val0, axis=0)
        else:
            u = jnp.dot(h_scr[...], wu_ref[:, sl], preferred_element_type=F32)
            val = jnp.dot(h_scr[...], wu_ref[:, sl_val], preferred_element_type=F32)
        u = _dwconv3(u, fc_ref.at[:, sl], row, seq_len)
        act_scr[:, sl] = (jax.nn.gelu(u, approximate=True) * val).astype(BF16)
    y = jnp.dot(act_scr[...], wd_ref[...], preferred_element_type=F32)
    xn = x_ref[...] + mod_ref[0, 5:6, :] * y
    if final:
        ms = jnp.mean(xn * xn, axis=-1, keepdims=True)
        xn = xn * lax.rsqrt(ms + EPS) * gf_ref[...]
    o_ref[...] = xn


def _ffn(x, mod, g2, w_up, f_conv, w_down, g_final, *, seq_len, final, layer):
    ntok = x.shape[0]
    return pl.pallas_call(
        functools.partial(_ffn_kernel, seq_len=seq_len, final=final),
        grid=(ntok // TOK_TILE,),
        in_specs=[
            pl.BlockSpec((TOK_TILE, D_MODEL), lambda i: (i, 0)),
            _mod_spec(mod, ntok, TOK_TILE),
            _resident((1, D_MODEL)),
            _resident_layer((D_MODEL, 2 * D_FF), layer),
            _resident((3, D_FF)),
            _resident_layer((D_FF, D_MODEL), layer),
            _resident((1, D_MODEL)),
        ],
        out_specs=pl.BlockSpec((TOK_TILE, D_MODEL), lambda i: (i, 0)),
        out_shape=jax.ShapeDtypeStruct((ntok, D_MODEL), F32),
        scratch_shapes=[pltpu.VMEM((TOK_TILE, D_MODEL), BF16), pltpu.VMEM((TOK_TILE, D_FF), BF16)],
        compiler_params=_params("arbitrary"),
        name="ffn",
    )(x, mod, g2, w_up, f_conv, w_down, g_final)


def kernel(x_prompt, x_sample, cache_kv, c, c_ctx, w_mod, b_mod, g_norm1, g_norm2, w_in, conv_w, rpb,
           pool_w, pool_scale, w_branch, w_out, ffn_w_up, ffn_conv, ffn_w_down, g_final):
    n_mod = 8
    cvec = jnp.concatenate([c_ctx[None, :], c, jnp.zeros((n_mod - 1 - N_LAT_SEQ, D_MODEL), F32)], axis=0)
    mod = _adaln(cvec, w_mod, b_mod).reshape(DEPTH, n_mod, 6, D_MODEL)

    xp = x_prompt.reshape(N_CTX_SEQ * CTX_LEN, D_MODEL)
    xs = x_sample.reshape(N_LAT_SEQ * LAT_LEN, D_MODEL)
    gf = g_final.reshape(1, D_MODEL)
    cache_tok = cache_kv.transpose(1, 2, 0, 4, 3, 5).reshape(DEPTH, 2, N_LAT_SEQ, PAST_LEN, ATTN_W).astype(BF16)
    w_in_b, pool_w_b, w_br_b, w_out_b, w_up_b, w_dn_b = _cast_weights(
        w_in, pool_w, w_branch, w_out, ffn_w_up, ffn_w_down)
    kv_layers = []
    kv_state = None
    for l in range(DEPTH):
        pool_s_l = pool_scale[l].reshape(1, POOL_W)
        g1 = g_norm1[l].reshape(1, D_MODEL)
        g2 = g_norm2[l].reshape(1, D_MODEL)
        mod_ctx = mod[l, 0:1]
        mod_lat = mod[l, 1:1 + N_LAT_SEQ]
        final = l == DEPTH - 1

        yc, qkv, d, gates, kv = _inproj(xp, mod_ctx, g1, w_in_b, conv_w[l], seq_len=CTX_LEN, tile=CTX_IN_TILE,
                                        with_kv=True, layer=l)
        kv_layers.append(kv)
        ya = _ctx_attention(qkv)
        xp = _mix(xp, mod_ctx, yc, ya, d, gates, pool_w_b, pool_s_l, w_br_b, w_out_b, layer=l)
        xp = _ffn(xp, mod_ctx, g2, w_up_b, ffn_conv[l], w_dn_b, gf, seq_len=CTX_LEN, final=final, layer=l)

        yc, qkv, d, gates = _inproj(xs, mod_lat, g1, w_in_b, conv_w[l], seq_len=LAT_LEN, tile=TOK_TILE,
                                    with_kv=False, layer=l)
        if final:
            ya, kv_state = _lat_attention(qkv, cache_tok, l, _row_pair_table(rpb[l]), tuple(kv_layers))
        else:
            ya = _lat_attention(qkv, cache_tok, l, _row_pair_table(rpb[l]))
        xs = _mix(xs, mod_lat, yc, ya, d, gates, pool_w_b, pool_s_l, w_br_b, w_out_b, layer=l)
        xs = _ffn(xs, mod_lat, g2, w_up_b, ffn_conv[l], w_dn_b, gf, seq_len=LAT_LEN, final=final, layer=l)

    y_prompt = xp.reshape(N_CTX_SEQ, CTX_LEN, D_MODEL)
    y_sample = xs.reshape(N_LAT_SEQ, LAT_LEN, D_MODEL)
    return (y_prompt, y_sample, kv_state)
```

```python
import functools

import numpy as np
import jax
import jax.numpy as jnp
from jax import lax
from jax.experimental import pallas as pl
from jax.experimental.pallas import tpu as pltpu

F32 = jnp.float32
BF16 = jnp.bfloat16

D_MODEL = 1024
N_CTX_SEQ = 16
CTX_LEN = 256
DEPTH = 2
N_LAT_SEQ = 4
LAT_LEN = 1024
PAST_LEN = 256
GRID_W = 64
GRID_ROWS = LAT_LEN // GRID_W
CONV_W = 512
N_HEADS = 8
HEAD_DIM = 64
ATTN_W = N_HEADS * HEAD_DIM
POOL_W = 512
POOL_SIZES = (2, 4, 8, 16)
POOL_GROUP = POOL_W // len(POOL_SIZES)
N_BRANCH = 3
WIN_H = 8
WIN_W = 16
D_FF = 2816
EPS = 1e-6
IN_W = 3 * CONV_W + 3 * ATTN_W + POOL_W + N_BRANCH * D_MODEL

LANES = 128
TOK_TILE = 1024
CTX_IN_TILE = 512
IN_CHUNK = 512
FF_CHUNK = 256
N_FF_CHUNKS = D_FF // FF_CHUNK
MIX_TILE = 1024
NORM_ROWS = 256
VMEM_LIMIT = 56 * 1024 * 1024


def _params(*sem):
    return pltpu.CompilerParams(dimension_semantics=sem, vmem_limit_bytes=VMEM_LIMIT)


def _resident(shape):
    return pl.BlockSpec(shape, lambda *_: (0,) * len(shape), pipeline_mode=pl.Buffered(1))


def _resident_layer(shape, layer):
    return pl.BlockSpec((None,) + shape, lambda *_: (layer,) + (0,) * len(shape), pipeline_mode=pl.Buffered(1))


def _mod_spec(mod, ntok, tile):
    tiles_per_mod = ntok // mod.shape[0] // tile
    return pl.BlockSpec((1, 6, D_MODEL), lambda i, *_: (i // tiles_per_mod, 0, 0))


def _norm_mod(x, g, shift, scale):
    ms = jnp.mean(x * x, axis=-1, keepdims=True)
    return (x * lax.rsqrt(ms + EPS) * g) * (1.0 + scale) + shift


def _row_in_seq(shape, seq_len):
    return lax.broadcasted_iota(jnp.int32, shape, 0) & (seq_len - 1)


def _shift_rows(x, s, row, seq_len):
    y = pltpu.roll(x, s % x.shape[0], 0)
    ok = (row >= s) if s > 0 else (row < seq_len + s)
    return jnp.where(ok, y, 0.0)


def _dwconv3(x, w_ref, row, seq_len):
    return (_shift_rows(x, 1, row, seq_len) * w_ref[0:1, :] + x * w_ref[1:2, :]
            + _shift_rows(x, -1, row, seq_len) * w_ref[2:3, :])


def _pool_inv_counts(tile, seq_len):
    r = np.arange(tile) % seq_len
    cols = []
    for w in POOL_SIZES:
        lo = np.maximum(r - w // 2, 0)
        hi = np.minimum(r - w // 2 + w, seq_len)
        cols.append(1.0 / (hi - lo))
    return jnp.asarray(np.stack(cols, axis=1), F32)


def _pool_delta(pg, w, inv_cnt, seq_len):
    row = _row_in_seq(pg.shape, seq_len)
    fwd = bwd = pg
    k = 1
    while k < w // 2:
        fwd = fwd + _shift_rows(fwd, -k, row, seq_len)
        bwd = bwd + _shift_rows(bwd, k, row, seq_len)
        k *= 2
    acc = fwd + _shift_rows(bwd, 1, row, seq_len)
    return acc * inv_cnt - pg


CAST_STEPS = 16


def _cast_kernel(*refs):
    n = len(refs) // 2
    for src, dst in zip(refs[:n], refs[n:]):
        dst[...] = src[...].astype(BF16)


def _cast_weights(*weights):
    flat = [w.reshape(-1, w.shape[-1]) for w in weights]
    specs = [pl.BlockSpec((w.shape[0] // CAST_STEPS, w.shape[1]), lambda i: (i, 0)) for w in flat]
    outs = pl.pallas_call(
        _cast_kernel,
        grid=(CAST_STEPS,),
        in_specs=specs,
        out_specs=specs,
        out_shape=[jax.ShapeDtypeStruct(w.shape, BF16) for w in flat],
        compiler_params=_params("arbitrary"),
        name="cast_weights",
    )(*flat)
    return [o.reshape(w.shape) for o, w in zip(outs, weights)]


ADALN_BLOCK = 2048


def _adaln_kernel(c_ref, w_ref, b_ref, o_ref):
    c = c_ref[...]
    s = c * jax.nn.sigmoid(c)
    o_ref[0] = jnp.dot(s.astype(BF16), w_ref[0].astype(BF16), preferred_element_type=F32) + b_ref[0]


def _adaln(cvec, w_mod, b_mod):
    nb = cvec.shape[0]
    n_out = 6 * D_MODEL
    blk = ADALN_BLOCK
    return pl.pallas_call(
        _adaln_kernel,
        grid=(DEPTH, n_out // blk),
        in_specs=[
            pl.BlockSpec((nb, D_MODEL), lambda l, n: (0, 0)),
            pl.BlockSpec((1, D_MODEL, blk), lambda l, n: (l, 0, n)),
            pl.BlockSpec((1, 1, blk), lambda l, n: (l, 0, n)),
        ],
        out_specs=pl.BlockSpec((1, nb, blk), lambda l, n: (l, 0, n)),
        out_shape=jax.ShapeDtypeStruct((DEPTH, nb, n_out), F32),
        compiler_params=_params("arbitrary", "arbitrary"),
        name="adaln",
    )(cvec, w_mod, b_mod.reshape(DEPTH, 1, n_out))


COL_CONV = 0
COL_QKV = 3 * CONV_W
COL_POOL = COL_QKV + 3 * ATTN_W
COL_GATE = COL_POOL + POOL_W
N_GATE_COLS = N_BRANCH * D_MODEL


def _inproj_kernel(x_ref, mod_ref, g_ref, w_ref, cw_ref, inv_ref, yconv_ref, qkv_ref, d_ref, gates_ref, *rest,
                   seq_len, with_kv):
    if with_kv:
        kv_ref, h_scr = rest
    else:
        (h_scr,) = rest

    def proj(col):
        return jnp.dot(h_scr[...], w_ref[:, col:col + IN_CHUNK], preferred_element_type=F32)

    parts = []
    for r in range(x_ref.shape[0] // NORM_ROWS):
        rows = slice(r * NORM_ROWS, (r + 1) * NORM_ROWS)
        h = _norm_mod(x_ref[rows, :], g_ref[...], mod_ref[0, 0:1, :], mod_ref[0, 1:2, :]).astype(BF16)
        h_scr[rows, :] = h
        parts.append(jnp.dot(h, w_ref[:, COL_CONV:COL_CONV + IN_CHUNK], preferred_element_type=F32))
    b_gate = jnp.concatenate(parts, axis=0)
    row = _row_in_seq((x_ref.shape[0], IN_CHUNK), seq_len)
    conv = _dwconv3(proj(COL_CONV + CONV_W) * proj(COL_CONV + 2 * CONV_W), cw_ref, row, seq_len)
    yconv_ref[...] = (b_gate * conv).astype(BF16)
    p = proj(COL_POOL)
    for c in range(N_GATE_COLS // IN_CHUNK):
        z = proj(COL_GATE + c * IN_CHUNK)
        if c < len(POOL_SIZES):
            lanes = slice(c * POOL_GROUP, (c + 1) * POOL_GROUP)
            d_ref[:, lanes] = _pool_delta(p[:, lanes], POOL_SIZES[c], inv_ref[:, c:c + 1], seq_len).astype(BF16)
        gates_ref[:, c * IN_CHUNK:(c + 1) * IN_CHUNK] = jax.nn.sigmoid(z).astype(BF16)
    for kv in range(2):
        z = proj(COL_QKV + (1 + kv) * ATTN_W)
        qkv_ref[:, (1 + kv) * ATTN_W:(2 + kv) * ATTN_W] = z.astype(BF16)
        if with_kv:
            for s in range(z.shape[0] // seq_len):
                for h_i in range(N_HEADS):
                    kv_ref[s, kv, h_i] = z[s * seq_len:(s + 1) * seq_len, h_i * HEAD_DIM:(h_i + 1) * HEAD_DIM]
    qkv_ref[:, 0:ATTN_W] = (proj(COL_QKV) * (HEAD_DIM ** -0.5)).astype(BF16)


def _inproj(x, mod, g1, w_in, conv_w, *, seq_len, tile, with_kv, layer):
    ntok = x.shape[0]
    row = lambda i: (i, 0)
    widths = (CONV_W, 3 * ATTN_W, POOL_W, N_GATE_COLS)
    out_shape = [jax.ShapeDtypeStruct((ntok, w), BF16) for w in widths]
    out_specs = [pl.BlockSpec((tile, w), row) for w in widths]
    if with_kv:
        one_layer = (2, N_HEADS, seq_len, HEAD_DIM)
        out_shape.append(jax.ShapeDtypeStruct((ntok // seq_len,) + one_layer, F32))
        out_specs.append(pl.BlockSpec((tile // seq_len,) + one_layer, lambda i: (i, 0, 0, 0, 0)))
    return pl.pallas_call(
        functools.partial(_inproj_kernel, seq_len=seq_len, with_kv=with_kv),
        grid=(ntok // tile,),
        in_specs=[
            pl.BlockSpec((tile, D_MODEL), row),
            _mod_spec(mod, ntok, tile),
            _resident((1, D_MODEL)),
            _resident_layer((D_MODEL, IN_W), layer),
            _resident((3, CONV_W)),
            _resident((tile, len(POOL_SIZES))),
        ],
        out_specs=out_specs,
        out_shape=out_shape,
        scratch_shapes=[pltpu.VMEM((tile, D_MODEL), BF16)],
        compiler_params=_params("arbitrary"),
        name="inproj_ctx" if with_kv else "inproj_lat",
    )(x, mod, g1, w_in, conv_w, _pool_inv_counts(tile, seq_len))


def _attend_pair(q2, kvb):
    lane = lax.broadcasted_iota(jnp.int32, (1, LANES), 1)
    out = None
    for half in range(2):
        m = (lane >= half * HEAD_DIM) & (lane < (half + 1) * HEAD_DIM)
        qh = jnp.where(m, q2, jnp.zeros_like(q2))
        scores = []
        for k2, _, bias in kvb:
            s = lax.dot_general(qh, k2, (((1,), (1,)), ((), ())), preferred_element_type=F32)
            if bias is not None:
                s = s + bias[half]
            scores.append(s)
        mx = functools.reduce(jnp.maximum, [jnp.max(s, axis=-1, keepdims=True) for s in scores])
        es = [jnp.exp(s - mx) for s in scores]
        den = functools.reduce(lambda a, b: a + b, [jnp.sum(e, axis=-1, keepdims=True) for e in es])
        acc = None
        for e, (_, v2, _) in zip(es, kvb):
            vh = jnp.where(m, v2, jnp.zeros_like(v2))
            o = jnp.dot(e.astype(BF16), vh, preferred_element_type=F32)
            acc = o if acc is None else acc + o
        acc = acc * (1.0 / den)
        out = acc if out is None else out + acc
    return out


Q_ROWS = 4
Q_BLOCK = Q_ROWS * GRID_W
N_Q_BLOCKS = GRID_ROWS // Q_ROWS
KEY_ROWS = 12
N_KEYS = KEY_ROWS * GRID_W
N_REL_ROWS = 2 * WIN_H - 1
N_ROW_PAIRS = N_REL_ROWS + 1


def _first_key_row(m):
    return min(max(Q_ROWS * m - WIN_H // 2, 0), GRID_ROWS - KEY_ROWS)


def _fill_bias(rp_ref, bias_scr):
    shape = (N_ROW_PAIRS * GRID_W, LANES)
    qc = lax.broadcasted_iota(jnp.int32, shape, 0) & (GRID_W - 1)
    lane = lax.broadcasted_iota(jnp.int32, shape, 1)
    kc = lane & (GRID_W - 1)
    col_start = jnp.clip(qc - WIN_W // 2, 0, GRID_W - WIN_W)
    valid = (kc >= col_start) & (kc < col_start + WIN_W)
    first_half = lax.broadcasted_iota(jnp.int32, (GRID_W, LANES), 1) < GRID_W
    for h in range(N_HEADS):
        t = jnp.concatenate(
            [pltpu.roll(jnp.broadcast_to(rp_ref[h, p:p + 1, :], (GRID_W, LANES)), LANES - (WIN_W - 1), 1,
                        stride=1, stride_axis=0) for p in range(N_ROW_PAIRS)], axis=0)
        t = jnp.where(valid, t, -jnp.inf)
        for m in range(N_Q_BLOCKS):
            for ql in range(Q_ROWS):
                qr = Q_ROWS * m + ql
                win0 = min(max(qr - WIN_H // 2, 0), GRID_ROWS - WIN_H)
                for u in range(KEY_ROWS // 2):
                    kr = _first_key_row(m) + 2 * u
                    in0 = win0 <= kr < win0 + WIN_H
                    in1 = win0 <= kr + 1 < win0 + WIN_H
                    if in0 or in1:
                        pair = kr - qr + WIN_H
                        tile = t[pair * GRID_W:(pair + 1) * GRID_W]
                        if not in0:
                            tile = jnp.where(first_half, -jnp.inf, tile)
                        if not in1:
                            tile = jnp.where(first_half, tile, -jnp.inf)
                    else:
                        tile = jnp.full((GRID_W, LANES), -jnp.inf, F32)
                    bias_scr[h, m, ql * GRID_W:(ql + 1) * GRID_W, u * LANES:(u + 1) * LANES] = tile


def _lat_attn_kernel(q_ref, k_ref, v_ref, cache_ref, rp_ref, *rest, n_kv):
    kv_in, o_ref, bias_scr = rest[:n_kv], rest[n_kv], rest[-1]
    if n_kv:
        kv_out = rest[n_kv + 1]
        for layer, src in enumerate(kv_in):
            kv_out[0, layer] = src[0]
    hp = pl.program_id(0)
    m = pl.program_id(1)

    @pl.when((hp == 0) & (m == 0))
    def _():
        _fill_bias(rp_ref, bias_scr)

    def attend(m_idx, first_key, n_rows, col0=0):
        keys = pl.ds(first_key, n_rows * GRID_W)
        cols = slice(col0, col0 + n_rows * GRID_W)
        for s in range(q_ref.shape[0]):
            kvb = [
                (k_ref[s, keys, :], v_ref[s, keys, :],
                 (bias_scr[2 * hp, m_idx, :, cols], bias_scr[2 * hp + 1, m_idx, :, cols])),
                (cache_ref[0, s], cache_ref[1, s], None),
            ]
            o_ref[s] = _attend_pair(q_ref[s], kvb).astype(BF16)

    last = N_Q_BLOCKS - 1
    last_row0 = GRID_ROWS - WIN_H
    pl.when(m == 0)(lambda: attend(0, 0, WIN_H))
    pl.when(m == last)(lambda: attend(last, last_row0 * GRID_W, WIN_H,
                                      col0=(last_row0 - _first_key_row(last)) * GRID_W))

    @pl.when((m > 0) & (m < last))
    def _():
        row0 = jnp.clip(Q_ROWS * m - WIN_H // 2, 0, GRID_ROWS - KEY_ROWS)
        attend(m, pl.multiple_of(row0 * GRID_W, Q_BLOCK), KEY_ROWS)


def _lat_attention(qkv, cache_kv, layer, rp, kv_layers=()):
    nseq = qkv.shape[0] // LAT_LEN
    qkv = qkv.reshape(nseq, LAT_LEN, 3 * ATTN_W)
    n_hp = ATTN_W // LANES
    in_specs = [
        pl.BlockSpec((nseq, Q_BLOCK, LANES), lambda hp, m: (0, m, hp)),
        pl.BlockSpec((nseq, LAT_LEN, LANES), lambda hp, m: (0, 0, n_hp + hp)),
        pl.BlockSpec((nseq, LAT_LEN, LANES), lambda hp, m: (0, 0, 2 * n_hp + hp)),
        pl.BlockSpec((None, 2, nseq, PAST_LEN, LANES), lambda hp, m: (layer, 0, 0, 0, hp)),
        _resident(rp.shape),
    ]
    out_specs = [pl.BlockSpec((nseq, Q_BLOCK, LANES), lambda hp, m: (0, m, hp))]
    out_shape = [jax.ShapeDtypeStruct((nseq, LAT_LEN, ATTN_W), BF16)]
    if kv_layers:
        n_ctx, *slab = kv_layers[0].shape
        assert n_ctx == n_hp * N_Q_BLOCKS
        step = lambda hp, m: hp * N_Q_BLOCKS + m
        in_specs += [pl.BlockSpec((1, *slab), lambda hp, m: (step(hp, m), 0, 0, 0, 0))] * len(kv_layers)
        out_specs.append(pl.BlockSpec((1, len(kv_layers), *slab), lambda hp, m: (step(hp, m), 0, 0, 0, 0, 0)))
        out_shape.append(jax.ShapeDtypeStruct((n_ctx, len(kv_layers), *slab), F32))
    outs = pl.pallas_call(
        functools.partial(_lat_attn_kernel, n_kv=len(kv_layers)),
        grid=(n_hp, N_Q_BLOCKS),
        in_specs=in_specs,
        out_specs=out_specs,
        out_shape=out_shape,
        scratch_shapes=[pltpu.VMEM((N_HEADS, N_Q_BLOCKS, Q_BLOCK, N_KEYS), F32)],
        compiler_params=_params("arbitrary", "arbitrary"),
        name="attn_lat",
    )(qkv, qkv, qkv, cache_kv, rp, *kv_layers)
    y = outs[0].reshape(nseq * LAT_LEN, ATTN_W)
    return (y, outs[1]) if kv_layers else y


def _row_pair_table(rpb_l):
    lane_pad = GRID_W - rpb_l.shape[-1]
    lo = jnp.pad(rpb_l, ((0, 0), (1, 0), (0, lane_pad)))
    hi = jnp.pad(rpb_l, ((0, 0), (0, 1), (0, lane_pad)))
    return jnp.concatenate([lo, hi], axis=-1)


def _mix_kernel(x_ref, mod_ref, yc_ref, ya_ref, d_ref, gates_ref, pw_ref, ps_ref, wb_ref, wo_ref, o_ref, *scratch,
                ctx_attention):
    if ctx_attention:
        (ya_scr,) = scratch
        for s in range(x_ref.shape[0] // CTX_LEN):
            rows = slice(s * CTX_LEN, (s + 1) * CTX_LEN)
            for hp in range(N_HEADS // 2):
                q, k, v = (ya_ref[rows, i * ATTN_W + hp * LANES:i * ATTN_W + (hp + 1) * LANES] for i in range(3))
                ya_scr[rows, hp * LANES:(hp + 1) * LANES] = _attend_pair(q, [(k, v, None)]).astype(BF16)
        ya_ref = ya_scr
    yp = [jnp.dot(d_ref[:, g * POOL_GROUP:(g + 1) * POOL_GROUP], pw_ref[g], preferred_element_type=F32)
          for g in range(len(POOL_SIZES))]
    y_pool = (jnp.concatenate(yp, axis=-1) * ps_ref[...]).astype(BF16)
    merged = None
    for i, y in enumerate((yc_ref[...], ya_ref[...], y_pool)):
        proj = jnp.dot(y, wb_ref[i], preferred_element_type=F32)
        term = gates_ref[:, i * D_MODEL:(i + 1) * D_MODEL].astype(F32) * proj
        merged = term if merged is None else merged + term
    mix = jnp.dot(merged.astype(BF16), wo_ref[...], preferred_element_type=F32)
    o_ref[...] = x_ref[...] + mod_ref[0, 2:3, :] * mix


def _mix(x, mod, yconv, yattn, d, gates, pool_w, pool_scale, w_branch, w_out, *, layer, ctx_attention=False):
    ntok = x.shape[0]
    row = lambda i: (i, 0)
    return pl.pallas_call(
        functools.partial(_mix_kernel, ctx_attention=ctx_attention),
        grid=(ntok // MIX_TILE,),
        in_specs=[
            pl.BlockSpec((MIX_TILE, D_MODEL), row),
            _mod_spec(mod, ntok, MIX_TILE),
            pl.BlockSpec((MIX_TILE, CONV_W), row),
            pl.BlockSpec((MIX_TILE, yattn.shape[1]), row),
            pl.BlockSpec((MIX_TILE, POOL_W), row),
            pl.BlockSpec((MIX_TILE, N_BRANCH * D_MODEL), row),
            _resident_layer((len(POOL_SIZES), POOL_GROUP, POOL_GROUP), layer),
            _resident((1, POOL_W)),
            _resident_layer((N_BRANCH, CONV_W, D_MODEL), layer),
            _resident_layer((D_MODEL, D_MODEL), layer),
        ],
        out_specs=pl.BlockSpec((MIX_TILE, D_MODEL), row),
        out_shape=jax.ShapeDtypeStruct((ntok, D_MODEL), F32),
        scratch_shapes=[pltpu.VMEM((MIX_TILE, ATTN_W), BF16)] if ctx_attention else [],
        compiler_params=_params("arbitrary"),
        name="mix",
    )(x, mod, yconv, yattn, d, gates, pool_w, pool_scale, w_branch, w_out)


def _ffn_kernel(x_ref, mod_ref, g2_ref, wu_ref, fc_ref, wd_ref, gf_ref, o_ref, h_scr, act_scr,
                *, seq_len, final):
    u0, val0 = [], []
    for r in range(x_ref.shape[0] // NORM_ROWS):
        rows = slice(r * NORM_ROWS, (r + 1) * NORM_ROWS)
        h = _norm_mod(x_ref[rows, :], g2_ref[...], mod_ref[0, 3:4, :], mod_ref[0, 4:5, :]).astype(BF16)
        h_scr[rows, :] = h
        u0.append(jnp.dot(h, wu_ref[:, 0:FF_CHUNK], preferred_element_type=F32))
        val0.append(jnp.dot(h, wu_ref[:, D_FF:D_FF + FF_CHUNK], preferred_element_type=F32))
    row = _row_in_seq((x_ref.shape[0], FF_CHUNK), seq_len)
    for c in range(N_FF_CHUNKS):
        sl = slice(c * FF_CHUNK, (c + 1) * FF_CHUNK)
        sl_val = slice(D_FF + c * FF_CHUNK, D_FF + (c + 1) * FF_CHUNK)
        if c == 0:
            u, val = jnp.concatenate(u0, axis=0), jnp.concatenate(val0, axis=0)
        else:
            u = jnp.dot(h_scr[...], wu_ref[:, sl], preferred_element_type=F32)
            val = jnp.dot(h_scr[...], wu_ref[:, sl_val], preferred_element_type=F32)
        u = _dwconv3(u, fc_ref.at[:, sl], row, seq_len)
        act_scr[:, sl] = (jax.nn.gelu(u, approximate=True) * val).astype(BF16)
    y = jnp.dot(act_scr[...], wd_ref[...], preferred_element_type=F32)
    xn = x_ref[...] + mod_ref[0, 5:6, :] * y
    if final:
        ms = jnp.mean(xn * xn, axis=-1, keepdims=True)
        xn = xn * lax.rsqrt(ms + EPS) * gf_ref[...]
    o_ref[...] = xn


def _ffn(x, mod, g2, w_up, f_conv, w_down, g_final, *, seq_len, final, layer):
    ntok = x.shape[0]
    return pl.pallas_call(
        functools.partial(_ffn_kernel, seq_len=seq_len, final=final),
        grid=(ntok // TOK_TILE,),
        in_specs=[
            pl.BlockSpec((TOK_TILE, D_MODEL), lambda i: (i, 0)),
            _mod_spec(mod, ntok, TOK_TILE),
            _resident((1, D_MODEL)),
            _resident_layer((D_MODEL, 2 * D_FF), layer),
            _resident((3, D_FF)),
            _resident_layer((D_FF, D_MODEL), layer),
            _resident((1, D_MODEL)),
        ],
        out_specs=pl.BlockSpec((TOK_TILE, D_MODEL), lambda i: (i, 0)),
        out_shape=jax.ShapeDtypeStruct((ntok, D_MODEL), F32),
        scratch_shapes=[pltpu.VMEM((TOK_TILE, D_MODEL), BF16), pltpu.VMEM((TOK_TILE, D_FF), BF16)],
        compiler_params=_params("arbitrary"),
        name="ffn",
    )(x, mod, g2, w_up, f_conv, w_down, g_final)


def kernel(x_prompt, x_sample, cache_kv, c, c_ctx, w_mod, b_mod, g_norm1, g_norm2, w_in, conv_w, rpb,
           pool_w, pool_scale, w_branch, w_out, ffn_w_up, ffn_conv, ffn_w_down, g_final):
    n_mod = 8
    cvec = jnp.concatenate([c_ctx[None, :], c, jnp.zeros((n_mod - 1 - N_LAT_SEQ, D_MODEL), F32)], axis=0)
    mod = _adaln(cvec, w_mod, b_mod).reshape(DEPTH, n_mod, 6, D_MODEL)

    xp = x_prompt.reshape(N_CTX_SEQ * CTX_LEN, D_MODEL)
    xs = x_sample.reshape(N_LAT_SEQ * LAT_LEN, D_MODEL)
    gf = g_final.reshape(1, D_MODEL)
    cache_tok = cache_kv.transpose(1, 2, 0, 4, 3, 5).reshape(DEPTH, 2, N_LAT_SEQ, PAST_LEN, ATTN_W).astype(BF16)
    w_in_b, pool_w_b, w_br_b, w_out_b, w_up_b, w_dn_b = _cast_weights(
        w_in, pool_w, w_branch, w_out, ffn_w_up, ffn_w_down)
    kv_layers = []
    kv_state = None
    for l in range(DEPTH):
        pool_s_l = pool_scale[l].reshape(1, POOL_W)
        g1 = g_norm1[l].reshape(1, D_MODEL)
        g2 = g_norm2[l].reshape(1, D_MODEL)
        mod_ctx = mod[l, 0:1]
        mod_lat = mod[l, 1:1 + N_LAT_SEQ]
        final = l == DEPTH - 1

        yc, qkv, d, gates, kv = _inproj(xp, mod_ctx, g1, w_in_b, conv_w[l], seq_len=CTX_LEN, tile=CTX_IN_TILE,
                                        with_kv=True, layer=l)
        kv_layers.append(kv)
        xp = _mix(xp, mod_ctx, yc, qkv, d, gates, pool_w_b, pool_s_l, w_br_b, w_out_b, layer=l, ctx_attention=True)
        xp = _ffn(xp, mod_ctx, g2, w_up_b, ffn_conv[l], w_dn_b, gf, seq_len=CTX_LEN, final=final, layer=l)

        yc, qkv, d, gates = _inproj(xs, mod_lat, g1, w_in_b, conv_w[l], seq_len=LAT_LEN, tile=TOK_TILE,
                                    with_kv=False, layer=l)
        if final:
            ya, kv_state = _lat_attention(qkv, cache_tok, l, _row_pair_table(rpb[l]), tuple(kv_layers))
        else:
            ya = _lat_attention(qkv, cache_tok, l, _row_pair_table(rpb[l]))
        xs = _mix(xs, mod_lat, yc, ya, d, gates, pool_w_b, pool_s_l, w_br_b, w_out_b, layer=l)
        xs = _ffn(xs, mod_lat, g2, w_up_b, ffn_conv[l], w_dn_b, gf, seq_len=LAT_LEN, final=final, layer=l)

    y_prompt = xp.reshape(N_CTX_SEQ, CTX_LEN, D_MODEL)
    y_sample = xs.reshape(N_LAT_SEQ, LAT_LEN, D_MODEL)
    return (y_prompt, y_sample, kv_state)
```

```python
import functools

import numpy as np
import jax
import jax.numpy as jnp
from jax import lax
from jax.experimental import pallas as pl
from jax.experimental.pallas import tpu as pltpu

F32 = jnp.float32
BF16 = jnp.bfloat16

D_MODEL = 1024
N_CTX_SEQ = 16
CTX_LEN = 256
DEPTH = 2
N_LAT_SEQ = 4
LAT_LEN = 1024
PAST_LEN = 256
GRID_W = 64
GRID_ROWS = LAT_LEN // GRID_W
CONV_W = 512
N_HEADS = 8
HEAD_DIM = 64
ATTN_W = N_HEADS * HEAD_DIM
POOL_W = 512
POOL_SIZES = (2, 4, 8, 16)
POOL_GROUP = POOL_W // len(POOL_SIZES)
N_BRANCH = 3
WIN_H = 8
WIN_W = 16
D_FF = 2816
EPS = 1e-6
IN_W = 3 * CONV_W + 3 * ATTN_W + POOL_W + N_BRANCH * D_MODEL

LANES = 128
TOK_TILE = 1024
CTX_IN_TILE = 512
IN_CHUNK = 512
FF_CHUNK = 256
N_FF_CHUNKS = D_FF // FF_CHUNK
MIX_TILE = 1024
NORM_ROWS = 256
VMEM_LIMIT = 56 * 1024 * 1024


def _params(*sem):
    return pltpu.CompilerParams(dimension_semantics=sem, vmem_limit_bytes=VMEM_LIMIT)


def _resident(shape):
    return pl.BlockSpec(shape, lambda *_: (0,) * len(shape), pipeline_mode=pl.Buffered(1))


def _resident_layer(shape, layer):
    return pl.BlockSpec((None,) + shape, lambda *_: (layer,) + (0,) * len(shape), pipeline_mode=pl.Buffered(1))


def _mod_spec(mod, ntok, tile):
    tiles_per_mod = ntok // mod.shape[0] // tile
    return pl.BlockSpec((1, 6, D_MODEL), lambda i, *_: (i // tiles_per_mod, 0, 0))


def _norm_mod(x, g, shift, scale):
    ms = jnp.mean(x * x, axis=-1, keepdims=True)
    return (x * lax.rsqrt(ms + EPS) * g) * (1.0 + scale) + shift


def _row_in_seq(shape, seq_len):
    return lax.broadcasted_iota(jnp.int32, shape, 0) & (seq_len - 1)


def _shift_rows(x, s, row, seq_len):
    y = pltpu.roll(x, s % x.shape[0], 0)
    ok = (row >= s) if s > 0 else (row < seq_len + s)
    return jnp.where(ok, y, 0.0)


def _dwconv3(x, w_ref, row, seq_len):
    return (_shift_rows(x, 1, row, seq_len) * w_ref[0:1, :] + x * w_ref[1:2, :]
            + _shift_rows(x, -1, row, seq_len) * w_ref[2:3, :])


def _pool_inv_counts(tile, seq_len):
    r = np.arange(tile) % seq_len
    cols = []
    for w in POOL_SIZES:
        lo = np.maximum(r - w // 2, 0)
        hi = np.minimum(r - w // 2 + w, seq_len)
        cols.append(1.0 / (hi - lo))
    return jnp.asarray(np.stack(cols, axis=1), F32)


def _pool_delta(pg, w, inv_cnt, seq_len):
    row = _row_in_seq(pg.shape, seq_len)
    fwd = bwd = pg
    k = 1
    while k < w // 2:
        fwd = fwd + _shift_rows(fwd, -k, row, seq_len)
        bwd = bwd + _shift_rows(bwd, k, row, seq_len)
        k *= 2
    acc = fwd + _shift_rows(bwd, 1, row, seq_len)
    return acc * inv_cnt - pg


CAST_STEPS = 16


def _cast_kernel(*refs):
    n = len(refs) // 2
    for src, dst in zip(refs[:n], refs[n:]):
        dst[...] = src[...].astype(BF16)


def _cast_weights(*weights):
    flat = [w.reshape(-1, w.shape[-1]) for w in weights]
    specs = [pl.BlockSpec((w.shape[0] // CAST_STEPS, w.shape[1]), lambda i: (i, 0)) for w in flat]
    outs = pl.pallas_call(
        _cast_kernel,
        grid=(CAST_STEPS,),
        in_specs=specs,
        out_specs=specs,
        out_shape=[jax.ShapeDtypeStruct(w.shape, BF16) for w in flat],
        compiler_params=_params("arbitrary"),
        name="cast_weights",
    )(*flat)
    return [o.reshape(w.shape) for o, w in zip(outs, weights)]


ADALN_BLOCK = 2048


def _adaln_kernel(c_ref, w_ref, b_ref, o_ref):
    c = c_ref[...]
    s = c * jax.nn.sigmoid(c)
    o_ref[0] = jnp.dot(s.astype(BF16), w_ref[0].astype(BF16), preferred_element_type=F32) + b_ref[0]


def _adaln(cvec, w_mod, b_mod):
    nb = cvec.shape[0]
    n_out = 6 * D_MODEL
    blk = ADALN_BLOCK
    return pl.pallas_call(
        _adaln_kernel,
        grid=(DEPTH, n_out // blk),
        in_specs=[
            pl.BlockSpec((nb, D_MODEL), lambda l, n: (0, 0)),
            pl.BlockSpec((1, D_MODEL, blk), lambda l, n: (l, 0, n)),
            pl.BlockSpec((1, 1, blk), lambda l, n: (l, 0, n)),
        ],
        out_specs=pl.BlockSpec((1, nb, blk), lambda l, n: (l, 0, n)),
        out_shape=jax.ShapeDtypeStruct((DEPTH, nb, n_out), F32),
        compiler_params=_params("arbitrary", "arbitrary"),
        name="adaln",
    )(cvec, w_mod, b_mod.reshape(DEPTH, 1, n_out))


COL_CONV = 0
COL_QKV = 3 * CONV_W
COL_POOL = COL_QKV + 3 * ATTN_W
COL_GATE = COL_POOL + POOL_W
N_GATE_COLS = N_BRANCH * D_MODEL


def _inproj_kernel(x_ref, mod_ref, g_ref, w_ref, cw_ref, inv_ref, yconv_ref, qkv_ref, d_ref, gates_ref, *rest,
                   seq_len, with_kv):
    if with_kv:
        kv_ref, h_scr = rest
    else:
        (h_scr,) = rest

    def proj(col):
        return jnp.dot(h_scr[...], w_ref[:, col:col + IN_CHUNK], preferred_element_type=F32)

    parts = []
    for r in range(x_ref.shape[0] // NORM_ROWS):
        rows = slice(r * NORM_ROWS, (r + 1) * NORM_ROWS)
        h = _norm_mod(x_ref[rows, :], g_ref[...], mod_ref[0, 0:1, :], mod_ref[0, 1:2, :]).astype(BF16)
        h_scr[rows, :] = h
        parts.append(jnp.dot(h, w_ref[:, COL_CONV:COL_CONV + IN_CHUNK], preferred_element_type=F32))
    b_gate = jnp.concatenate(parts, axis=0)
    row = _row_in_seq((x_ref.shape[0], IN_CHUNK), seq_len)
    conv = _dwconv3(proj(COL_CONV + CONV_W) * proj(COL_CONV + 2 * CONV_W), cw_ref, row, seq_len)
    yconv_ref[...] = (b_gate * conv).astype(BF16)
    p = proj(COL_POOL)
    for c in range(N_GATE_COLS // IN_CHUNK):
        z = proj(COL_GATE + c * IN_CHUNK)
        if c < len(POOL_SIZES):
            lanes = slice(c * POOL_GROUP, (c + 1) * POOL_GROUP)
            d_ref[:, lanes] = _pool_delta(p[:, lanes], POOL_SIZES[c], inv_ref[:, c:c + 1], seq_len).astype(BF16)
        gates_ref[:, c * IN_CHUNK:(c + 1) * IN_CHUNK] = jax.nn.sigmoid(z).astype(BF16)
    for kv in range(2):
        z = proj(COL_QKV + (1 + kv) * ATTN_W)
        qkv_ref[:, (1 + kv) * ATTN_W:(2 + kv) * ATTN_W] = z.astype(BF16)
        if with_kv:
            for s in range(z.shape[0] // seq_len):
                for h_i in range(N_HEADS):
                    kv_ref[s, kv, h_i] = z[s * seq_len:(s + 1) * seq_len, h_i * HEAD_DIM:(h_i + 1) * HEAD_DIM]
    qkv_ref[:, 0:ATTN_W] = (proj(COL_QKV) * (HEAD_DIM ** -0.5)).astype(BF16)


def _inproj(x, mod, g1, w_in, conv_w, *, seq_len, tile, with_kv, layer):
    ntok = x.shape[0]
    row = lambda i: (i, 0)
    widths = (CONV_W, 3 * ATTN_W, POOL_W, N_GATE_COLS)
    out_shape = [jax.ShapeDtypeStruct((ntok, w), BF16) for w in widths]
    out_specs = [pl.BlockSpec((tile, w), row) for w in widths]
    if with_kv:
        one_layer = (2, N_HEADS, seq_len, HEAD_DIM)
        out_shape.append(jax.ShapeDtypeStruct((ntok // seq_len,) + one_layer, F32))
        out_specs.append(pl.BlockSpec((tile // seq_len,) + one_layer, lambda i: (i, 0, 0, 0, 0)))
    return pl.pallas_call(
        functools.partial(_inproj_kernel, seq_len=seq_len, with_kv=with_kv),
        grid=(ntok // tile,),
        in_specs=[
            pl.BlockSpec((tile, D_MODEL), row),
            _mod_spec(mod, ntok, tile),
            _resident((1, D_MODEL)),
            _resident_layer((D_MODEL, IN_W), layer),
            _resident((3, CONV_W)),
            _resident((tile, len(POOL_SIZES))),
        ],
        out_specs=out_specs,
        out_shape=out_shape,
        scratch_shapes=[pltpu.VMEM((tile, D_MODEL), BF16)],
        compiler_params=_params("arbitrary"),
        name="inproj_ctx" if with_kv else "inproj_lat",
    )(x, mod, g1, w_in, conv_w, _pool_inv_counts(tile, seq_len))


def _attend_pair(q2, kvb):
    lane = lax.broadcasted_iota(jnp.int32, (1, LANES), 1)
    out = None
    for half in range(2):
        m = (lane >= half * HEAD_DIM) & (lane < (half + 1) * HEAD_DIM)
        qh = jnp.where(m, q2, jnp.zeros_like(q2))
        scores = []
        for k2, _, bias in kvb:
            s = lax.dot_general(qh, k2, (((1,), (1,)), ((), ())), preferred_element_type=F32)
            if bias is not None:
                s = s + bias[half]
            scores.append(s)
        mx = functools.reduce(jnp.maximum, [jnp.max(s, axis=-1, keepdims=True) for s in scores])
        es = [jnp.exp(s - mx) for s in scores]
        den = functools.reduce(lambda a, b: a + b, [jnp.sum(e, axis=-1, keepdims=True) for e in es])
        acc = None
        for e, (_, v2, _) in zip(es, kvb):
            vh = jnp.where(m, v2, jnp.zeros_like(v2))
            o = jnp.dot(e.astype(BF16), vh, preferred_element_type=F32)
            acc = o if acc is None else acc + o
        acc = acc * (1.0 / den)
        out = acc if out is None else out + acc
    return out


Q_ROWS = 4
Q_BLOCK = Q_ROWS * GRID_W
N_Q_BLOCKS = GRID_ROWS // Q_ROWS
KEY_ROWS = 12
N_KEYS = KEY_ROWS * GRID_W
N_REL_ROWS = 2 * WIN_H - 1
N_ROW_PAIRS = N_REL_ROWS + 1


def _first_key_row(m):
    return min(max(Q_ROWS * m - WIN_H // 2, 0), GRID_ROWS - KEY_ROWS)


def _fill_bias(rp_ref, bias_scr):
    shape = (N_ROW_PAIRS * GRID_W, LANES)
    qc = lax.broadcasted_iota(jnp.int32, shape, 0) & (GRID_W - 1)
    lane = lax.broadcasted_iota(jnp.int32, shape, 1)
    kc = lane & (GRID_W - 1)
    col_start = jnp.clip(qc - WIN_W // 2, 0, GRID_W - WIN_W)
    valid = (kc >= col_start) & (kc < col_start + WIN_W)
    first_half = lax.broadcasted_iota(jnp.int32, (GRID_W, LANES), 1) < GRID_W
    for h in range(N_HEADS):
        t = jnp.concatenate(
            [pltpu.roll(jnp.broadcast_to(rp_ref[h, p:p + 1, :], (GRID_W, LANES)), LANES - (WIN_W - 1), 1,
                        stride=1, stride_axis=0) for p in range(N_ROW_PAIRS)], axis=0)
        t = jnp.where(valid, t, -jnp.inf)
        for m in range(N_Q_BLOCKS):
            for ql in range(Q_ROWS):
                qr = Q_ROWS * m + ql
                win0 = min(max(qr - WIN_H // 2, 0), GRID_ROWS - WIN_H)
                for u in range(KEY_ROWS // 2):
                    kr = _first_key_row(m) + 2 * u
                    in0 = win0 <= kr < win0 + WIN_H
                    in1 = win0 <= kr + 1 < win0 + WIN_H
                    if in0 or in1:
                        pair = kr - qr + WIN_H
                        tile = t[pair * GRID_W:(pair + 1) * GRID_W]
                        if not in0:
                            tile = jnp.where(first_half, -jnp.inf, tile)
                        if not in1:
                            tile = jnp.where(first_half, tile, -jnp.inf)
                    else:
                        tile = jnp.full((GRID_W, LANES), -jnp.inf, F32)
                    bias_scr[h, m, ql * GRID_W:(ql + 1) * GRID_W, u * LANES:(u + 1) * LANES] = tile


def _lat_attn_kernel(q_ref, k_ref, v_ref, cache_ref, rp_ref, *rest, n_kv, n_after):
    kv_in, o_ref, bias_scr = rest[:n_kv], rest[n_kv + n_after], rest[-1]
    if n_kv:
        kv_out = rest[n_kv + n_after + 1]
        for layer, src in enumerate(kv_in):
            kv_out[0, layer] = src[0]
    hp = pl.program_id(0)
    m = pl.program_id(1)

    @pl.when((hp == 0) & (m == 0))
    def _():
        _fill_bias(rp_ref, bias_scr)

    def attend(m_idx, first_key, n_rows, col0=0):
        keys = pl.ds(first_key, n_rows * GRID_W)
        cols = slice(col0, col0 + n_rows * GRID_W)
        for s in range(q_ref.shape[0]):
            kvb = [
                (k_ref[s, keys, :], v_ref[s, keys, :],
                 (bias_scr[2 * hp, m_idx, :, cols], bias_scr[2 * hp + 1, m_idx, :, cols])),
                (cache_ref[0, s], cache_ref[1, s], None),
            ]
            o_ref[s] = _attend_pair(q_ref[s], kvb).astype(BF16)

    last = N_Q_BLOCKS - 1
    last_row0 = GRID_ROWS - WIN_H
    pl.when(m == 0)(lambda: attend(0, 0, WIN_H))
    pl.when(m == last)(lambda: attend(last, last_row0 * GRID_W, WIN_H,
                                      col0=(last_row0 - _first_key_row(last)) * GRID_W))

    @pl.when((m > 0) & (m < last))
    def _():
        row0 = jnp.clip(Q_ROWS * m - WIN_H // 2, 0, GRID_ROWS - KEY_ROWS)
        attend(m, pl.multiple_of(row0 * GRID_W, Q_BLOCK), KEY_ROWS)


def _lat_attention(qkv, cache_kv, layer, rp, kv_layers=(), after=()):
    nseq = qkv.shape[0] // LAT_LEN
    qkv = qkv.reshape(nseq, LAT_LEN, 3 * ATTN_W)
    n_hp = ATTN_W // LANES
    in_specs = [
        pl.BlockSpec((nseq, Q_BLOCK, LANES), lambda hp, m: (0, m, hp)),
        pl.BlockSpec((nseq, LAT_LEN, LANES), lambda hp, m: (0, 0, n_hp + hp)),
        pl.BlockSpec((nseq, LAT_LEN, LANES), lambda hp, m: (0, 0, 2 * n_hp + hp)),
        pl.BlockSpec((None, 2, nseq, PAST_LEN, LANES), lambda hp, m: (layer, 0, 0, 0, hp)),
        _resident(rp.shape),
    ]
    out_specs = [pl.BlockSpec((nseq, Q_BLOCK, LANES), lambda hp, m: (0, m, hp))]
    out_shape = [jax.ShapeDtypeStruct((nseq, LAT_LEN, ATTN_W), BF16)]
    if kv_layers:
        n_ctx, *slab = kv_layers[0].shape
        assert n_ctx == n_hp * N_Q_BLOCKS
        step = lambda hp, m: hp * N_Q_BLOCKS + m
        in_specs += [pl.BlockSpec((1, *slab), lambda hp, m: (step(hp, m), 0, 0, 0, 0))] * len(kv_layers)
        out_specs.append(pl.BlockSpec((1, len(kv_layers), *slab), lambda hp, m: (step(hp, m), 0, 0, 0, 0, 0)))
        out_shape.append(jax.ShapeDtypeStruct((n_ctx, len(kv_layers), *slab), F32))
    in_specs += [pl.BlockSpec(memory_space=pltpu.HBM)] * len(after)
    outs = pl.pallas_call(
        functools.partial(_lat_attn_kernel, n_kv=len(kv_layers), n_after=len(after)),
        grid=(n_hp, N_Q_BLOCKS),
        in_specs=in_specs,
        out_specs=out_specs,
        out_shape=out_shape,
        scratch_shapes=[pltpu.VMEM((N_HEADS, N_Q_BLOCKS, Q_BLOCK, N_KEYS), F32)],
        compiler_params=_params("arbitrary", "arbitrary"),
        name="attn_lat",
    )(qkv, qkv, qkv, cache_kv, rp, *kv_layers, *after)
    y = outs[0].reshape(nseq * LAT_LEN, ATTN_W)
    return (y, outs[1]) if kv_layers else y


def _row_pair_table(rpb_l):
    lane_pad = GRID_W - rpb_l.shape[-1]
    lo = jnp.pad(rpb_l, ((0, 0), (1, 0), (0, lane_pad)))
    hi = jnp.pad(rpb_l, ((0, 0), (0, 1), (0, lane_pad)))
    return jnp.concatenate([lo, hi], axis=-1)


def _mix_kernel(x_ref, mod_ref, yc_ref, ya_ref, d_ref, gates_ref, pw_ref, ps_ref, wb_ref, wo_ref, o_ref, *scratch,
                ctx_attention):
    if ctx_attention:
        (ya_scr,) = scratch
        for s in range(x_ref.shape[0] // CTX_LEN):
            rows = slice(s * CTX_LEN, (s + 1) * CTX_LEN)
            for hp in range(N_HEADS // 2):
                q, k, v = (ya_ref[rows, i * ATTN_W + hp * LANES:i * ATTN_W + (hp + 1) * LANES] for i in range(3))
                ya_scr[rows, hp * LANES:(hp + 1) * LANES] = _attend_pair(q, [(k, v, None)]).astype(BF16)
        ya_ref = ya_scr
    yp = [jnp.dot(d_ref[:, g * POOL_GROUP:(g + 1) * POOL_GROUP], pw_ref[g], preferred_element_type=F32)
          for g in range(len(POOL_SIZES))]
    y_pool = (jnp.concatenate(yp, axis=-1) * ps_ref[...]).astype(BF16)
    merged = None
    for i, y in enumerate((yc_ref[...], ya_ref[...], y_pool)):
        proj = jnp.dot(y, wb_ref[i], preferred_element_type=F32)
        term = gates_ref[:, i * D_MODEL:(i + 1) * D_MODEL].astype(F32) * proj
        merged = term if merged is None else merged + term
    mix = jnp.dot(merged.astype(BF16), wo_ref[...], preferred_element_type=F32)
    o_ref[...] = x_ref[...] + mod_ref[0, 2:3, :] * mix


def _mix(x, mod, yconv, yattn, d, gates, pool_w, pool_scale, w_branch, w_out, *, layer, ctx_attention=False):
    ntok = x.shape[0]
    row = lambda i: (i, 0)
    return pl.pallas_call(
        functools.partial(_mix_kernel, ctx_attention=ctx_attention),
        grid=(ntok // MIX_TILE,),
        in_specs=[
            pl.BlockSpec((MIX_TILE, D_MODEL), row),
            _mod_spec(mod, ntok, MIX_TILE),
            pl.BlockSpec((MIX_TILE, CONV_W), row),
            pl.BlockSpec((MIX_TILE, yattn.shape[1]), row),
            pl.BlockSpec((MIX_TILE, POOL_W), row),
            pl.BlockSpec((MIX_TILE, N_BRANCH * D_MODEL), row),
            _resident_layer((len(POOL_SIZES), POOL_GROUP, POOL_GROUP), layer),
            _resident((1, POOL_W)),
            _resident_layer((N_BRANCH, CONV_W, D_MODEL), layer),
            _resident_layer((D_MODEL, D_MODEL), layer),
        ],
        out_specs=pl.BlockSpec((MIX_TILE, D_MODEL), row),
        out_shape=jax.ShapeDtypeStruct((ntok, D_MODEL), F32),
        scratch_shapes=[pltpu.VMEM((MIX_TILE, ATTN_W), BF16)] if ctx_attention else [],
        compiler_params=_params("arbitrary"),
        name="mix",
    )(x, mod, yconv, yattn, d, gates, pool_w, pool_scale, w_branch, w_out)


def _ffn_kernel(x_ref, mod_ref, g2_ref, wu_ref, fc_ref, wd_ref, gf_ref, o_ref, h_scr, act_scr,
                *, seq_len, final):
    u0, val0 = [], []
    for r in range(x_ref.shape[0] // NORM_ROWS):
        rows = slice(r * NORM_ROWS, (r + 1) * NORM_ROWS)
        h = _norm_mod(x_ref[rows, :], g2_ref[...], mod_ref[0, 3:4, :], mod_ref[0, 4:5, :]).astype(BF16)
        h_scr[rows, :] = h
        u0.append(jnp.dot(h, wu_ref[:, 0:FF_CHUNK], preferred_element_type=F32))
        val0.append(jnp.dot(h, wu_ref[:, D_FF:D_FF + FF_CHUNK], preferred_element_type=F32))
    row = _row_in_seq((x_ref.shape[0], FF_CHUNK), seq_len)
    for c in range(N_FF_CHUNKS):
        sl = slice(c * FF_CHUNK, (c + 1) * FF_CHUNK)
        sl_val = slice(D_FF + c * FF_CHUNK, D_FF + (c + 1) * FF_CHUNK)
        if c == 0:
            u, val = jnp.concatenate(u0, axis=0), jnp.concatenate(val0, axis=0)
        else:
            u = jnp.dot(h_scr[...], wu_ref[:, sl], preferred_element_type=F32)
            val = jnp.dot(h_scr[...], wu_ref[:, sl_val], preferred_element_type=F32)
        u = _dwconv3(u, fc_ref.at[:, sl], row, seq_len)
        act_scr[:, sl] = (jax.nn.gelu(u, approximate=True) * val).astype(BF16)
    y = jnp.dot(act_scr[...], wd_ref[...], preferred_element_type=F32)
    xn = x_ref[...] + mod_ref[0, 5:6, :] * y
    if final:
        ms = jnp.mean(xn * xn, axis=-1, keepdims=True)
        xn = xn * lax.rsqrt(ms + EPS) * gf_ref[...]
    o_ref[...] = xn


def _ffn(x, mod, g2, w_up, f_conv, w_down, g_final, *, seq_len, final, layer):
    ntok = x.shape[0]
    return pl.pallas_call(
        functools.partial(_ffn_kernel, seq_len=seq_len, final=final),
        grid=(ntok // TOK_TILE,),
        in_specs=[
            pl.BlockSpec((TOK_TILE, D_MODEL), lambda i: (i, 0)),
            _mod_spec(mod, ntok, TOK_TILE),
            _resident((1, D_MODEL)),
            _resident_layer((D_MODEL, 2 * D_FF), layer),
            _resident((3, D_FF)),
            _resident_layer((D_FF, D_MODEL), layer),
            _resident((1, D_MODEL)),
        ],
        out_specs=pl.BlockSpec((TOK_TILE, D_MODEL), lambda i: (i, 0)),
        out_shape=jax.ShapeDtypeStruct((ntok, D_MODEL), F32),
        scratch_shapes=[pltpu.VMEM((TOK_TILE, D_MODEL), BF16), pltpu.VMEM((TOK_TILE, D_FF), BF16)],
        compiler_params=_params("arbitrary"),
        name="ffn",
    )(x, mod, g2, w_up, f_conv, w_down, g_final)


def kernel(x_prompt, x_sample, cache_kv, c, c_ctx, w_mod, b_mod, g_norm1, g_norm2, w_in, conv_w, rpb,
           pool_w, pool_scale, w_branch, w_out, ffn_w_up, ffn_conv, ffn_w_down, g_final):
    n_mod = 8
    cvec = jnp.concatenate([c_ctx[None, :], c, jnp.zeros((n_mod - 1 - N_LAT_SEQ, D_MODEL), F32)], axis=0)
    mod = _adaln(cvec, w_mod, b_mod).reshape(DEPTH, n_mod, 6, D_MODEL)

    xp = x_prompt.reshape(N_CTX_SEQ * CTX_LEN, D_MODEL)
    xs = x_sample.reshape(N_LAT_SEQ * LAT_LEN, D_MODEL)
    gf = g_final.reshape(1, D_MODEL)
    cache_tok = cache_kv.transpose(1, 2, 0, 4, 3, 5).reshape(DEPTH, 2, N_LAT_SEQ, PAST_LEN, ATTN_W).astype(BF16)
    w_in_b, pool_w_b, w_br_b, w_out_b, w_up_b, w_dn_b = _cast_weights(
        w_in, pool_w, w_branch, w_out, ffn_w_up, ffn_w_down)
    kv_layers = []
    kv_state = None
    for l in range(DEPTH):
        pool_s_l = pool_scale[l].reshape(1, POOL_W)
        g1 = g_norm1[l].reshape(1, D_MODEL)
        g2 = g_norm2[l].reshape(1, D_MODEL)
        mod_ctx = mod[l, 0:1]
        mod_lat = mod[l, 1:1 + N_LAT_SEQ]
        final = l == DEPTH - 1

        yc, qkv, d, gates, kv = _inproj(xp, mod_ctx, g1, w_in_b, conv_w[l], seq_len=CTX_LEN, tile=CTX_IN_TILE,
                                        with_kv=True, layer=l)
        kv_layers.append(kv)
        xp = _mix(xp, mod_ctx, yc, qkv, d, gates, pool_w_b, pool_s_l, w_br_b, w_out_b, layer=l, ctx_attention=True)
        x_ctx_mixed = xp
        xp = _ffn(xp, mod_ctx, g2, w_up_b, ffn_conv[l], w_dn_b, gf, seq_len=CTX_LEN, final=final, layer=l)

        yc, qkv, d, gates = _inproj(xs, mod_lat, g1, w_in_b, conv_w[l], seq_len=LAT_LEN, tile=TOK_TILE,
                                    with_kv=False, layer=l)
        if final:
            ya, kv_state = _lat_attention(qkv, cache_tok, l, _row_pair_table(rpb[l]), tuple(kv_layers),
                                          after=(x_ctx_mixed,))
        else:
            ya = _lat_attention(qkv, cache_tok, l, _row_pair_table(rpb[l]))
        xs = _mix(xs, mod_lat, yc, ya, d, gates, pool_w_b, pool_s_l, w_br_b, w_out_b, layer=l)
        xs = _ffn(xs, mod_lat, g2, w_up_b, ffn_conv[l], w_dn_b, gf, seq_len=LAT_LEN, final=final, layer=l)

    y_prompt = xp.reshape(N_CTX_SEQ, CTX_LEN, D_MODEL)
    y_sample = xs.reshape(N_LAT_SEQ, LAT_LEN, D_MODEL)
    return (y_prompt, y_sample, kv_state)
```

```python
import functools

import numpy as np
import jax
import jax.numpy as jnp
from jax import lax
from jax.experimental import pallas as pl
from jax.experimental.pallas import tpu as pltpu

F32 = jnp.float32
BF16 = jnp.bfloat16

D_MODEL = 1024
N_CTX_SEQ = 16
CTX_LEN = 256
DEPTH = 2
N_LAT_SEQ = 4
LAT_LEN = 1024
PAST_LEN = 256
GRID_W = 64
GRID_ROWS = LAT_LEN // GRID_W
CONV_W = 512
N_HEADS = 8
HEAD_DIM = 64
ATTN_W = N_HEADS * HEAD_DIM
POOL_W = 512
POOL_SIZES = (2, 4, 8, 16)
POOL_GROUP = POOL_W // len(POOL_SIZES)
N_BRANCH = 3
WIN_H = 8
WIN_W = 16
D_FF = 2816
EPS = 1e-6
IN_W = 3 * CONV_W + 3 * ATTN_W + POOL_W + N_BRANCH * D_MODEL

LANES = 128
TOK_TILE = 1024
CTX_IN_TILE = 512
IN_CHUNK = 512
FF_CHUNK = 256
N_FF_CHUNKS = D_FF // FF_CHUNK
MIX_TILE = 1024
NORM_ROWS = 256
VMEM_LIMIT = 56 * 1024 * 1024


def _params(*sem):
    return pltpu.CompilerParams(dimension_semantics=sem, vmem_limit_bytes=VMEM_LIMIT)


def _resident(shape):
    return pl.BlockSpec(shape, lambda *_: (0,) * len(shape), pipeline_mode=pl.Buffered(1))


def _resident_layer(shape, layer):
    return pl.BlockSpec((None,) + shape, lambda *_: (layer,) + (0,) * len(shape), pipeline_mode=pl.Buffered(1))


def _mod_spec(mod, ntok, tile):
    tiles_per_mod = ntok // mod.shape[0] // tile
    return pl.BlockSpec((1, 6, D_MODEL), lambda i, *_: (i // tiles_per_mod, 0, 0))


def _norm_mod(x, g, shift, scale):
    ms = jnp.mean(x * x, axis=-1, keepdims=True)
    return (x * lax.rsqrt(ms + EPS) * g) * (1.0 + scale) + shift


def _row_in_seq(shape, seq_len):
    return lax.broadcasted_iota(jnp.int32, shape, 0) & (seq_len - 1)


def _shift_rows(x, s, row, seq_len):
    y = pltpu.roll(x, s % x.shape[0], 0)
    ok = (row >= s) if s > 0 else (row < seq_len + s)
    return jnp.where(ok, y, 0.0)


def _dwconv3(x, w_ref, row, seq_len):
    return (_shift_rows(x, 1, row, seq_len) * w_ref[0:1, :] + x * w_ref[1:2, :]
            + _shift_rows(x, -1, row, seq_len) * w_ref[2:3, :])


def _pool_inv_counts(tile, seq_len):
    r = np.arange(tile) % seq_len
    cols = []
    for w in POOL_SIZES:
        lo = np.maximum(r - w // 2, 0)
        hi = np.minimum(r - w // 2 + w, seq_len)
        cols.append(1.0 / (hi - lo))
    return jnp.asarray(np.stack(cols, axis=1), F32)


def _pool_delta(pg, w, inv_cnt, seq_len):
    row = _row_in_seq(pg.shape, seq_len)
    fwd = bwd = pg
    k = 1
    while k < w // 2:
        fwd = fwd + _shift_rows(fwd, -k, row, seq_len)
        bwd = bwd + _shift_rows(bwd, k, row, seq_len)
        k *= 2
    acc = fwd + _shift_rows(bwd, 1, row, seq_len)
    return acc * inv_cnt - pg


CAST_STEPS = 16


def _cast_kernel(*refs):
    n = len(refs) // 2
    for src, dst in zip(refs[:n], refs[n:]):
        dst[...] = src[...].astype(BF16)


def _cast_weights(*weights):
    flat = [w.reshape(-1, w.shape[-1]) for w in weights]
    specs = [pl.BlockSpec((w.shape[0] // CAST_STEPS, w.shape[1]), lambda i: (i, 0)) for w in flat]
    outs = pl.pallas_call(
        _cast_kernel,
        grid=(CAST_STEPS,),
        in_specs=specs,
        out_specs=specs,
        out_shape=[jax.ShapeDtypeStruct(w.shape, BF16) for w in flat],
        compiler_params=_params("arbitrary"),
        name="cast_weights",
    )(*flat)
    return [o.reshape(w.shape) for o, w in zip(outs, weights)]


ADALN_BLOCK = 2048


def _adaln_kernel(c_ref, w_ref, b_ref, o_ref):
    c = c_ref[...]
    s = c * jax.nn.sigmoid(c)
    o_ref[0] = jnp.dot(s.astype(BF16), w_ref[0].astype(BF16), preferred_element_type=F32) + b_ref[0]


def _adaln(cvec, w_mod, b_mod):
    nb = cvec.shape[0]
    n_out = 6 * D_MODEL
    blk = ADALN_BLOCK
    return pl.pallas_call(
        _adaln_kernel,
        grid=(DEPTH, n_out // blk),
        in_specs=[
            pl.BlockSpec((nb, D_MODEL), lambda l, n: (0, 0)),
            pl.BlockSpec((1, D_MODEL, blk), lambda l, n: (l, 0, n)),
            pl.BlockSpec((1, 1, blk), lambda l, n: (l, 0, n)),
        ],
        out_specs=pl.BlockSpec((1, nb, blk), lambda l, n: (l, 0, n)),
        out_shape=jax.ShapeDtypeStruct((DEPTH, nb, n_out), F32),
        compiler_params=_params("arbitrary", "arbitrary"),
        name="adaln",
    )(cvec, w_mod, b_mod.reshape(DEPTH, 1, n_out))


COL_CONV = 0
COL_QKV = 3 * CONV_W
COL_POOL = COL_QKV + 3 * ATTN_W
COL_GATE = COL_POOL + POOL_W
N_GATE_COLS = N_BRANCH * D_MODEL


def _inproj_kernel(x_ref, mod_ref, g_ref, w_ref, cw_ref, inv_ref, yconv_ref, qkv_ref, d_ref, gates_ref, *rest,
                   seq_len, with_kv):
    if with_kv:
        kv_ref, h_scr = rest
    else:
        (h_scr,) = rest

    def proj(col):
        return jnp.dot(h_scr[...], w_ref[:, col:col + IN_CHUNK], preferred_element_type=F32)

    parts = []
    for r in range(x_ref.shape[0] // NORM_ROWS):
        rows = slice(r * NORM_ROWS, (r + 1) * NORM_ROWS)
        h = _norm_mod(x_ref[rows, :], g_ref[...], mod_ref[0, 0:1, :], mod_ref[0, 1:2, :]).astype(BF16)
        h_scr[rows, :] = h
        parts.append(jnp.dot(h, w_ref[:, COL_CONV:COL_CONV + IN_CHUNK], preferred_element_type=F32))
    b_gate = jnp.concatenate(parts, axis=0)
    row = _row_in_seq((x_ref.shape[0], IN_CHUNK), seq_len)
    conv = _dwconv3(proj(COL_CONV + CONV_W) * proj(COL_CONV + 2 * CONV_W), cw_ref, row, seq_len)
    yconv_ref[...] = (b_gate * conv).astype(BF16)
    p = proj(COL_POOL)
    for c in range(N_GATE_COLS // IN_CHUNK):
        z = proj(COL_GATE + c * IN_CHUNK)
        if c < len(POOL_SIZES):
            lanes = slice(c * POOL_GROUP, (c + 1) * POOL_GROUP)
            d_ref[:, lanes] = _pool_delta(p[:, lanes], POOL_SIZES[c], inv_ref[:, c:c + 1], seq_len).astype(BF16)
        gates_ref[:, c * IN_CHUNK:(c + 1) * IN_CHUNK] = jax.nn.sigmoid(z).astype(BF16)
    for kv in range(2):
        z = proj(COL_QKV + (1 + kv) * ATTN_W)
        qkv_ref[:, (1 + kv) * ATTN_W:(2 + kv) * ATTN_W] = z.astype(BF16)
        if with_kv:
            for s in range(z.shape[0] // seq_len):
                for h_i in range(N_HEADS):
                    kv_ref[s, kv, h_i] = z[s * seq_len:(s + 1) * seq_len, h_i * HEAD_DIM:(h_i + 1) * HEAD_DIM]
    qkv_ref[:, 0:ATTN_W] = (proj(COL_QKV) * (HEAD_DIM ** -0.5)).astype(BF16)


def _inproj(x, mod, g1, w_in, conv_w, *, seq_len, tile, with_kv, layer):
    ntok = x.shape[0]
    row = lambda i: (i, 0)
    widths = (CONV_W, 3 * ATTN_W, POOL_W, N_GATE_COLS)
    out_shape = [jax.ShapeDtypeStruct((ntok, w), BF16) for w in widths]
    out_specs = [pl.BlockSpec((tile, w), row) for w in widths]
    if with_kv:
        one_layer = (2, N_HEADS, seq_len, HEAD_DIM)
        out_shape.append(jax.ShapeDtypeStruct((ntok // seq_len,) + one_layer, F32))
        out_specs.append(pl.BlockSpec((tile // seq_len,) + one_layer, lambda i: (i, 0, 0, 0, 0)))
    return pl.pallas_call(
        functools.partial(_inproj_kernel, seq_len=seq_len, with_kv=with_kv),
        grid=(ntok // tile,),
        in_specs=[
            pl.BlockSpec((tile, D_MODEL), row),
            _mod_spec(mod, ntok, tile),
            _resident((1, D_MODEL)),
            _resident_layer((D_MODEL, IN_W), layer),
            _resident((3, CONV_W)),
            _resident((tile, len(POOL_SIZES))),
        ],
        out_specs=out_specs,
        out_shape=out_shape,
        scratch_shapes=[pltpu.VMEM((tile, D_MODEL), BF16)],
        compiler_params=_params("arbitrary"),
        name="inproj_ctx" if with_kv else "inproj_lat",
    )(x, mod, g1, w_in, conv_w, _pool_inv_counts(tile, seq_len))


def _attend_pair(q2, kvb):
    lane = lax.broadcasted_iota(jnp.int32, (1, LANES), 1)
    out = None
    for half in range(2):
        m = (lane >= half * HEAD_DIM) & (lane < (half + 1) * HEAD_DIM)
        qh = jnp.where(m, q2, jnp.zeros_like(q2))
        scores = []
        for k2, _, bias in kvb:
            s = lax.dot_general(qh, k2, (((1,), (1,)), ((), ())), preferred_element_type=F32)
            if bias is not None:
                s = s + bias[half]
            scores.append(s)
        mx = functools.reduce(jnp.maximum, [jnp.max(s, axis=-1, keepdims=True) for s in scores])
        es = [jnp.exp(s - mx) for s in scores]
        den = functools.reduce(lambda a, b: a + b, [jnp.sum(e, axis=-1, keepdims=True) for e in es])
        acc = None
        for e, (_, v2, _) in zip(es, kvb):
            vh = jnp.where(m, v2, jnp.zeros_like(v2))
            o = jnp.dot(e.astype(BF16), vh, preferred_element_type=F32)
            acc = o if acc is None else acc + o
        acc = acc * (1.0 / den)
        out = acc if out is None else out + acc
    return out


Q_ROWS = 4
Q_BLOCK = Q_ROWS * GRID_W
N_Q_BLOCKS = GRID_ROWS // Q_ROWS
KEY_ROWS = 12
N_KEYS = KEY_ROWS * GRID_W
N_REL_ROWS = 2 * WIN_H - 1
N_ROW_PAIRS = N_REL_ROWS + 1


def _first_key_row(m):
    return min(max(Q_ROWS * m - WIN_H // 2, 0), GRID_ROWS - KEY_ROWS)


def _fill_bias(rp_ref, bias_scr):
    shape = (N_ROW_PAIRS * GRID_W, LANES)
    qc = lax.broadcasted_iota(jnp.int32, shape, 0) & (GRID_W - 1)
    lane = lax.broadcasted_iota(jnp.int32, shape, 1)
    kc = lane & (GRID_W - 1)
    col_start = jnp.clip(qc - WIN_W // 2, 0, GRID_W - WIN_W)
    valid = (kc >= col_start) & (kc < col_start + WIN_W)
    first_half = lax.broadcasted_iota(jnp.int32, (GRID_W, LANES), 1) < GRID_W
    for h in range(N_HEADS):
        t = jnp.concatenate(
            [pltpu.roll(jnp.broadcast_to(rp_ref[h, p:p + 1, :], (GRID_W, LANES)), LANES - (WIN_W - 1), 1,
                        stride=1, stride_axis=0) for p in range(N_ROW_PAIRS)], axis=0)
        t = jnp.where(valid, t, -jnp.inf)
        for m in range(N_Q_BLOCKS):
            for ql in range(Q_ROWS):
                qr = Q_ROWS * m + ql
                win0 = min(max(qr - WIN_H // 2, 0), GRID_ROWS - WIN_H)
                for u in range(KEY_ROWS // 2):
                    kr = _first_key_row(m) + 2 * u
                    in0 = win0 <= kr < win0 + WIN_H
                    in1 = win0 <= kr + 1 < win0 + WIN_H
                    if in0 or in1:
                        pair = kr - qr + WIN_H
                        tile = t[pair * GRID_W:(pair + 1) * GRID_W]
                        if not in0:
                            tile = jnp.where(first_half, -jnp.inf, tile)
                        if not in1:
                            tile = jnp.where(first_half, tile, -jnp.inf)
                    else:
                        tile = jnp.full((GRID_W, LANES), -jnp.inf, F32)
                    bias_scr[h, m, ql * GRID_W:(ql + 1) * GRID_W, u * LANES:(u + 1) * LANES] = tile


def _lat_attn_kernel(q_ref, k_ref, v_ref, cache_ref, rp_ref, *rest, n_kv, n_after):
    kv_in, o_ref, bias_scr = rest[:n_kv], rest[n_kv + n_after], rest[-1]
    if n_kv:
        kv_out = rest[n_kv + n_after + 1]
        for layer, src in enumerate(kv_in):
            kv_out[0, layer] = src[0]
    hp = pl.program_id(0)
    m = pl.program_id(1)

    @pl.when((hp == 0) & (m == 0))
    def _():
        _fill_bias(rp_ref, bias_scr)

    def attend(m_idx, first_key, n_rows, col0=0):
        keys = pl.ds(first_key, n_rows * GRID_W)
        cols = slice(col0, col0 + n_rows * GRID_W)
        for s in range(q_ref.shape[0]):
            kvb = [
                (k_ref[s, keys, :], v_ref[s, keys, :],
                 (bias_scr[2 * hp, m_idx, :, cols], bias_scr[2 * hp + 1, m_idx, :, cols])),
                (cache_ref[0, s], cache_ref[1, s], None),
            ]
            o_ref[s] = _attend_pair(q_ref[s], kvb).astype(BF16)

    last = N_Q_BLOCKS - 1
    last_row0 = GRID_ROWS - WIN_H
    pl.when(m == 0)(lambda: attend(0, 0, WIN_H))
    pl.when(m == last)(lambda: attend(last, last_row0 * GRID_W, WIN_H,
                                      col0=(last_row0 - _first_key_row(last)) * GRID_W))

    @pl.when((m > 0) & (m < last))
    def _():
        row0 = jnp.clip(Q_ROWS * m - WIN_H // 2, 0, GRID_ROWS - KEY_ROWS)
        attend(m, pl.multiple_of(row0 * GRID_W, Q_BLOCK), KEY_ROWS)


def _lat_attention(qkv, cache_kv, layer, rp, kv_layers=(), after=()):
    nseq = qkv.shape[0] // LAT_LEN
    qkv = qkv.reshape(nseq, LAT_LEN, 3 * ATTN_W)
    n_hp = ATTN_W // LANES
    in_specs = [
        pl.BlockSpec((nseq, Q_BLOCK, LANES), lambda hp, m: (0, m, hp)),
        pl.BlockSpec((nseq, LAT_LEN, LANES), lambda hp, m: (0, 0, n_hp + hp)),
        pl.BlockSpec((nseq, LAT_LEN, LANES), lambda hp, m: (0, 0, 2 * n_hp + hp)),
        pl.BlockSpec((None, 2, nseq, PAST_LEN, LANES), lambda hp, m: (layer, 0, 0, 0, hp)),
        _resident(rp.shape),
    ]
    out_specs = [pl.BlockSpec((nseq, Q_BLOCK, LANES), lambda hp, m: (0, m, hp))]
    out_shape = [jax.ShapeDtypeStruct((nseq, LAT_LEN, ATTN_W), BF16)]
    if kv_layers:
        n_ctx, *slab = kv_layers[0].shape
        assert n_ctx == n_hp * N_Q_BLOCKS
        step = lambda hp, m: hp * N_Q_BLOCKS + m
        in_specs += [pl.BlockSpec((1, *slab), lambda hp, m: (step(hp, m), 0, 0, 0, 0))] * len(kv_layers)
        out_specs.append(pl.BlockSpec((1, len(kv_layers), *slab), lambda hp, m: (step(hp, m), 0, 0, 0, 0, 0)))
        out_shape.append(jax.ShapeDtypeStruct((n_ctx, len(kv_layers), *slab), F32))
    in_specs += [pl.BlockSpec(memory_space=pltpu.HBM)] * len(after)
    outs = pl.pallas_call(
        functools.partial(_lat_attn_kernel, n_kv=len(kv_layers), n_after=len(after)),
        grid=(n_hp, N_Q_BLOCKS),
        in_specs=in_specs,
        out_specs=out_specs,
        out_shape=out_shape,
        scratch_shapes=[pltpu.VMEM((N_HEADS, N_Q_BLOCKS, Q_BLOCK, N_KEYS), F32)],
        compiler_params=_params("arbitrary", "arbitrary"),
        name="attn_lat",
    )(qkv, qkv, qkv, cache_kv, rp, *kv_layers, *after)
    y = outs[0].reshape(nseq * LAT_LEN, ATTN_W)
    return (y, outs[1]) if kv_layers else y


def _row_pair_table(rpb_l):
    lane_pad = GRID_W - rpb_l.shape[-1]
    lo = jnp.pad(rpb_l, ((0, 0), (1, 0), (0, lane_pad)))
    hi = jnp.pad(rpb_l, ((0, 0), (0, 1), (0, lane_pad)))
    return jnp.concatenate([lo, hi], axis=-1)


def _mix_kernel(x_ref, mod_ref, yc_ref, ya_ref, d_ref, gates_ref, pw_ref, ps_ref, wb_ref, wo_ref, o_ref, *scratch,
                ctx_attention):
    if ctx_attention:
        (ya_scr,) = scratch
        for s in range(x_ref.shape[0] // CTX_LEN):
            rows = slice(s * CTX_LEN, (s + 1) * CTX_LEN)
            for hp in range(N_HEADS // 2):
                q, k, v = (ya_ref[rows, i * ATTN_W + hp * LANES:i * ATTN_W + (hp + 1) * LANES] for i in range(3))
                ya_scr[rows, hp * LANES:(hp + 1) * LANES] = _attend_pair(q, [(k, v, None)]).astype(BF16)
        ya_ref = ya_scr
    yp = [jnp.dot(d_ref[:, g * POOL_GROUP:(g + 1) * POOL_GROUP], pw_ref[g], preferred_element_type=F32)
          for g in range(len(POOL_SIZES))]
    y_pool = (jnp.concatenate(yp, axis=-1) * ps_ref[...]).astype(BF16)
    merged = None
    for i, y in enumerate((yc_ref[...], ya_ref[...], y_pool)):
        proj = jnp.dot(y, wb_ref[i], preferred_element_type=F32)
        term = gates_ref[:, i * D_MODEL:(i + 1) * D_MODEL].astype(F32) * proj
        merged = term if merged is None else merged + term
    mix = jnp.dot(merged.astype(BF16), wo_ref[...], preferred_element_type=F32)
    o_ref[...] = x_ref[...] + mod_ref[0, 2:3, :] * mix


def _mix(x, mod, yconv, yattn, d, gates, pool_w, pool_scale, w_branch, w_out, *, layer, ctx_attention=False):
    ntok = x.shape[0]
    row = lambda i: (i, 0)
    return pl.pallas_call(
        functools.partial(_mix_kernel, ctx_attention=ctx_attention),
        grid=(ntok // MIX_TILE,),
        in_specs=[
            pl.BlockSpec((MIX_TILE, D_MODEL), row),
            _mod_spec(mod, ntok, MIX_TILE),
            pl.BlockSpec((MIX_TILE, CONV_W), row),
            pl.BlockSpec((MIX_TILE, yattn.shape[1]), row),
            pl.BlockSpec((MIX_TILE, POOL_W), row),
            pl.BlockSpec((MIX_TILE, N_BRANCH * D_MODEL), row),
            _resident_layer((len(POOL_SIZES), POOL_GROUP, POOL_GROUP), layer),
            _resident((1, POOL_W)),
            _resident_layer((N_BRANCH, CONV_W, D_MODEL), layer),
            _resident_layer((D_MODEL, D_MODEL), layer),
        ],
        out_specs=pl.BlockSpec((MIX_TILE, D_MODEL), row),
        out_shape=jax.ShapeDtypeStruct((ntok, D_MODEL), F32),
        scratch_shapes=[pltpu.VMEM((MIX_TILE, ATTN_W), BF16)] if ctx_attention else [],
        compiler_params=_params("arbitrary"),
        name="mix",
    )(x, mod, yconv, yattn, d, gates, pool_w, pool_scale, w_branch, w_out)


def _ffn_kernel(x_ref, mod_ref, g2_ref, wu_ref, fc_ref, wd_ref, gf_ref, o_ref, h_scr, act_scr,
                *, seq_len, final):
    u0, val0 = [], []
    for r in range(x_ref.shape[0] // NORM_ROWS):
        rows = slice(r * NORM_ROWS, (r + 1) * NORM_ROWS)
        h = _norm_mod(x_ref[rows, :], g2_ref[...], mod_ref[0, 3:4, :], mod_ref[0, 4:5, :]).astype(BF16)
        h_scr[rows, :] = h
        u0.append(jnp.dot(h, wu_ref[:, 0:FF_CHUNK], preferred_element_type=F32))
        val0.append(jnp.dot(h, wu_ref[:, D_FF:D_FF + FF_CHUNK], preferred_element_type=F32))
    row = _row_in_seq((x_ref.shape[0], FF_CHUNK), seq_len)
    for c in range(N_FF_CHUNKS):
        sl = slice(c * FF_CHUNK, (c + 1) * FF_CHUNK)
        sl_val = slice(D_FF + c * FF_CHUNK, D_FF + (c + 1) * FF_CHUNK)
        if c == 0:
            u, val = jnp.concatenate(u0, axis=0), jnp.concatenate(val0, axis=0)
        else:
            u = jnp.dot(h_scr[...], wu_ref[:, sl], preferred_element_type=F32)
            val = jnp.dot(h_scr[...], wu_ref[:, sl_val], preferred_element_type=F32)
        u = _dwconv3(u, fc_ref.at[:, sl], row, seq_len)
        act_scr[:, sl] = (jax.nn.gelu(u, approximate=True) * val).astype(BF16)
    y = jnp.dot(act_scr[...], wd_ref[...], preferred_element_type=F32)
    xn = x_ref[...] + mod_ref[0, 5:6, :] * y
    if final:
        ms = jnp.mean(xn * xn, axis=-1, keepdims=True)
        xn = xn * lax.rsqrt(ms + EPS) * gf_ref[...]
    o_ref[...] = xn


def _ffn(x, mod, g2, w_up, f_conv, w_down, g_final, *, seq_len, final, layer):
    ntok = x.shape[0]
    return pl.pallas_call(
        functools.partial(_ffn_kernel, seq_len=seq_len, final=final),
        grid=(ntok // TOK_TILE,),
        in_specs=[
            pl.BlockSpec((TOK_TILE, D_MODEL), lambda i: (i, 0)),
            _mod_spec(mod, ntok, TOK_TILE),
            _resident((1, D_MODEL)),
            _resident_layer((D_MODEL, 2 * D_FF), layer),
            _resident((3, D_FF)),
            _resident_layer((D_FF, D_MODEL), layer),
            _resident((1, D_MODEL)),
        ],
        out_specs=pl.BlockSpec((TOK_TILE, D_MODEL), lambda i: (i, 0)),
        out_shape=jax.ShapeDtypeStruct((ntok, D_MODEL), F32),
        scratch_shapes=[pltpu.VMEM((TOK_TILE, D_MODEL), BF16), pltpu.VMEM((TOK_TILE, D_FF), BF16)],
        compiler_params=_params("arbitrary"),
        name="ffn",
    )(x, mod, g2, w_up, f_conv, w_down, g_final)


def kernel(x_prompt, x_sample, cache_kv, c, c_ctx, w_mod, b_mod, g_norm1, g_norm2, w_in, conv_w, rpb,
           pool_w, pool_scale, w_branch, w_out, ffn_w_up, ffn_conv, ffn_w_down, g_final):
    n_mod = 8
    cvec = jnp.concatenate([c_ctx[None, :], c, jnp.zeros((n_mod - 1 - N_LAT_SEQ, D_MODEL), F32)], axis=0)
    mod = _adaln(cvec, w_mod, b_mod).reshape(DEPTH, n_mod, 6, D_MODEL)

    xp = x_prompt.reshape(N_CTX_SEQ * CTX_LEN, D_MODEL)
    xs = x_sample.reshape(N_LAT_SEQ * LAT_LEN, D_MODEL)
    gf = g_final.reshape(1, D_MODEL)
    cache_tok = cache_kv.transpose(1, 2, 0, 4, 3, 5).reshape(DEPTH, 2, N_LAT_SEQ, PAST_LEN, ATTN_W).astype(BF16)
    w_in_b, pool_w_b, w_br_b, w_out_b, w_up_b, w_dn_b = _cast_weights(
        w_in, pool_w, w_branch, w_out, ffn_w_up, ffn_w_down)
    kv_layers = []
    kv_state = None
    for l in range(DEPTH):
        pool_s_l = pool_scale[l].reshape(1, POOL_W)
        g1 = g_norm1[l].reshape(1, D_MODEL)
        g2 = g_norm2[l].reshape(1, D_MODEL)
        mod_ctx = mod[l, 0:1]
        mod_lat = mod[l, 1:1 + N_LAT_SEQ]
        final = l == DEPTH - 1

        yc, qkv, d, gates, kv = _inproj(xp, mod_ctx, g1, w_in_b, conv_w[l], seq_len=CTX_LEN, tile=CTX_IN_TILE,
                                        with_kv=True, layer=l)
        kv_layers.append(kv)
        xp = _mix(xp, mod_ctx, yc, qkv, d, gates, pool_w_b, pool_s_l, w_br_b, w_out_b, layer=l, ctx_attention=True)
        xp = _ffn(xp, mod_ctx, g2, w_up_b, ffn_conv[l], w_dn_b, gf, seq_len=CTX_LEN, final=final, layer=l)
        x_ctx_mixed = xp

        yc, qkv, d, gates = _inproj(xs, mod_lat, g1, w_in_b, conv_w[l], seq_len=LAT_LEN, tile=TOK_TILE,
                                    with_kv=False, layer=l)
        if final:
            ya, kv_state = _lat_attention(qkv, cache_tok, l, _row_pair_table(rpb[l]), tuple(kv_layers),
                                          after=(x_ctx_mixed,))
        else:
            ya = _lat_attention(qkv, cache_tok, l, _row_pair_table(rpb[l]))
        xs = _mix(xs, mod_lat, yc, ya, d, gates, pool_w_b, pool_s_l, w_br_b, w_out_b, layer=l)
        xs = _ffn(xs, mod_lat, g2, w_up_b, ffn_conv[l], w_dn_b, gf, seq_len=LAT_LEN, final=final, layer=l)

    y_prompt = xp.reshape(N_CTX_SEQ, CTX_LEN, D_MODEL)
    y_sample = xs.reshape(N_LAT_SEQ, LAT_LEN, D_MODEL)
    return (y_prompt, y_sample, kv_state)
```

```python
import functools

import numpy as np
import jax
import jax.numpy as jnp
from jax import lax
from jax.experimental import pallas as pl
from jax.experimental.pallas import tpu as pltpu

F32 = jnp.float32
BF16 = jnp.bfloat16

D_MODEL = 1024
N_CTX_SEQ = 16
CTX_LEN = 256
DEPTH = 2
N_LAT_SEQ = 4
LAT_LEN = 1024
PAST_LEN = 256
GRID_W = 64
GRID_ROWS = LAT_LEN // GRID_W
CONV_W = 512
N_HEADS = 8
HEAD_DIM = 64
ATTN_W = N_HEADS * HEAD_DIM
POOL_W = 512
POOL_SIZES = (2, 4, 8, 16)
POOL_GROUP = POOL_W // len(POOL_SIZES)
N_BRANCH = 3
WIN_H = 8
WIN_W = 16
D_FF = 2816
EPS = 1e-6
IN_W = 3 * CONV_W + 3 * ATTN_W + POOL_W + N_BRANCH * D_MODEL

LANES = 128
TOK_TILE = 1024
CTX_IN_TILE = 512
IN_CHUNK = 512
FF_CHUNK = 256
N_FF_CHUNKS = D_FF // FF_CHUNK
MIX_TILE = 1024
NORM_ROWS = 256
VMEM_LIMIT = 56 * 1024 * 1024


def _params(*sem):
    return pltpu.CompilerParams(dimension_semantics=sem, vmem_limit_bytes=VMEM_LIMIT)


def _resident(shape):
    return pl.BlockSpec(shape, lambda *_: (0,) * len(shape), pipeline_mode=pl.Buffered(1))


def _resident_layer(shape, layer):
    return pl.BlockSpec((None,) + shape, lambda *_: (layer,) + (0,) * len(shape), pipeline_mode=pl.Buffered(1))


def _mod_spec(mod, ntok, tile):
    tiles_per_mod = ntok // mod.shape[0] // tile
    return pl.BlockSpec((1, 6, D_MODEL), lambda i, *_: (i // tiles_per_mod, 0, 0))


def _norm_mod(x, g, shift, scale):
    ms = jnp.mean(x * x, axis=-1, keepdims=True)
    return (x * lax.rsqrt(ms + EPS) * g) * (1.0 + scale) + shift


def _row_in_seq(shape, seq_len):
    return lax.broadcasted_iota(jnp.int32, shape, 0) & (seq_len - 1)


def _shift_rows(x, s, row, seq_len):
    y = pltpu.roll(x, s % x.shape[0], 0)
    ok = (row >= s) if s > 0 else (row < seq_len + s)
    return jnp.where(ok, y, 0.0)


def _dwconv3(x, w_ref, row, seq_len):
    return (_shift_rows(x, 1, row, seq_len) * w_ref[0:1, :] + x * w_ref[1:2, :]
            + _shift_rows(x, -1, row, seq_len) * w_ref[2:3, :])


def _pool_inv_counts(tile, seq_len):
    r = np.arange(tile) % seq_len
    cols = []
    for w in POOL_SIZES:
        lo = np.maximum(r - w // 2, 0)
        hi = np.minimum(r - w // 2 + w, seq_len)
        cols.append(1.0 / (hi - lo))
    return jnp.asarray(np.stack(cols, axis=1), F32)


def _pool_delta(pg, w, inv_cnt, seq_len):
    row = _row_in_seq(pg.shape, seq_len)
    fwd = bwd = pg
    k = 1
    while k < w // 2:
        fwd = fwd + _shift_rows(fwd, -k, row, seq_len)
        bwd = bwd + _shift_rows(bwd, k, row, seq_len)
        k *= 2
    acc = fwd + _shift_rows(bwd, 1, row, seq_len)
    return acc * inv_cnt - pg


CAST_STEPS = 16


def _cast_kernel(*refs):
    n = len(refs) // 2
    for src, dst in zip(refs[:n], refs[n:]):
        dst[...] = src[...].astype(BF16)


def _cast_weights(*weights):
    flat = [w.reshape(-1, w.shape[-1]) for w in weights]
    specs = [pl.BlockSpec((w.shape[0] // CAST_STEPS, w.shape[1]), lambda i: (i, 0)) for w in flat]
    outs = pl.pallas_call(
        _cast_kernel,
        grid=(CAST_STEPS,),
        in_specs=specs,
        out_specs=specs,
        out_shape=[jax.ShapeDtypeStruct(w.shape, BF16) for w in flat],
        compiler_params=_params("arbitrary"),
        name="cast_weights",
    )(*flat)
    return [o.reshape(w.shape) for o, w in zip(outs, weights)]


ADALN_BLOCK = 2048


def _adaln_kernel(c_ref, w_ref, b_ref, o_ref):
    c = c_ref[...]
    s = c * jax.nn.sigmoid(c)
    o_ref[0] = jnp.dot(s.astype(BF16), w_ref[0].astype(BF16), preferred_element_type=F32) + b_ref[0]


def _adaln(cvec, w_mod, b_mod):
    nb = cvec.shape[0]
    n_out = 6 * D_MODEL
    blk = ADALN_BLOCK
    return pl.pallas_call(
        _adaln_kernel,
        grid=(DEPTH, n_out // blk),
        in_specs=[
            pl.BlockSpec((nb, D_MODEL), lambda l, n: (0, 0)),
            pl.BlockSpec((1, D_MODEL, blk), lambda l, n: (l, 0, n)),
            pl.BlockSpec((1, 1, blk), lambda l, n: (l, 0, n)),
        ],
        out_specs=pl.BlockSpec((1, nb, blk), lambda l, n: (l, 0, n)),
        out_shape=jax.ShapeDtypeStruct((DEPTH, nb, n_out), F32),
        compiler_params=_params("arbitrary", "arbitrary"),
        name="adaln",
    )(cvec, w_mod, b_mod.reshape(DEPTH, 1, n_out))


COL_CONV = 0
COL_QKV = 3 * CONV_W
COL_POOL = COL_QKV + 3 * ATTN_W
COL_GATE = COL_POOL + POOL_W
N_GATE_COLS = N_BRANCH * D_MODEL


def _inproj_kernel(x_ref, mod_ref, g_ref, w_ref, cw_ref, inv_ref, yconv_ref, qkv_ref, d_ref, gates_ref, *rest,
                   seq_len, with_kv):
    if with_kv:
        kv_ref, h_scr = rest
    else:
        (h_scr,) = rest

    def proj(col):
        return jnp.dot(h_scr[...], w_ref[:, col:col + IN_CHUNK], preferred_element_type=F32)

    parts = []
    for r in range(x_ref.shape[0] // NORM_ROWS):
        rows = slice(r * NORM_ROWS, (r + 1) * NORM_ROWS)
        h = _norm_mod(x_ref[rows, :], g_ref[...], mod_ref[0, 0:1, :], mod_ref[0, 1:2, :]).astype(BF16)
        h_scr[rows, :] = h
        parts.append(jnp.dot(h, w_ref[:, COL_CONV:COL_CONV + IN_CHUNK], preferred_element_type=F32))
    b_gate = jnp.concatenate(parts, axis=0)
    row = _row_in_seq((x_ref.shape[0], IN_CHUNK), seq_len)
    conv = _dwconv3(proj(COL_CONV + CONV_W) * proj(COL_CONV + 2 * CONV_W), cw_ref, row, seq_len)
    yconv_ref[...] = (b_gate * conv).astype(BF16)
    p = proj(COL_POOL)
    for c in range(N_GATE_COLS // IN_CHUNK):
        z = proj(COL_GATE + c * IN_CHUNK)
        if c < len(POOL_SIZES):
            lanes = slice(c * POOL_GROUP, (c + 1) * POOL_GROUP)
            d_ref[:, lanes] = _pool_delta(p[:, lanes], POOL_SIZES[c], inv_ref[:, c:c + 1], seq_len).astype(BF16)
        gates_ref[:, c * IN_CHUNK:(c + 1) * IN_CHUNK] = z.astype(BF16)
    for kv in range(2):
        z = proj(COL_QKV + (1 + kv) * ATTN_W)
        qkv_ref[:, (1 + kv) * ATTN_W:(2 + kv) * ATTN_W] = z.astype(BF16)
        if with_kv:
            for s in range(z.shape[0] // seq_len):
                for h_i in range(N_HEADS):
                    kv_ref[s, kv, h_i] = z[s * seq_len:(s + 1) * seq_len, h_i * HEAD_DIM:(h_i + 1) * HEAD_DIM]
    qkv_ref[:, 0:ATTN_W] = (proj(COL_QKV) * (HEAD_DIM ** -0.5)).astype(BF16)


def _inproj(x, mod, g1, w_in, conv_w, *, seq_len, tile, with_kv, layer):
    ntok = x.shape[0]
    row = lambda i: (i, 0)
    widths = (CONV_W, 3 * ATTN_W, POOL_W, N_GATE_COLS)
    out_shape = [jax.ShapeDtypeStruct((ntok, w), BF16) for w in widths]
    out_specs = [pl.BlockSpec((tile, w), row) for w in widths]
    if with_kv:
        one_layer = (2, N_HEADS, seq_len, HEAD_DIM)
        out_shape.append(jax.ShapeDtypeStruct((ntok // seq_len,) + one_layer, F32))
        out_specs.append(pl.BlockSpec((tile // seq_len,) + one_layer, lambda i: (i, 0, 0, 0, 0)))
    return pl.pallas_call(
        functools.partial(_inproj_kernel, seq_len=seq_len, with_kv=with_kv),
        grid=(ntok // tile,),
        in_specs=[
            pl.BlockSpec((tile, D_MODEL), row),
            _mod_spec(mod, ntok, tile),
            _resident((1, D_MODEL)),
            _resident_layer((D_MODEL, IN_W), layer),
            _resident((3, CONV_W)),
            _resident((tile, len(POOL_SIZES))),
        ],
        out_specs=out_specs,
        out_shape=out_shape,
        scratch_shapes=[pltpu.VMEM((tile, D_MODEL), BF16)],
        compiler_params=_params("arbitrary"),
        name="inproj_ctx" if with_kv else "inproj_lat",
    )(x, mod, g1, w_in, conv_w, _pool_inv_counts(tile, seq_len))


def _attend_pair(q2, kvb):
    lane = lax.broadcasted_iota(jnp.int32, (1, LANES), 1)
    out = None
    for half in range(2):
        m = (lane >= half * HEAD_DIM) & (lane < (half + 1) * HEAD_DIM)
        qh = jnp.where(m, q2, jnp.zeros_like(q2))
        scores = []
        for k2, _, bias in kvb:
            s = lax.dot_general(qh, k2, (((1,), (1,)), ((), ())), preferred_element_type=F32)
            if bias is not None:
                s = s + bias[half]
            scores.append(s)
        mx = functools.reduce(jnp.maximum, [jnp.max(s, axis=-1, keepdims=True) for s in scores])
        es = [jnp.exp(s - mx) for s in scores]
        den = functools.reduce(lambda a, b: a + b, [jnp.sum(e, axis=-1, keepdims=True) for e in es])
        acc = None
        for e, (_, v2, _) in zip(es, kvb):
            vh = jnp.where(m, v2, jnp.zeros_like(v2))
            o = jnp.dot(e.astype(BF16), vh, preferred_element_type=F32)
            acc = o if acc is None else acc + o
        acc = acc * (1.0 / den)
        out = acc if out is None else out + acc
    return out


Q_ROWS = 4
Q_BLOCK = Q_ROWS * GRID_W
N_Q_BLOCKS = GRID_ROWS // Q_ROWS
KEY_ROWS = 12
N_KEYS = KEY_ROWS * GRID_W
N_REL_ROWS = 2 * WIN_H - 1
N_ROW_PAIRS = N_REL_ROWS + 1


def _first_key_row(m):
    return min(max(Q_ROWS * m - WIN_H // 2, 0), GRID_ROWS - KEY_ROWS)


def _fill_bias(rp_ref, bias_scr):
    shape = (N_ROW_PAIRS * GRID_W, LANES)
    qc = lax.broadcasted_iota(jnp.int32, shape, 0) & (GRID_W - 1)
    lane = lax.broadcasted_iota(jnp.int32, shape, 1)
    kc = lane & (GRID_W - 1)
    col_start = jnp.clip(qc - WIN_W // 2, 0, GRID_W - WIN_W)
    valid = (kc >= col_start) & (kc < col_start + WIN_W)
    first_half = lax.broadcasted_iota(jnp.int32, (GRID_W, LANES), 1) < GRID_W
    for h in range(N_HEADS):
        t = jnp.concatenate(
            [pltpu.roll(jnp.broadcast_to(rp_ref[h, p:p + 1, :], (GRID_W, LANES)), LANES - (WIN_W - 1), 1,
                        stride=1, stride_axis=0) for p in range(N_ROW_PAIRS)], axis=0)
        t = jnp.where(valid, t, -jnp.inf)
        for m in range(N_Q_BLOCKS):
            for ql in range(Q_ROWS):
                qr = Q_ROWS * m + ql
                win0 = min(max(qr - WIN_H // 2, 0), GRID_ROWS - WIN_H)
                for u in range(KEY_ROWS // 2):
                    kr = _first_key_row(m) + 2 * u
                    in0 = win0 <= kr < win0 + WIN_H
                    in1 = win0 <= kr + 1 < win0 + WIN_H
                    if in0 or in1:
                        pair = kr - qr + WIN_H
                        tile = t[pair * GRID_W:(pair + 1) * GRID_W]
                        if not in0:
                            tile = jnp.where(first_half, -jnp.inf, tile)
                        if not in1:
                            tile = jnp.where(first_half, tile, -jnp.inf)
                    else:
                        tile = jnp.full((GRID_W, LANES), -jnp.inf, F32)
                    bias_scr[h, m, ql * GRID_W:(ql + 1) * GRID_W, u * LANES:(u + 1) * LANES] = tile


def _lat_attn_kernel(q_ref, k_ref, v_ref, cache_ref, rp_ref, *rest, n_kv, n_after):
    kv_in, o_ref, bias_scr = rest[:n_kv], rest[n_kv + n_after], rest[-1]
    if n_kv:
        kv_out = rest[n_kv + n_after + 1]
        for layer, src in enumerate(kv_in):
            kv_out[0, layer] = src[0]
    hp = pl.program_id(0)
    m = pl.program_id(1)

    @pl.when((hp == 0) & (m == 0))
    def _():
        _fill_bias(rp_ref, bias_scr)

    def attend(m_idx, first_key, n_rows, col0=0):
        keys = pl.ds(first_key, n_rows * GRID_W)
        cols = slice(col0, col0 + n_rows * GRID_W)
        for s in range(q_ref.shape[0]):
            kvb = [
                (k_ref[s, keys, :], v_ref[s, keys, :],
                 (bias_scr[2 * hp, m_idx, :, cols], bias_scr[2 * hp + 1, m_idx, :, cols])),
                (cache_ref[0, s], cache_ref[1, s], None),
            ]
            o_ref[s] = _attend_pair(q_ref[s], kvb).astype(BF16)

    last = N_Q_BLOCKS - 1
    last_row0 = GRID_ROWS - WIN_H
    pl.when(m == 0)(lambda: attend(0, 0, WIN_H))
    pl.when(m == last)(lambda: attend(last, last_row0 * GRID_W, WIN_H,
                                      col0=(last_row0 - _first_key_row(last)) * GRID_W))

    @pl.when((m > 0) & (m < last))
    def _():
        row0 = jnp.clip(Q_ROWS * m - WIN_H // 2, 0, GRID_ROWS - KEY_ROWS)
        attend(m, pl.multiple_of(row0 * GRID_W, Q_BLOCK), KEY_ROWS)


def _lat_attention(qkv, cache_kv, layer, rp, kv_layers=(), after=()):
    nseq = qkv.shape[0] // LAT_LEN
    qkv = qkv.reshape(nseq, LAT_LEN, 3 * ATTN_W)
    n_hp = ATTN_W // LANES
    in_specs = [
        pl.BlockSpec((nseq, Q_BLOCK, LANES), lambda hp, m: (0, m, hp)),
        pl.BlockSpec((nseq, LAT_LEN, LANES), lambda hp, m: (0, 0, n_hp + hp)),
        pl.BlockSpec((nseq, LAT_LEN, LANES), lambda hp, m: (0, 0, 2 * n_hp + hp)),
        pl.BlockSpec((None, 2, nseq, PAST_LEN, LANES), lambda hp, m: (layer, 0, 0, 0, hp)),
        _resident(rp.shape),
    ]
    out_specs = [pl.BlockSpec((nseq, Q_BLOCK, LANES), lambda hp, m: (0, m, hp))]
    out_shape = [jax.ShapeDtypeStruct((nseq, LAT_LEN, ATTN_W), BF16)]
    if kv_layers:
        n_ctx, *slab = kv_layers[0].shape
        assert n_ctx == n_hp * N_Q_BLOCKS
        step = lambda hp, m: hp * N_Q_BLOCKS + m
        in_specs += [pl.BlockSpec((1, *slab), lambda hp, m: (step(hp, m), 0, 0, 0, 0))] * len(kv_layers)
        out_specs.append(pl.BlockSpec((1, len(kv_layers), *slab), lambda hp, m: (step(hp, m), 0, 0, 0, 0, 0)))
        out_shape.append(jax.ShapeDtypeStruct((n_ctx, len(kv_layers), *slab), F32))
    in_specs += [pl.BlockSpec(memory_space=pltpu.HBM)] * len(after)
    outs = pl.pallas_call(
        functools.partial(_lat_attn_kernel, n_kv=len(kv_layers), n_after=len(after)),
        grid=(n_hp, N_Q_BLOCKS),
        in_specs=in_specs,
        out_specs=out_specs,
        out_shape=out_shape,
        scratch_shapes=[pltpu.VMEM((N_HEADS, N_Q_BLOCKS, Q_BLOCK, N_KEYS), F32)],
        compiler_params=_params("arbitrary", "arbitrary"),
        name="attn_lat",
    )(qkv, qkv, qkv, cache_kv, rp, *kv_layers, *after)
    y = outs[0].reshape(nseq * LAT_LEN, ATTN_W)
    return (y, outs[1]) if kv_layers else y


def _row_pair_table(rpb_l):
    lane_pad = GRID_W - rpb_l.shape[-1]
    lo = jnp.pad(rpb_l, ((0, 0), (1, 0), (0, lane_pad)))
    hi = jnp.pad(rpb_l, ((0, 0), (0, 1), (0, lane_pad)))
    return jnp.concatenate([lo, hi], axis=-1)


def _mix_kernel(x_ref, mod_ref, yc_ref, ya_ref, d_ref, gates_ref, pw_ref, ps_ref, wb_ref, wo_ref, o_ref, *scratch,
                ctx_attention):
    if ctx_attention:
        (ya_scr,) = scratch
        for s in range(x_ref.shape[0] // CTX_LEN):
            rows = slice(s * CTX_LEN, (s + 1) * CTX_LEN)
            for hp in range(N_HEADS // 2):
                q, k, v = (ya_ref[rows, i * ATTN_W + hp * LANES:i * ATTN_W + (hp + 1) * LANES] for i in range(3))
                ya_scr[rows, hp * LANES:(hp + 1) * LANES] = _attend_pair(q, [(k, v, None)]).astype(BF16)
        ya_ref = ya_scr
    yp = [jnp.dot(d_ref[:, g * POOL_GROUP:(g + 1) * POOL_GROUP], pw_ref[g], preferred_element_type=F32)
          for g in range(len(POOL_SIZES))]
    y_pool = (jnp.concatenate(yp, axis=-1) * ps_ref[...]).astype(BF16)
    merged = None
    for i, y in enumerate((yc_ref[...], ya_ref[...], y_pool)):
        proj = jnp.dot(y, wb_ref[i], preferred_element_type=F32)
        term = jax.nn.sigmoid(gates_ref[:, i * D_MODEL:(i + 1) * D_MODEL].astype(F32)) * proj
        merged = term if merged is None else merged + term
    mix = jnp.dot(merged.astype(BF16), wo_ref[...], preferred_element_type=F32)
    o_ref[...] = x_ref[...] + mod_ref[0, 2:3, :] * mix


def _mix(x, mod, yconv, yattn, d, gates, pool_w, pool_scale, w_branch, w_out, *, layer, ctx_attention=False):
    ntok = x.shape[0]
    row = lambda i: (i, 0)
    return pl.pallas_call(
        functools.partial(_mix_kernel, ctx_attention=ctx_attention),
        grid=(ntok // MIX_TILE,),
        in_specs=[
            pl.BlockSpec((MIX_TILE, D_MODEL), row),
            _mod_spec(mod, ntok, MIX_TILE),
            pl.BlockSpec((MIX_TILE, CONV_W), row),
            pl.BlockSpec((MIX_TILE, yattn.shape[1]), row),
            pl.BlockSpec((MIX_TILE, POOL_W), row),
            pl.BlockSpec((MIX_TILE, N_BRANCH * D_MODEL), row),
            _resident_layer((len(POOL_SIZES), POOL_GROUP, POOL_GROUP), layer),
            _resident((1, POOL_W)),
            _resident_layer((N_BRANCH, CONV_W, D_MODEL), layer),
            _resident_layer((D_MODEL, D_MODEL), layer),
        ],
        out_specs=pl.BlockSpec((MIX_TILE, D_MODEL), row),
        out_shape=jax.ShapeDtypeStruct((ntok, D_MODEL), F32),
        scratch_shapes=[pltpu.VMEM((MIX_TILE, ATTN_W), BF16)] if ctx_attention else [],
        compiler_params=_params("arbitrary"),
        name="mix",
    )(x, mod, yconv, yattn, d, gates, pool_w, pool_scale, w_branch, w_out)


def _ffn_kernel(x_ref, mod_ref, g2_ref, wu_ref, fc_ref, wd_ref, gf_ref, o_ref, h_scr, act_scr,
                *, seq_len, final):
    u0, val0 = [], []
    for r in range(x_ref.shape[0] // NORM_ROWS):
        rows = slice(r * NORM_ROWS, (r + 1) * NORM_ROWS)
        h = _norm_mod(x_ref[rows, :], g2_ref[...], mod_ref[0, 3:4, :], mod_ref[0, 4:5, :]).astype(BF16)
        h_scr[rows, :] = h
        u0.append(jnp.dot(h, wu_ref[:, 0:FF_CHUNK], preferred_element_type=F32))
        val0.append(jnp.dot(h, wu_ref[:, D_FF:D_FF + FF_CHUNK], preferred_element_type=F32))
    row = _row_in_seq((x_ref.shape[0], FF_CHUNK), seq_len)
    for c in range(N_FF_CHUNKS):
        sl = slice(c * FF_CHUNK, (c + 1) * FF_CHUNK)
        sl_val = slice(D_FF + c * FF_CHUNK, D_FF + (c + 1) * FF_CHUNK)
        if c == 0:
            u, val = jnp.concatenate(u0, axis=0), jnp.concatenate(val0, axis=0)
        else:
            u = jnp.dot(h_scr[...], wu_ref[:, sl], preferred_element_type=F32)
            val = jnp.dot(h_scr[...], wu_ref[:, sl_val], preferred_element_type=F32)
        u = _dwconv3(u, fc_ref.at[:, sl], row, seq_len)
        act_scr[:, sl] = (jax.nn.gelu(u, approximate=True) * val).astype(BF16)
    y = jnp.dot(act_scr[...], wd_ref[...], preferred_element_type=F32)
    xn = x_ref[...] + mod_ref[0, 5:6, :] * y
    if final:
        ms = jnp.mean(xn * xn, axis=-1, keepdims=True)
        xn = xn * lax.rsqrt(ms + EPS) * gf_ref[...]
    o_ref[...] = xn


def _ffn(x, mod, g2, w_up, f_conv, w_down, g_final, *, seq_len, final, layer):
    ntok = x.shape[0]
    return pl.pallas_call(
        functools.partial(_ffn_kernel, seq_len=seq_len, final=final),
        grid=(ntok // TOK_TILE,),
        in_specs=[
            pl.BlockSpec((TOK_TILE, D_MODEL), lambda i: (i, 0)),
            _mod_spec(mod, ntok, TOK_TILE),
            _resident((1, D_MODEL)),
            _resident_layer((D_MODEL, 2 * D_FF), layer),
            _resident((3, D_FF)),
            _resident_layer((D_FF, D_MODEL), layer),
            _resident((1, D_MODEL)),
        ],
        out_specs=pl.BlockSpec((TOK_TILE, D_MODEL), lambda i: (i, 0)),
        out_shape=jax.ShapeDtypeStruct((ntok, D_MODEL), F32),
        scratch_shapes=[pltpu.VMEM((TOK_TILE, D_MODEL), BF16), pltpu.VMEM((TOK_TILE, D_FF), BF16)],
        compiler_params=_params("arbitrary"),
        name="ffn",
    )(x, mod, g2, w_up, f_conv, w_down, g_final)


def kernel(x_prompt, x_sample, cache_kv, c, c_ctx, w_mod, b_mod, g_norm1, g_norm2, w_in, conv_w, rpb,
           pool_w, pool_scale, w_branch, w_out, ffn_w_up, ffn_conv, ffn_w_down, g_final):
    n_mod = 8
    cvec = jnp.concatenate([c_ctx[None, :], c, jnp.zeros((n_mod - 1 - N_LAT_SEQ, D_MODEL), F32)], axis=0)
    mod = _adaln(cvec, w_mod, b_mod).reshape(DEPTH, n_mod, 6, D_MODEL)

    xp = x_prompt.reshape(N_CTX_SEQ * CTX_LEN, D_MODEL)
    xs = x_sample.reshape(N_LAT_SEQ * LAT_LEN, D_MODEL)
    gf = g_final.reshape(1, D_MODEL)
    cache_tok = cache_kv.transpose(1, 2, 0, 4, 3, 5).reshape(DEPTH, 2, N_LAT_SEQ, PAST_LEN, ATTN_W).astype(BF16)
    w_in_b, pool_w_b, w_br_b, w_out_b, w_up_b, w_dn_b = _cast_weights(
        w_in, pool_w, w_branch, w_out, ffn_w_up, ffn_w_down)
    kv_layers = []
    kv_state = None
    for l in range(DEPTH):
        pool_s_l = pool_scale[l].reshape(1, POOL_W)
        g1 = g_norm1[l].reshape(1, D_MODEL)
        g2 = g_norm2[l].reshape(1, D_MODEL)
        mod_ctx = mod[l, 0:1]
        mod_lat = mod[l, 1:1 + N_LAT_SEQ]
        final = l == DEPTH - 1

        yc, qkv, d, gates, kv = _inproj(xp, mod_ctx, g1, w_in_b, conv_w[l], seq_len=CTX_LEN, tile=CTX_IN_TILE,
                                        with_kv=True, layer=l)
        kv_layers.append(kv)
        xp = _mix(xp, mod_ctx, yc, qkv, d, gates, pool_w_b, pool_s_l, w_br_b, w_out_b, layer=l, ctx_attention=True)
        x_ctx_mixed = xp
        xp = _ffn(xp, mod_ctx, g2, w_up_b, ffn_conv[l], w_dn_b, gf, seq_len=CTX_LEN, final=final, layer=l)

        yc, qkv, d, gates = _inproj(xs, mod_lat, g1, w_in_b, conv_w[l], seq_len=LAT_LEN, tile=TOK_TILE,
                                    with_kv=False, layer=l)
        if final:
            ya, kv_state = _lat_attention(qkv, cache_tok, l, _row_pair_table(rpb[l]), tuple(kv_layers),
                                          after=(x_ctx_mixed,))
        else:
            ya = _lat_attention(qkv, cache_tok, l, _row_pair_table(rpb[l]))
        xs = _mix(xs, mod_lat, yc, ya, d, gates, pool_w_b, pool_s_l, w_br_b, w_out_b, layer=l)
        xs = _ffn(xs, mod_lat, g2, w_up_b, ffn_conv[l], w_dn_b, gf, seq_len=LAT_LEN, final=final, layer=l)

    y_prompt = xp.reshape(N_CTX_SEQ, CTX_LEN, D_MODEL)
    y_sample = xs.reshape(N_LAT_SEQ, LAT_LEN, D_MODEL)
    return (y_prompt, y_sample, kv_state)
```
